```python
import jax, jax.numpy as jnp
from jax import lax
import numpy as np

D_MODEL = 1024
BATCH = 8
SEQ = 2048
DEPTH = 2

POOL_WINDOWS = (2, 4, 8, 16)
POOL_GROUPS = len(POOL_WINDOWS)
POOL_GROUP_DIM = D_MODEL // 8
POOL_WIDTH = POOL_GROUPS * POOL_GROUP_DIM
HGRN_HEAD_DIM = 128
HGRN_HEADS = D_MODEL // HGRN_HEAD_DIM
HGRN_WIDTH = HGRN_HEADS * HGRN_HEAD_DIM
CHUNK = 64
NORM_EPS = 1e-6
IN_SIZES = (POOL_WIDTH, POOL_WIDTH, HGRN_WIDTH, HGRN_WIDTH, HGRN_WIDTH, HGRN_WIDTH, D_MODEL, D_MODEL)
IN_WIDTH = sum(IN_SIZES)

kernel_name = "hybrid_pool_hgrn2_gated_block"


def rms_norm(x, g):
    xf = x.astype(jnp.float32)
    y = xf * lax.rsqrt(jnp.mean(xf * xf, axis=-1, keepdims=True) + NORM_EPS)
    return (y * g.astype(jnp.float32)).astype(x.dtype)


def multiscale_pool(u):
    b, s, _ = u.shape
    uf = u.astype(jnp.float32)
    csum = lax.cumsum(uf, axis=1)
    pos = jnp.arange(s, dtype=jnp.float32) + 1.0
    outs = []
    for gi, w in enumerate(POOL_WINDOWS):
        sl = slice(gi * POOL_GROUP_DIM, (gi + 1) * POOL_GROUP_DIM)
        cg = csum[:, :, sl]
        prev = jnp.pad(cg, ((0, 0), (w, 0), (0, 0)))[:, :s]
        count = jnp.minimum(pos, float(w))[None, :, None]
        outs.append((cg - prev) / count - uf[:, :, sl])
    return jnp.stack(outs, axis=2)


def _hgrn2_chunk_step(state, inp):
    q, k, v, logf = inp
    cum = jnp.cumsum(logf, axis=2)
    o_inter = jnp.einsum('bhtk,bhkv->bhtv', q * jnp.exp(cum), state)
    c = q.shape[2]
    causal = jnp.tril(jnp.ones((c, c), dtype=bool))[:, :, None]
    diff = cum[:, :, :, None, :] - cum[:, :, None, :, :]
    decay = jnp.where(causal, jnp.exp(jnp.minimum(diff, 0.0)), 0.0)
    scores = jnp.sum(q[:, :, :, None, :] * k[:, :, None, :, :] * decay, axis=-1)
    o = o_inter + jnp.einsum('bhts,bhsv->bhtv', scores, v)
    last = cum[:, :, -1:, :]
    new_state = (jnp.exp(last[:, :, 0, :])[..., None] * state
                 + jnp.einsum('bhsk,bhsv->bhkv', k * jnp.exp(last - cum), v))
    return new_state, o


def hgrn2(q, k, v, logf):
    b, s, h, dk = q.shape
    n = s // CHUNK

    def to_chunks(t):
        return t.reshape(b, n, CHUNK, h, t.shape[-1]).transpose(1, 0, 3, 2, 4)

    state0 = jnp.zeros((b, h, dk, v.shape[-1]), jnp.float32)
    _, o = lax.scan(_hgrn2_chunk_step, state0,
                    (to_chunks(q), to_chunks(k), to_chunks(v), to_chunks(logf)))
    return o.transpose(1, 0, 3, 2, 4).reshape(b, s, h, v.shape[-1])


def setup_inputs(seed: int = 0) -> dict:
    key = jax.random.key(seed)
    ks = jax.random.split(key, 16)
    L, D = DEPTH, D_MODEL
    nrm = jax.random.normal
    return {
        "x": nrm(ks[0], (BATCH, SEQ, D), jnp.float32),
        "c": nrm(ks[1], (BATCH, D), jnp.float32),
        "w_ada": nrm(ks[2], (L, D, 3 * D), jnp.float32) * (0.5 * D ** -0.5),
        "b_ada": nrm(ks[3], (L, 3 * D), jnp.float32) * 0.02,
        "g_pre": 1.0 + 0.02 * nrm(ks[4], (L, D), jnp.float32),
        "g_post": 1.0 + 0.02 * nrm(ks[5], (L, D), jnp.float32),
        "w_in": nrm(ks[6], (L, D, IN_WIDTH), jnp.float32) * D ** -0.5,
        "pool_w": nrm(ks[7], (L, POOL_GROUPS, POOL_GROUP_DIM, POOL_GROUP_DIM), jnp.float32) * POOL_GROUP_DIM ** -0.5,
        "pool_scale": 1.0 + 0.02 * nrm(ks[8], (L, POOL_WIDTH), jnp.float32),
        "lb_logits": nrm(ks[9], (L, HGRN_WIDTH), jnp.float32),
        "hgrn_norm_g": 1.0 + 0.02 * nrm(ks[10], (L, HGRN_HEAD_DIM), jnp.float32),
        "w_pool_o": nrm(ks[11], (L, POOL_WIDTH, D), jnp.float32) * POOL_WIDTH ** -0.5,
        "w_hgrn_o": nrm(ks[12], (L, HGRN_WIDTH, D), jnp.float32) * HGRN_WIDTH ** -0.5,
        "w_out": nrm(ks[13], (L, D, D), jnp.float32) * D ** -0.5,
    }


def reference(x, c, w_ada, b_ada, g_pre, g_post, w_in, pool_w, pool_scale, lb_logits,
              hgrn_norm_g, w_pool_o, w_hgrn_o, w_out):
    b, s, d = x.shape
    p = jax.nn.softmax(lb_logits.astype(jnp.float32), axis=0)
    lower_bounds = jnp.cumsum(p, axis=0) - p[0:1]
    split_idx = np.cumsum(IN_SIZES)[:-1].tolist()
    c_act = jax.nn.silu(c)
    for l in range(DEPTH):
        ada = c_act @ w_ada[l] + b_ada[l]
        shift, scale, gate = jnp.split(ada[:, None, :], 3, axis=-1)
        h = rms_norm(x, g_pre[l]) * (1.0 + scale) + shift
        (pv, pg, hq, hf, hi, hg, mg_pool, mg_hgrn) = jnp.split(h @ w_in[l], split_idx, axis=-1)

        pooled = multiscale_pool(pv)
        pooled = jnp.einsum('bsgc,gcd->bsgd', pooled, pool_w[l].astype(jnp.float32))
        pooled = pooled.reshape(b, s, POOL_WIDTH) * pool_scale[l]
        branch_a = (pooled.astype(x.dtype) * jax.nn.silu(pg)) @ w_pool_o[l]

        shp = (b, s, HGRN_HEADS, HGRN_HEAD_DIM)
        lb = jnp.clip(lower_bounds[l], 0.0, 1.0).reshape(HGRN_HEADS, HGRN_HEAD_DIM)
        zf = hf.astype(jnp.float32).reshape(shp)
        f = lb + (1.0 - lb) * jax.nn.sigmoid(zf)
        logf = jnp.log(jnp.maximum(f, 1e-30))
        k = 1.0 - f
        q = jax.nn.silu(hq.astype(jnp.float32)).reshape(shp)
        v = hi.astype(jnp.float32).reshape(shp)
        o = hgrn2(q, k, v, logf)
        o = rms_norm(o, hgrn_norm_g[l]).astype(x.dtype).reshape(b, s, HGRN_WIDTH)
        branch_b = (o * jax.nn.silu(hg)) @ w_hgrn_o[l]

        merged = jax.nn.sigmoid(mg_pool) * branch_a + jax.nn.sigmoid(mg_hgrn) * branch_b
        y = merged @ w_out[l]
        x = x + gate * rms_norm(y, g_post[l])
    return x
```

```python
import functools

import jax
import jax.numpy as jnp
from jax import lax
from jax.experimental import pallas as pl
from jax.experimental.pallas import tpu as pltpu

D_MODEL = 1024
POOL_WINDOWS = (2, 4, 8, 16)
POOL_GROUPS = len(POOL_WINDOWS)
POOL_GROUP_DIM = D_MODEL // 8
POOL_WIDTH = POOL_GROUPS * POOL_GROUP_DIM
HEAD_DIM = 128
HEADS = D_MODEL // HEAD_DIM
HGRN_WIDTH = HEADS * HEAD_DIM
NORM_EPS = 1e-6
F_FLOOR = 1e-30
IN_SIZES = (POOL_WIDTH, POOL_WIDTH, HGRN_WIDTH, HGRN_WIDTH, HGRN_WIDTH, HGRN_WIDTH, D_MODEL, D_MODEL)
IN_WIDTH = sum(IN_SIZES)
(OFF_PV, OFF_PG, OFF_HQ, OFF_HF, OFF_HI, OFF_HG, OFF_MGP, OFF_MGH) = (
    sum(IN_SIZES[:i]) for i in range(len(IN_SIZES)))

SEQ_TILE = 256
CHUNK = 64
HALO = max(POOL_WINDOWS)
ADA_TILE = 512
VMEM_LIMIT_BYTES = 56 * 1024 * 1024

NT = (((1,), (1,)), ((), ()))
TN = (((0,), (0,)), ((), ()))


def _bdot(a, b):
    return jnp.dot(a.astype(jnp.bfloat16), b.astype(jnp.bfloat16),
                   preferred_element_type=jnp.float32)


def _bdot_general(a, b, dims):
    return lax.dot_general(a.astype(jnp.bfloat16), b.astype(jnp.bfloat16), dims,
                           preferred_element_type=jnp.float32)


def _sigmoid(z):
    return 1.0 / (1.0 + jnp.exp(-z))


def _silu(z):
    return z * _sigmoid(z)


def _rms(x):
    return x * lax.rsqrt(jnp.mean(x * x, axis=-1, keepdims=True) + NORM_EPS)


def _roll_rows(a, shift):
    return pltpu.roll(a, shift % a.shape[0], axis=0)


def _ada_kernel(c_ref, w_ref, b_ref, o_ref):
    c = c_ref[...]
    o_ref[...] = jnp.dot(_silu(c), w_ref[...], preferred_element_type=jnp.float32) + b_ref[...]


def _half_totals(pre, n, rowi):
    low = _roll_rows(pre, 1)
    up = _roll_rows(pre, -n)
    k = 1
    while k < n:
        bit = (rowi & k) != 0
        low = jnp.where(bit, _roll_rows(low, k), low)
        up = jnp.where(bit, up, _roll_rows(up, -k))
        k *= 2
    return low, up


def _hgrn_chunk(q, f, v, state, rowi, blk_eq):
    kk = 1.0 - f
    pre = jnp.maximum(f, F_FLOOR)
    suf = jnp.ones_like(f)
    scores = jnp.where(blk_eq[0], _bdot_general(q, kk, NT), 0.0)
    n = 1
    level = 1
    while n < CHUNK:
        upper = (rowi & n) != 0
        qn = jnp.where(upper, q * pre, 0.0)
        kn = jnp.where(upper, 0.0, kk * suf)
        p = _bdot_general(qn, kn, NT)
        scores = scores + (p if 2 * n == CHUNK else jnp.where(blk_eq[level], p, 0.0))
        low, up = _half_totals(pre, n, rowi)
        suf = jnp.where(upper, suf, suf * up)
        pre = jnp.where(upper, pre * low, pre)
        n *= 2
        level += 1
    o = _bdot(q * pre, state) + _bdot(scores, v)
    total = jnp.broadcast_to(pre[CHUNK - 1:CHUNK, :], (HEAD_DIM, HEAD_DIM)).T
    new_state = total * state + _bdot_general(kk * suf, v, TN)
    return o, new_state


def _layer_kernel(x_ref, ada_ref, gpre_ref, gpost_ref, lbl_ref, pscale_ref, hng_ref,
                  win_ref, poolw_ref, wpo_ref, who_ref, wout_ref, o_ref,
                  q_ref, f_ref, v_ref, g_ref, bpre_ref, ext_ref, state_ref, *, layer):
    ts = x_ref.shape[0]
    si = pl.program_id(1)

    @pl.when(si == 0)
    def _():
        state_ref[...] = jnp.zeros_like(state_ref)
        ext_ref[0:HALO, :] = jnp.zeros((HALO, POOL_WIDTH), jnp.float32)

    x = x_ref[...]
    shift = ada_ref[0:1, :]
    scale = ada_ref[1:2, :]
    gate = ada_ref[2:3, :]
    hb = ((_rms(x) * gpre_ref[...]) * (1.0 + scale) + shift).astype(jnp.bfloat16)

    def proj(off, width):
        return jnp.dot(hb, win_ref[:, off:off + width], preferred_element_type=jnp.float32)

    u = proj(OFF_PV, POOL_WIDTH)
    ext_ref[HALO:HALO + ts, :] = u
    pos = (si * ts + 1 + lax.broadcasted_iota(jnp.int32, (ts, 1), 0)).astype(jnp.float32)
    mixed = []
    for gi, w in enumerate(POOL_WINDOWS):
        cols = slice(gi * POOL_GROUP_DIM, (gi + 1) * POOL_GROUP_DIM)
        acc = ext_ref[:, cols]
        step = 1
        while step < w:
            acc = acc + _roll_rows(acc, step)
            step *= 2
        inv_count = 1.0 / jnp.minimum(pos, float(w))
        pooled = acc[HALO:, :] * inv_count - u[:, cols]
        mixed.append(_bdot(pooled, poolw_ref[gi]))
    ext_ref[0:HALO, :] = u[ts - HALO:, :]
    pooled = jnp.concatenate(mixed, axis=-1) * pscale_ref[...]
    branch_a = _bdot(pooled * _silu(proj(OFF_PG, POOL_WIDTH)), wpo_ref[...])

    logits = lbl_ref[...]
    e = jnp.exp(logits - jnp.max(logits, axis=0, keepdims=True))
    lb = jnp.zeros((1, HGRN_WIDTH), jnp.float32)
    for j in range(1, layer + 1):
        lb = lb + e[j:j + 1, :]
    lb = jnp.clip(lb / jnp.sum(e, axis=0, keepdims=True), 0.0, 1.0)
    q_ref[...] = _silu(proj(OFF_HQ, HGRN_WIDTH))
    f_ref[...] = lb + (1.0 - lb) * _sigmoid(proj(OFF_HF, HGRN_WIDTH))
    v_ref[...] = proj(OFF_HI, HGRN_WIDTH)
    g_ref[...] = _silu(proj(OFF_HG, HGRN_WIDTH))

    rowi = lax.broadcasted_iota(jnp.int32, (CHUNK, HEAD_DIM), 0)
    r2 = lax.broadcasted_iota(jnp.int32, (CHUNK, CHUNK), 0)
    c2 = lax.broadcasted_iota(jnp.int32, (CHUNK, CHUNK), 1)
    blk_eq = []
    size = 1
    while size < CHUNK:
        blk_eq.append((r2 // size) == (c2 // size))
        size *= 2
    hng = hng_ref[...]

    def chunk_body(ci, carry):
        rows = pl.ds(pl.multiple_of(ci * CHUNK, CHUNK), CHUNK)
        for hd in range(HEADS):
            cols = slice(hd * HEAD_DIM, (hd + 1) * HEAD_DIM)
            o, new_state = _hgrn_chunk(q_ref[rows, cols], f_ref[rows, cols], v_ref[rows, cols],
                                       state_ref[hd], rowi, blk_eq)
            state_ref[hd] = new_state
            bpre_ref[rows, cols] = ((_rms(o) * hng) * g_ref[rows, cols]).astype(jnp.bfloat16)
        return carry

    lax.fori_loop(0, ts // CHUNK, chunk_body, 0)
    branch_b = jnp.dot(bpre_ref[...], who_ref[...], preferred_element_type=jnp.float32)

    merged = (_sigmoid(proj(OFF_MGP, D_MODEL)) * branch_a
              + _sigmoid(proj(OFF_MGH, D_MODEL)) * branch_b)
    y = _bdot(merged, wout_ref[...])
    o_ref[...] = x + gate * (_rms(y) * gpost_ref[...])


def _resident(shape, index_map):
    return pl.BlockSpec(shape, index_map, pipeline_mode=pl.Buffered(1))


def _layer_call(layer, x, ada, g_pre, g_post, lb_logits, pool_scale, hgrn_norm_g,
                w_in, pool_w, w_pool_o, w_hgrn_o, w_out):
    batch, seq, d = x.shape
    depth = lb_logits.shape[0]
    ts = SEQ_TILE
    assert seq % ts == 0 and ts % CHUNK == 0 and d == D_MODEL

    def const2(b, s):
        return (0, 0)

    def per_layer3(b, s):
        return (layer, 0, 0)

    in_specs = [
        pl.BlockSpec((None, ts, d), lambda b, s: (b, s, 0)),
        pl.BlockSpec((None, 3, d), lambda b, s: (b, 0, 0)),
        _resident((1, d), const2),
        _resident((1, d), const2),
        _resident((depth, HGRN_WIDTH), const2),
        _resident((1, POOL_WIDTH), const2),
        _resident((1, HEAD_DIM), const2),
        _resident((None, d, IN_WIDTH), per_layer3),
        _resident((None, POOL_GROUPS, POOL_GROUP_DIM, POOL_GROUP_DIM), lambda b, s: (layer, 0, 0, 0)),
        _resident((None, POOL_WIDTH, d), per_layer3),
        _resident((None, HGRN_WIDTH, d), per_layer3),
        _resident((None, d, d), per_layer3),
    ]
    scratch = [
        pltpu.VMEM((ts, HGRN_WIDTH), jnp.float32),
        pltpu.VMEM((ts, HGRN_WIDTH), jnp.float32),
        pltpu.VMEM((ts, HGRN_WIDTH), jnp.float32),
        pltpu.VMEM((ts, HGRN_WIDTH), jnp.float32),
        pltpu.VMEM((ts, HGRN_WIDTH), jnp.bfloat16),
        pltpu.VMEM((HALO + ts, POOL_WIDTH), jnp.float32),
        pltpu.VMEM((HEADS, HEAD_DIM, HEAD_DIM), jnp.float32),
    ]
    return pl.pallas_call(
        functools.partial(_layer_kernel, layer=layer),
        grid=(batch, seq // ts),
        in_specs=in_specs,
        out_specs=pl.BlockSpec((None, ts, d), lambda b, s: (b, s, 0)),
        out_shape=jax.ShapeDtypeStruct(x.shape, x.dtype),
        scratch_shapes=scratch,
        compiler_params=pltpu.CompilerParams(
            dimension_semantics=("arbitrary", "arbitrary"),
            vmem_limit_bytes=VMEM_LIMIT_BYTES),
        name=f"hybrid_layer{layer}",
    )(x, ada, g_pre[layer][None, :], g_post[layer][None, :], lb_logits,
      pool_scale[layer][None, :], hgrn_norm_g[layer][None, :],
      w_in, pool_w, w_pool_o, w_hgrn_o, w_out)


def _ada_call(c, w_ada, b_ada):
    depth, d, width = w_ada.shape
    batch = c.shape[0]
    assert width % ADA_TILE == 0
    return pl.pallas_call(
        _ada_kernel,
        grid=(depth, width // ADA_TILE),
        in_specs=[
            pl.BlockSpec((batch, d), lambda l, n: (0, 0)),
            pl.BlockSpec((None, d, ADA_TILE), lambda l, n: (l, 0, n)),
            pl.BlockSpec((None, 1, ADA_TILE), lambda l, n: (l, 0, n)),
        ],
        out_specs=pl.BlockSpec((None, batch, ADA_TILE), lambda l, n: (l, 0, n)),
        out_shape=jax.ShapeDtypeStruct((depth, batch, width), jnp.float32),
        name="adaln_vectors",
    )(c, w_ada, b_ada[:, None, :])


def kernel(x, c, w_ada, b_ada, g_pre, g_post, w_in, pool_w, pool_scale, lb_logits,
           hgrn_norm_g, w_pool_o, w_hgrn_o, w_out):
    depth = w_in.shape[0]
    batch, _, d = x.shape
    ada = _ada_call(c, w_ada, b_ada).reshape(depth, batch, 3, d)
    bf = jnp.bfloat16
    w_in_b, pool_w_b = w_in.astype(bf), pool_w.astype(bf)
    w_pool_o_b, w_hgrn_o_b, w_out_b = w_pool_o.astype(bf), w_hgrn_o.astype(bf), w_out.astype(bf)
    for layer in range(depth):
        x = _layer_call(layer, x, ada[layer], g_pre, g_post, lb_logits, pool_scale, hgrn_norm_g,
                        w_in_b, pool_w_b, w_pool_o_b, w_hgrn_o_b, w_out_b)
    return x
```

```python
import functools

import jax
import jax.numpy as jnp
from jax import lax
from jax.experimental import pallas as pl
from jax.experimental.pallas import tpu as pltpu

D_MODEL = 1024
POOL_WINDOWS = (2, 4, 8, 16)
POOL_GROUPS = len(POOL_WINDOWS)
POOL_GROUP_DIM = D_MODEL // 8
POOL_WIDTH = POOL_GROUPS * POOL_GROUP_DIM
HEAD_DIM = 128
HEADS = D_MODEL // HEAD_DIM
HGRN_WIDTH = HEADS * HEAD_DIM
NORM_EPS = 1e-6
F_FLOOR = 1e-30
IN_SIZES = (POOL_WIDTH, POOL_WIDTH, HGRN_WIDTH, HGRN_WIDTH, HGRN_WIDTH, HGRN_WIDTH, D_MODEL, D_MODEL)
IN_WIDTH = sum(IN_SIZES)
(OFF_PV, OFF_PG, OFF_HQ, OFF_HF, OFF_HI, OFF_HG, OFF_MGP, OFF_MGH) = (
    sum(IN_SIZES[:i]) for i in range(len(IN_SIZES)))

SEQ_TILE = 256
CHUNK = 64
HALO = max(POOL_WINDOWS)
SUBLANES = 8
TILES = CHUNK // SUBLANES
TILE_BITS = (0, 1, 5)
SUBLANE_BITS = (2, 3, 4)
TIME_BITS = 6
TILES_PER_GROUP = 4
ROW_STRIDE = 4
assert CHUNK == 1 << TIME_BITS
N_OPS = TIME_BITS + 2
ADA_TILE = 512
VMEM_LIMIT_BYTES = 56 * 1024 * 1024

NT = (((1,), (1,)), ((), ()))
TN = (((0,), (0,)), ((), ()))


def _bdot(a, b):
    return jnp.dot(a.astype(jnp.bfloat16), b.astype(jnp.bfloat16),
                   preferred_element_type=jnp.float32)


def _bdot_general(a, b, dims):
    return lax.dot_general(a.astype(jnp.bfloat16), b.astype(jnp.bfloat16), dims,
                           preferred_element_type=jnp.float32)


def _sigmoid(z):
    return 1.0 / (1.0 + jnp.exp(-z))


def _silu(z):
    return z * _sigmoid(z)


def _rms(x):
    return x * lax.rsqrt(jnp.mean(x * x, axis=-1, keepdims=True) + NORM_EPS)


def _roll_rows(a, shift):
    return pltpu.roll(a, shift % a.shape[0], axis=0)


def _token_of_row(p):
    return ((p >> 5) << 5) | ((p & 7) << 2) | ((p >> 3) & 3)


def _ada_kernel(c_ref, w_ref, b_ref, o_ref):
    c = c_ref[...]
    o_ref[...] = jnp.dot(_silu(c), w_ref[...], preferred_element_type=jnp.float32) + b_ref[...]


def _sublane_partner(p, sb, isub):
    k = 1
    while k < sb:
        p = jnp.where((isub & k) != 0, p, _roll_rows(p, -k))
        k *= 2
    if 2 * sb == p.shape[0]:
        return _roll_rows(p, sb)
    return jnp.where((isub & sb) != 0, _roll_rows(p, sb), _roll_rows(p, -sb))


def _last_row(a):
    return jnp.broadcast_to(a[SUBLANES - 1:SUBLANES, :], a.shape)


def _scan_operands(q, f, isub):
    tiles = range(TILES)
    kk = [1.0 - f[t] for t in tiles]
    pre = [jnp.maximum(f[t], F_FLOOR) for t in tiles]
    suf = [None] * TILES
    zero = jnp.zeros_like(q[0])
    lhs, rhs = [list(q)], [list(kk)]
    for bit in range(TIME_BITS):
        ksuf = [kk[t] if suf[t] is None else kk[t] * suf[t] for t in tiles]
        if bit in TILE_BITS:
            tb = 1 << TILE_BITS.index(bit)
            lhs.append([q[t] * pre[t] if t & tb else zero for t in tiles])
            rhs.append([zero if t & tb else ksuf[t] for t in tiles])
        else:
            sb = 1 << SUBLANE_BITS.index(bit)
            upper = (isub & sb) != 0
            lhs.append([jnp.where(upper, q[t] * pre[t], 0.0) for t in tiles])
            rhs.append([jnp.where(upper, 0.0, ksuf[t]) for t in tiles])
        new_pre, new_suf = list(pre), list(suf)
        if bit in TILE_BITS:
            totals = {}
            for t in tiles:
                src = (t ^ tb) | (tb - 1)
                if src not in totals:
                    totals[src] = _last_row(pre[src]) if bit > max(SUBLANE_BITS) else pre[src]
                if t & tb:
                    new_pre[t] = pre[t] * totals[src]
                else:
                    new_suf[t] = totals[src] if suf[t] is None else suf[t] * totals[src]
        else:
            for base in range(0, TILES, TILES_PER_GROUP):
                partner = _sublane_partner(pre[base + TILES_PER_GROUP - 1], sb, isub)
                pre_mul = jnp.where(upper, partner, 1.0)
                suf_mul = jnp.where(upper, 1.0, partner)
                for t in range(base, base + TILES_PER_GROUP):
                    new_pre[t] = pre[t] * pre_mul
                    new_suf[t] = suf_mul if suf[t] is None else suf[t] * suf_mul
        pre, suf = new_pre, new_suf
    lhs.append([q[t] * pre[t] for t in tiles])
    rhs.append([kk[t] * suf[t] for t in tiles])
    return lhs, rhs, pre[TILES - 1]


def _layer_kernel(x_ref, ada_ref, gpre_ref, gpost_ref, lbl_ref, pscale_ref, hng_ref,
                  win_ref, poolw_ref, wpo_ref, who_ref, wout_ref, o_ref,
                  q_ref, f_ref, v_ref, g_ref, ob_ref, ext_ref, state_ref,
                  lhs_ref, rhs_ref, vb_ref, tot_ref, *, layer):
    ts = x_ref.shape[0]
    si = pl.program_id(1)

    @pl.when(si == 0)
    def _():
        state_ref[...] = jnp.zeros_like(state_ref)
        ext_ref[0:HALO, :] = jnp.zeros((HALO, POOL_WIDTH), jnp.float32)

    x = x_ref[...]
    shift = ada_ref[0:1, :]
    scale = ada_ref[1:2, :]
    gate = ada_ref[2:3, :]
    hb = ((_rms(x) * gpre_ref[...]) * (1.0 + scale) + shift).astype(jnp.bfloat16)

    def proj(off, width):
        return jnp.dot(hb, win_ref[:, off:off + width], preferred_element_type=jnp.float32)

    u = proj(OFF_PV, POOL_WIDTH)
    ext_ref[HALO:HALO + ts, :] = u
    pos = (si * ts + 1 + lax.broadcasted_iota(jnp.int32, (ts, 1), 0)).astype(jnp.float32)
    mixed = []
    for gi, w in enumerate(POOL_WINDOWS):
        cols = slice(gi * POOL_GROUP_DIM, (gi + 1) * POOL_GROUP_DIM)
        acc = ext_ref[:, cols]
        step = 1
        while step < w:
            acc = acc + _roll_rows(acc, step)
            step *= 2
        inv_count = 1.0 / jnp.minimum(pos, float(w))
        pooled = acc[HALO:, :] * inv_count - u[:, cols]
        mixed.append(_bdot(pooled, poolw_ref[gi]))
    ext_ref[0:HALO, :] = u[ts - HALO:, :]
    pooled = jnp.concatenate(mixed, axis=-1) * pscale_ref[...]
    branch_a = _bdot(pooled * _silu(proj(OFF_PG, POOL_WIDTH)), wpo_ref[...])

    logits = lbl_ref[...]
    e = jnp.exp(logits - jnp.max(logits, axis=0, keepdims=True))
    lb = jnp.zeros((1, HGRN_WIDTH), jnp.float32)
    for j in range(1, layer + 1):
        lb = lb + e[j:j + 1, :]
    lb = jnp.clip(lb / jnp.sum(e, axis=0, keepdims=True), 0.0, 1.0)

    def to_heads(ref, a):
        for hd in range(HEADS):
            ref[hd] = a[:, hd * HEAD_DIM:(hd + 1) * HEAD_DIM]

    to_heads(q_ref, _silu(proj(OFF_HQ, HGRN_WIDTH)))
    to_heads(f_ref, lb + (1.0 - lb) * _sigmoid(proj(OFF_HF, HGRN_WIDTH)))
    to_heads(v_ref, proj(OFF_HI, HGRN_WIDTH))
    to_heads(g_ref, _silu(proj(OFF_HG, HGRN_WIDTH)))

    isub = lax.broadcasted_iota(jnp.int32, (SUBLANES, HEAD_DIM), 0)
    cat = functools.partial(jnp.concatenate, axis=0)

    def tile_rows(ci):
        r0 = pl.multiple_of(ci * CHUNK, CHUNK)
        return [pl.ds(r0 + 32 * (t // TILES_PER_GROUP) + t % TILES_PER_GROUP, SUBLANES,
                      stride=ROW_STRIDE) for t in range(TILES)]

    def scan_body(ci, carry):
        rows_of = tile_rows(ci)
        for hd in range(HEADS):
            q, f, v = ([ref[hd, rows, :] for rows in rows_of] for ref in (q_ref, f_ref, v_ref))
            lhs, rhs, total = _scan_operands(q, f, isub)
            for i in range(N_OPS):
                lhs_ref[ci, hd, i] = cat(lhs[i]).astype(jnp.bfloat16)
                rhs_ref[ci, hd, i] = cat(rhs[i]).astype(jnp.bfloat16)
            vb_ref[ci, hd] = cat(v).astype(jnp.bfloat16)
            tot_ref[ci, hd] = _last_row(total)
        return carry

    lax.fori_loop(0, ts // CHUNK, scan_body, 0)

    r2 = _token_of_row(lax.broadcasted_iota(jnp.int32, (CHUNK, CHUNK), 0))
    c2 = _token_of_row(lax.broadcasted_iota(jnp.int32, (CHUNK, CHUNK), 1))
    eye = r2 == c2
    blk_eq = [(r2 >> (bit + 1)) == (c2 >> (bit + 1)) for bit in range(TIME_BITS)]
    hng = hng_ref[...]

    def chunk_body(ci, carry):
        rows_of = tile_rows(ci)
        heads = range(HEADS)

        def nt(hd, i):
            return lax.dot_general(lhs_ref[ci, hd, i], rhs_ref[ci, hd, i], NT,
                                   preferred_element_type=jnp.float32)

        scores = [jnp.where(eye, nt(hd, 0), 0.0) for hd in heads]
        for bit in range(TIME_BITS):
            for hd in heads:
                p = nt(hd, 1 + bit)
                scores[hd] = scores[hd] + (p if bit == TIME_BITS - 1
                                           else jnp.where(blk_eq[bit], p, 0.0))
        state_t = [state_ref[hd] for hd in heads]
        o_inter = [lax.dot_general(lhs_ref[ci, hd, N_OPS - 1], state_t[hd].astype(jnp.bfloat16), NT,
                                   preferred_element_type=jnp.float32) for hd in heads]
        for hd in heads:
            state_ref[hd] = (state_t[hd] * tot_ref[ci, hd][0:1, :]
                             + lax.dot_general(vb_ref[ci, hd], rhs_ref[ci, hd, N_OPS - 1], TN,
                                               preferred_element_type=jnp.float32))
        for hd in heads:
            o = o_inter[hd] + jnp.dot(scores[hd].astype(jnp.bfloat16), vb_ref[ci, hd],
                                      preferred_element_type=jnp.float32)
            for t, rows in enumerate(rows_of):
                o_tile = o[SUBLANES * t:SUBLANES * (t + 1), :]
                ob_ref[hd, rows, :] = (_rms(o_tile) * hng) * g_ref[hd, rows, :]
        return carry

    lax.fori_loop(0, ts // CHUNK, chunk_body, 0)
    gated = jnp.concatenate([ob_ref[hd] for hd in range(HEADS)], axis=-1)
    branch_b = _bdot(gated, who_ref[...])

    merged = (_sigmoid(proj(OFF_MGP, D_MODEL)) * branch_a
              + _sigmoid(proj(OFF_MGH, D_MODEL)) * branch_b)
    y = _bdot(merged, wout_ref[...])
    o_ref[...] = x + gate * (_rms(y) * gpost_ref[...])


def _resident(shape, index_map):
    return pl.BlockSpec(shape, index_map, pipeline_mode=pl.Buffered(1))


def _layer_call(layer, x, ada, g_pre, g_post, lb_logits, pool_scale, hgrn_norm_g,
                w_in, pool_w, w_pool_o, w_hgrn_o, w_out):
    batch, seq, d = x.shape
    depth = lb_logits.shape[0]
    ts = SEQ_TILE
    assert seq % ts == 0 and ts % CHUNK == 0 and d == D_MODEL
    chunks = ts // CHUNK

    def const2(b, s):
        return (0, 0)

    def per_layer3(b, s):
        return (layer, 0, 0)

    in_specs = [
        pl.BlockSpec((None, ts, d), lambda b, s: (b, s, 0)),
        pl.BlockSpec((None, 3, d), lambda b, s: (b, 0, 0)),
        _resident((1, d), const2),
        _resident((1, d), const2),
        _resident((depth, HGRN_WIDTH), const2),
        _resident((1, POOL_WIDTH), const2),
        _resident((1, HEAD_DIM), const2),
        _resident((None, d, IN_WIDTH), per_layer3),
        _resident((None, POOL_GROUPS, POOL_GROUP_DIM, POOL_GROUP_DIM), lambda b, s: (layer, 0, 0, 0)),
        _resident((None, POOL_WIDTH, d), per_layer3),
        _resident((None, HGRN_WIDTH, d), per_layer3),
        _resident((None, d, d), per_layer3),
    ]
    scratch = [
        pltpu.VMEM((HEADS, ts, HEAD_DIM), jnp.float32),
        pltpu.VMEM((HEADS, ts, HEAD_DIM), jnp.float32),
        pltpu.VMEM((HEADS, ts, HEAD_DIM), jnp.float32),
        pltpu.VMEM((HEADS, ts, HEAD_DIM), jnp.float32),
        pltpu.VMEM((HEADS, ts, HEAD_DIM), jnp.float32),
        pltpu.VMEM((HALO + ts, POOL_WIDTH), jnp.float32),
        pltpu.VMEM((HEADS, HEAD_DIM, HEAD_DIM), jnp.float32),
        pltpu.VMEM((chunks, HEADS, N_OPS, CHUNK, HEAD_DIM), jnp.bfloat16),
        pltpu.VMEM((chunks, HEADS, N_OPS, CHUNK, HEAD_DIM), jnp.bfloat16),
        pltpu.VMEM((chunks, HEADS, CHUNK, HEAD_DIM), jnp.bfloat16),
        pltpu.VMEM((chunks, HEADS, SUBLANES, HEAD_DIM), jnp.float32),
    ]
    return pl.pallas_call(
        functools.partial(_layer_kernel, layer=layer),
        grid=(batch, seq // ts),
        in_specs=in_specs,
        out_specs=pl.BlockSpec((None, ts, d), lambda b, s: (b, s, 0)),
        out_shape=jax.ShapeDtypeStruct(x.shape, x.dtype),
        scratch_shapes=scratch,
        compiler_params=pltpu.CompilerParams(
            dimension_semantics=("arbitrary", "arbitrary"),
            vmem_limit_bytes=VMEM_LIMIT_BYTES),
        name=f"hybrid_layer{layer}",
    )(x, ada, g_pre[layer][None, :], g_post[layer][None, :], lb_logits,
      pool_scale[layer][None, :], hgrn_norm_g[layer][None, :],
      w_in, pool_w, w_pool_o, w_hgrn_o, w_out)


def _ada_call(c, w_ada, b_ada):
    depth, d, width = w_ada.shape
    batch = c.shape[0]
    assert width % ADA_TILE == 0
    return pl.pallas_call(
        _ada_kernel,
        grid=(depth, width // ADA_TILE),
        in_specs=[
            pl.BlockSpec((batch, d), lambda l, n: (0, 0)),
            pl.BlockSpec((None, d, ADA_TILE), lambda l, n: (l, 0, n)),
            pl.BlockSpec((None, 1, ADA_TILE), lambda l, n: (l, 0, n)),
        ],
        out_specs=pl.BlockSpec((None, batch, ADA_TILE), lambda l, n: (l, 0, n)),
        out_shape=jax.ShapeDtypeStruct((depth, batch, width), jnp.float32),
        name="adaln_vectors",
    )(c, w_ada, b_ada[:, None, :])


def kernel(x, c, w_ada, b_ada, g_pre, g_post, w_in, pool_w, pool_scale, lb_logits,
           hgrn_norm_g, w_pool_o, w_hgrn_o, w_out):
    depth = w_in.shape[0]
    batch, _, d = x.shape
    ada = _ada_call(c, w_ada, b_ada).reshape(depth, batch, 3, d)
    bf = jnp.bfloat16
    w_in_b, pool_w_b = w_in.astype(bf), pool_w.astype(bf)
    w_pool_o_b, w_hgrn_o_b, w_out_b = w_pool_o.astype(bf), w_hgrn_o.astype(bf), w_out.astype(bf)
    for layer in range(depth):
        x = _layer_call(layer, x, ada[layer], g_pre, g_post, lb_logits, pool_scale, hgrn_norm_g,
                        w_in_b, pool_w_b, w_pool_o_b, w_hgrn_o_b, w_out_b)
    return x
```

```python
import functools

import jax
import jax.numpy as jnp
from jax import lax
from jax.experimental import pallas as pl
from jax.experimental.pallas import tpu as pltpu

D_MODEL = 1024
POOL_WINDOWS = (2, 4, 8, 16)
POOL_GROUPS = len(POOL_WINDOWS)
POOL_GROUP_DIM = D_MODEL // 8
POOL_WIDTH = POOL_GROUPS * POOL_GROUP_DIM
HEAD_DIM = 128
HEADS = D_MODEL // HEAD_DIM
HGRN_WIDTH = HEADS * HEAD_DIM
NORM_EPS = 1e-6
F_FLOOR = 1e-30
IN_SIZES = (POOL_WIDTH, POOL_WIDTH, HGRN_WIDTH, HGRN_WIDTH, HGRN_WIDTH, HGRN_WIDTH, D_MODEL, D_MODEL)
IN_WIDTH = sum(IN_SIZES)
(OFF_PV, OFF_PG, OFF_HQ, OFF_HF, OFF_HI, OFF_HG, OFF_MGP, OFF_MGH) = (
    sum(IN_SIZES[:i]) for i in range(len(IN_SIZES)))

SEQ_TILE = 256
CHUNK = 64
HALO = max(POOL_WINDOWS)
SUBLANES = 8
TILES = CHUNK // SUBLANES
TILE_BITS = (0, 1, 5)
SUBLANE_BITS = (2, 3, 4)
TIME_BITS = 6
TILES_PER_GROUP = 4
ROW_STRIDE = 4
assert CHUNK == 1 << TIME_BITS
LOCAL_BITS = 2
assert TILE_BITS[:LOCAL_BITS] == tuple(range(LOCAL_BITS)) and TILES_PER_GROUP == 1 << LOCAL_BITS
N_OPS = TIME_BITS - LOCAL_BITS + 1
ADA_TILE = 512
VMEM_LIMIT_BYTES = 56 * 1024 * 1024

NT = (((1,), (1,)), ((), ()))
TN = (((0,), (0,)), ((), ()))


def _bdot(a, b):
    return jnp.dot(a.astype(jnp.bfloat16), b.astype(jnp.bfloat16),
                   preferred_element_type=jnp.float32)


def _sigmoid(z):
    return 1.0 / (1.0 + jnp.exp(-z))


def _silu(z):
    return z * _sigmoid(z)


def _rms(x):
    return x * lax.rsqrt(jnp.mean(x * x, axis=-1, keepdims=True) + NORM_EPS)


def _roll_rows(a, shift):
    return pltpu.roll(a, shift % a.shape[0], axis=0)


def _token_of_row(p):
    return ((p >> 5) << 5) | ((p & 7) << 2) | ((p >> 3) & 3)


def _ada_kernel(c_ref, w_ref, b_ref, o_ref):
    c = c_ref[...]
    o_ref[...] = jnp.dot(_silu(c), w_ref[...], preferred_element_type=jnp.float32) + b_ref[...]


def _sublane_partner(p, sb, isub):
    k = 1
    while k < sb:
        p = jnp.where((isub & k) != 0, p, _roll_rows(p, -k))
        k *= 2
    if 2 * sb == p.shape[0]:
        return _roll_rows(p, sb)
    return jnp.where((isub & sb) != 0, _roll_rows(p, sb), _roll_rows(p, -sb))


def _last_row(a):
    return jnp.broadcast_to(a[SUBLANES - 1:SUBLANES, :], a.shape)


def _lane_sum(a):
    return jnp.sum(a, axis=-1, keepdims=True)


def _local_outputs(q, pf, kk, v):
    out = []
    for base in range(0, TILES, TILES_PER_GROUP):
        qs, fs, ks, vs = (a[base:base + TILES_PER_GROUP] for a in (q, pf, kk, v))
        acc = [_lane_sum(qs[j] * ks[j]) * vs[j] for j in range(TILES_PER_GROUP)]
        decayed_k = {}
        for j_t in range(1, TILES_PER_GROUP):
            qf = qs[j_t] * fs[j_t]
            for j_s in range(j_t - 1, -1, -1):
                decayed_k[j_t, j_s] = (ks[j_s] if j_s == j_t - 1
                                       else decayed_k[j_t - 1, j_s] * fs[j_t - 1])
                acc[j_t] = acc[j_t] + _lane_sum(qf * decayed_k[j_t, j_s]) * vs[j_s]
        out.extend(acc)
    return out


def _scan_operands(q, f, v, isub):
    tiles = range(TILES)
    kk = [1.0 - f[t] for t in tiles]
    pre = [jnp.maximum(f[t], F_FLOOR) for t in tiles]
    suf = [None] * TILES
    local = _local_outputs(q, pre, kk, v)
    zero = jnp.zeros_like(q[0])
    lhs, rhs = [], []
    for bit in range(TIME_BITS):
        if bit >= LOCAL_BITS:
            ksuf = [kk[t] if suf[t] is None else kk[t] * suf[t] for t in tiles]
        if bit in TILE_BITS:
            tb = 1 << TILE_BITS.index(bit)
            if bit >= LOCAL_BITS:
                lhs.append([q[t] * pre[t] if t & tb else zero for t in tiles])
                rhs.append([zero if t & tb else ksuf[t] for t in tiles])
        else:
            sb = 1 << SUBLANE_BITS.index(bit)
            upper = (isub & sb) != 0
            lhs.append([jnp.where(upper, q[t] * pre[t], 0.0) for t in tiles])
            rhs.append([jnp.where(upper, 0.0, ksuf[t]) for t in tiles])
        new_pre, new_suf = list(pre), list(suf)
        if bit in TILE_BITS:
            totals = {}
            for t in tiles:
                src = (t ^ tb) | (tb - 1)
                if src not in totals:
                    totals[src] = _last_row(pre[src]) if bit > max(SUBLANE_BITS) else pre[src]
                if t & tb:
                    new_pre[t] = pre[t] * totals[src]
                else:
                    new_suf[t] = totals[src] if suf[t] is None else suf[t] * totals[src]
        else:
            for base in range(0, TILES, TILES_PER_GROUP):
                partner = _sublane_partner(pre[base + TILES_PER_GROUP - 1], sb, isub)
                pre_mul = jnp.where(upper, partner, 1.0)
                suf_mul = jnp.where(upper, 1.0, partner)
                for t in range(base, base + TILES_PER_GROUP):
                    new_pre[t] = pre[t] * pre_mul
                    new_suf[t] = suf_mul if suf[t] is None else suf[t] * suf_mul
        pre, suf = new_pre, new_suf
    lhs.append([q[t] * pre[t] for t in tiles])
    rhs.append([kk[t] * suf[t] for t in tiles])
    return lhs, rhs, pre[TILES - 1], local


def _layer_kernel(x_ref, ada_ref, gpre_ref, gpost_ref, lbl_ref, pscale_ref, hng_ref,
                  win_ref, poolw_ref, wpo_ref, who_ref, wout_ref, o_ref,
                  q_ref, f_ref, v_ref, g_ref, ob_ref, ext_ref, state_ref,
                  lhs_ref, rhs_ref, vb_ref, tot_ref, loc_ref, *, layer):
    ts = x_ref.shape[0]
    si = pl.program_id(1)

    @pl.when(si == 0)
    def _():
        state_ref[...] = jnp.zeros_like(state_ref)
        ext_ref[0:HALO, :] = jnp.zeros((HALO, POOL_WIDTH), jnp.float32)

    x = x_ref[...]
    shift = ada_ref[0:1, :]
    scale = ada_ref[1:2, :]
    gate = ada_ref[2:3, :]
    hb = ((_rms(x) * gpre_ref[...]) * (1.0 + scale) + shift).astype(jnp.bfloat16)

    def proj(off, width):
        return jnp.dot(hb, win_ref[:, off:off + width], preferred_element_type=jnp.float32)

    u = proj(OFF_PV, POOL_WIDTH)
    ext_ref[HALO:HALO + ts, :] = u
    pos = (si * ts + 1 + lax.broadcasted_iota(jnp.int32, (ts, 1), 0)).astype(jnp.float32)
    mixed = []
    for gi, w in enumerate(POOL_WINDOWS):
        cols = slice(gi * POOL_GROUP_DIM, (gi + 1) * POOL_GROUP_DIM)
        acc = ext_ref[:, cols]
        step = 1
        while step < w:
            acc = acc + _roll_rows(acc, step)
            step *= 2
        inv_count = 1.0 / jnp.minimum(pos, float(w))
        pooled = acc[HALO:, :] * inv_count - u[:, cols]
        mixed.append(_bdot(pooled, poolw_ref[gi]))
    ext_ref[0:HALO, :] = u[ts - HALO:, :]
    pooled = jnp.concatenate(mixed, axis=-1) * pscale_ref[...]
    branch_a = _bdot(pooled * _silu(proj(OFF_PG, POOL_WIDTH)), wpo_ref[...])

    logits = lbl_ref[...]
    e = jnp.exp(logits - jnp.max(logits, axis=0, keepdims=True))
    lb = jnp.zeros((1, HGRN_WIDTH), jnp.float32)
    for j in range(1, layer + 1):
        lb = lb + e[j:j + 1, :]
    lb = jnp.clip(lb / jnp.sum(e, axis=0, keepdims=True), 0.0, 1.0)

    def to_heads(ref, a):
        for hd in range(HEADS):
            ref[hd] = a[:, hd * HEAD_DIM:(hd + 1) * HEAD_DIM]

    to_heads(q_ref, _silu(proj(OFF_HQ, HGRN_WIDTH)))
    to_heads(f_ref, lb + (1.0 - lb) * _sigmoid(proj(OFF_HF, HGRN_WIDTH)))
    to_heads(v_ref, proj(OFF_HI, HGRN_WIDTH))
    to_heads(g_ref, _silu(proj(OFF_HG, HGRN_WIDTH)))

    isub = lax.broadcasted_iota(jnp.int32, (SUBLANES, HEAD_DIM), 0)
    cat = functools.partial(jnp.concatenate, axis=0)

    def tile_rows(ci):
        r0 = ci * CHUNK if isinstance(ci, int) else pl.multiple_of(ci * CHUNK, CHUNK)
        return [pl.ds(r0 + 32 * (t // TILES_PER_GROUP) + t % TILES_PER_GROUP, SUBLANES,
                      stride=ROW_STRIDE) for t in range(TILES)]

    def scan_body(ci, carry):
        rows_of = tile_rows(ci)
        for hd in range(HEADS):
            q, f, v = ([ref[hd, rows, :] for rows in rows_of] for ref in (q_ref, f_ref, v_ref))
            lhs, rhs, total, local = _scan_operands(q, f, v, isub)
            for i in range(N_OPS):
                lhs_ref[ci, hd, i] = cat(lhs[i]).astype(jnp.bfloat16)
                rhs_ref[ci, hd, i] = cat(rhs[i]).astype(jnp.bfloat16)
            vb_ref[ci, hd] = cat(v).astype(jnp.bfloat16)
            tot_ref[ci, hd] = _last_row(total)
            loc_ref[ci, hd] = cat(local)
        return carry

    for ci in range(ts // CHUNK):
        scan_body(ci, 0)

    r2 = _token_of_row(lax.broadcasted_iota(jnp.int32, (CHUNK, CHUNK), 0))
    c2 = _token_of_row(lax.broadcasted_iota(jnp.int32, (CHUNK, CHUNK), 1))
    blk_eq = [(r2 >> (bit + 1)) == (c2 >> (bit + 1)) for bit in range(TIME_BITS)]
    hng = hng_ref[...]

    def chunk_body(ci, carry):
        rows_of = tile_rows(ci)
        heads = range(HEADS)

        def nt(hd, i):
            return lax.dot_general(lhs_ref[ci, hd, i], rhs_ref[ci, hd, i], NT,
                                   preferred_element_type=jnp.float32)

        scores = [None] * HEADS
        for bit in range(LOCAL_BITS, TIME_BITS):
            for hd in heads:
                p = nt(hd, bit - LOCAL_BITS)
                if bit < TIME_BITS - 1:
                    p = jnp.where(blk_eq[bit], p, 0.0)
                scores[hd] = p if scores[hd] is None else scores[hd] + p
        state_t = [state_ref[hd] for hd in heads]
        o_inter = [lax.dot_general(lhs_ref[ci, hd, N_OPS - 1], state_t[hd].astype(jnp.bfloat16), NT,
                                   preferred_element_type=jnp.float32) for hd in heads]
        for hd in heads:
            state_ref[hd] = (state_t[hd] * tot_ref[ci, hd][0:1, :]
                             + lax.dot_general(vb_ref[ci, hd], rhs_ref[ci, hd, N_OPS - 1], TN,
                                               preferred_element_type=jnp.float32))
        for hd in heads:
            o = (o_inter[hd] + loc_ref[ci, hd]
                 + jnp.dot(scores[hd].astype(jnp.bfloat16), vb_ref[ci, hd],
                           preferred_element_type=jnp.float32))
            for t, rows in enumerate(rows_of):
                o_tile = o[SUBLANES * t:SUBLANES * (t + 1), :]
                ob_ref[hd, rows, :] = (_rms(o_tile) * hng) * g_ref[hd, rows, :]
        return carry

    for ci in range(ts // CHUNK):
        chunk_body(ci, 0)
    gated =jnp.concatenate([ob_ref[hd] for hd in range(HEADS)], axis=-1)
    branch_b = _bdot(gated, who_ref[...])

    merged = (_sigmoid(proj(OFF_MGP, D_MODEL)) * branch_a
              + _sigmoid(proj(OFF_MGH, D_MODEL)) * branch_b)
    y = _bdot(merged, wout_ref[...])
    o_ref[...] = x + gate * (_rms(y) * gpost_ref[...])


def _resident(shape, index_map):
    return pl.BlockSpec(shape, index_map, pipeline_mode=pl.Buffered(1))


def _layer_call(layer, x, ada, g_pre, g_post, lb_logits, pool_scale, hgrn_norm_g,
                w_in, pool_w, w_pool_o, w_hgrn_o, w_out):
    batch, seq, d = x.shape
    depth = lb_logits.shape[0]
    ts = SEQ_TILE
    assert seq % ts == 0 and ts % CHUNK == 0 and d == D_MODEL
    chunks = ts // CHUNK

    def const2(b, s):
        return (0, 0)

    def per_layer3(b, s):
        return (layer, 0, 0)

    in_specs = [
        pl.BlockSpec((None, ts, d), lambda b, s: (b, s, 0)),
        pl.BlockSpec((None, 3, d), lambda b, s: (b, 0, 0)),
        _resident((1, d), const2),
        _resident((1, d), const2),
        _resident((depth, HGRN_WIDTH), const2),
        _resident((1, POOL_WIDTH), const2),
        _resident((1, HEAD_DIM), const2),
        _resident((None, d, IN_WIDTH), per_layer3),
        _resident((None, POOL_GROUPS, POOL_GROUP_DIM, POOL_GROUP_DIM), lambda b, s: (layer, 0, 0, 0)),
        _resident((None, POOL_WIDTH, d), per_layer3),
        _resident((None, HGRN_WIDTH, d), per_layer3),
        _resident((None, d, d), per_layer3),
    ]
    scratch = [
        pltpu.VMEM((HEADS, ts, HEAD_DIM), jnp.float32),
        pltpu.VMEM((HEADS, ts, HEAD_DIM), jnp.float32),
        pltpu.VMEM((HEADS, ts, HEAD_DIM), jnp.float32),
        pltpu.VMEM((HEADS, ts, HEAD_DIM), jnp.float32),
        pltpu.VMEM((HEADS, ts, HEAD_DIM), jnp.float32),
        pltpu.VMEM((HALO + ts, POOL_WIDTH), jnp.float32),
        pltpu.VMEM((HEADS, HEAD_DIM, HEAD_DIM), jnp.float32),
        pltpu.VMEM((chunks, HEADS, N_OPS, CHUNK, HEAD_DIM), jnp.bfloat16),
        pltpu.VMEM((chunks, HEADS, N_OPS, CHUNK, HEAD_DIM), jnp.bfloat16),
        pltpu.VMEM((chunks, HEADS, CHUNK, HEAD_DIM), jnp.bfloat16),
        pltpu.VMEM((chunks, HEADS, SUBLANES, HEAD_DIM), jnp.float32),
        pltpu.VMEM((chunks, HEADS, CHUNK, HEAD_DIM), jnp.float32),
    ]
    return pl.pallas_call(
        functools.partial(_layer_kernel, layer=layer),
        grid=(batch, seq // ts),
        in_specs=in_specs,
        out_specs=pl.BlockSpec((None, ts, d), lambda b, s: (b, s, 0)),
        out_shape=jax.ShapeDtypeStruct(x.shape, x.dtype),
        scratch_shapes=scratch,
        compiler_params=pltpu.CompilerParams(
            dimension_semantics=("arbitrary", "arbitrary"),
            vmem_limit_bytes=VMEM_LIMIT_BYTES),
        name=f"hybrid_layer{layer}",
    )(x, ada, g_pre[layer][None, :], g_post[layer][None, :], lb_logits,
      pool_scale[layer][None, :], hgrn_norm_g[layer][None, :],
      w_in, pool_w, w_pool_o, w_hgrn_o, w_out)


def _ada_call(c, w_ada, b_ada):
    depth, d, width = w_ada.shape
    batch = c.shape[0]
    assert width % ADA_TILE == 0
    return pl.pallas_call(
        _ada_kernel,
        grid=(depth, width // ADA_TILE),
        in_specs=[
            pl.BlockSpec((batch, d), lambda l, n: (0, 0)),
            pl.BlockSpec((None, d, ADA_TILE), lambda l, n: (l, 0, n)),
            pl.BlockSpec((None, 1, ADA_TILE), lambda l, n: (l, 0, n)),
        ],
        out_specs=pl.BlockSpec((None, batch, ADA_TILE), lambda l, n: (l, 0, n)),
        out_shape=jax.ShapeDtypeStruct((depth, batch, width), jnp.float32),
        name="adaln_vectors",
    )(c, w_ada, b_ada[:, None, :])


def kernel(x, c, w_ada, b_ada, g_pre, g_post, w_in, pool_w, pool_scale, lb_logits,
           hgrn_norm_g, w_pool_o, w_hgrn_o, w_out):
    depth = w_in.shape[0]
    batch, _, d = x.shape
    ada = _ada_call(c, w_ada, b_ada).reshape(depth, batch, 3, d)
    bf = jnp.bfloat16
    w_in_b, pool_w_b = w_in.astype(bf), pool_w.astype(bf)
    w_pool_o_b, w_hgrn_o_b, w_out_b = w_pool_o.astype(bf), w_hgrn_o.astype(bf), w_out.astype(bf)
    for layer in range(depth):
        x = _layer_call(layer, x, ada[layer], g_pre, g_post, lb_logits, pool_scale, hgrn_norm_g,
                        w_in_b, pool_w_b, w_pool_o_b, w_hgrn_o_b, w_out_b)
    return x
```

```python
import functools

import jax
import jax.numpy as jnp
from jax import lax
from jax.experimental import pallas as pl
from jax.experimental.pallas import tpu as pltpu

D_MODEL = 1024
POOL_WINDOWS = (2, 4, 8, 16)
POOL_GROUPS = len(POOL_WINDOWS)
POOL_GROUP_DIM = D_MODEL // 8
POOL_WIDTH = POOL_GROUPS * POOL_GROUP_DIM
HEAD_DIM = 128
HEADS = D_MODEL // HEAD_DIM
HGRN_WIDTH = HEADS * HEAD_DIM
NORM_EPS = 1e-6
F_FLOOR = 1e-30
NEG_LOG2_E = -1.4426950408889634
IN_SIZES = (POOL_WIDTH, POOL_WIDTH, HGRN_WIDTH, HGRN_WIDTH, HGRN_WIDTH, HGRN_WIDTH, D_MODEL, D_MODEL)
IN_OFFSETS = tuple(sum(IN_SIZES[:i]) for i in range(len(IN_SIZES)))
(SEC_PV, SEC_PG, SEC_HQ, SEC_HF, SEC_HI, SEC_HG, SEC_MGP, SEC_MGH) = range(len(IN_SIZES))

SEQ_TILE = 256
CHUNK = 64
HALO = max(POOL_WINDOWS)
SUBLANES = 8
TILES = CHUNK // SUBLANES
TILE_BITS = (0, 1, 5)
SUBLANE_BITS = (2, 3, 4)
TIME_BITS = 6
TILES_PER_GROUP = 4
ROW_STRIDE = 4
assert CHUNK == 1 << TIME_BITS
LOCAL_BITS = 2
assert TILE_BITS[:LOCAL_BITS] == tuple(range(LOCAL_BITS)) and TILES_PER_GROUP == 1 << LOCAL_BITS
N_OPS = TIME_BITS - LOCAL_BITS + 1
ADA_TILE = 512
VMEM_LIMIT_BYTES = 56 * 1024 * 1024

NT = (((1,), (1,)), ((), ()))
TN = (((0,), (0,)), ((), ()))


def _bdot(a, b):
    return jnp.dot(a.astype(jnp.bfloat16), b.astype(jnp.bfloat16),
                   preferred_element_type=jnp.float32)


def _sigmoid(z):
    return 1.0 / (1.0 + jnp.exp2(z * NEG_LOG2_E))


def _silu(z):
    return z * _sigmoid(z)


def _rms(x):
    return x * lax.rsqrt(jnp.mean(x * x, axis=-1, keepdims=True) + NORM_EPS)


def _roll_rows(a, shift):
    return pltpu.roll(a, shift % a.shape[0], axis=0)


def _token_of_row(p):
    return ((p >> 5) << 5) | ((p & 7) << 2) | ((p >> 3) & 3)


def _ada_kernel(c_ref, w_ref, b_ref, o_ref):
    c = c_ref[...]
    o_ref[...] = jnp.dot(_silu(c), w_ref[...], preferred_element_type=jnp.float32) + b_ref[...]


def _sublane_partner(p, sb, isub):
    k = 1
    while k < sb:
        p = jnp.where((isub & k) != 0, p, _roll_rows(p, -k))
        k *= 2
    if 2 * sb == p.shape[0]:
        return _roll_rows(p, sb)
    return jnp.where((isub & sb) != 0, _roll_rows(p, sb), _roll_rows(p, -sb))


def _last_row(a):
    return jnp.broadcast_to(a[SUBLANES - 1:SUBLANES, :], a.shape)


def _lane_sum(a):
    return jnp.sum(a, axis=-1, keepdims=True)


def _local_outputs(q, pf, kk, v):
    out = []
    for base in range(0, TILES, TILES_PER_GROUP):
        qs, fs, ks, vs = (a[base:base + TILES_PER_GROUP] for a in (q, pf, kk, v))
        acc = [_lane_sum(qs[j] * ks[j]) * vs[j] for j in range(TILES_PER_GROUP)]
        decayed_k = {}
        for j_t in range(1, TILES_PER_GROUP):
            qf = qs[j_t] * fs[j_t]
            for j_s in range(j_t - 1, -1, -1):
                decayed_k[j_t, j_s] = (ks[j_s] if j_s == j_t - 1
                                       else decayed_k[j_t - 1, j_s] * fs[j_t - 1])
                acc[j_t] = acc[j_t] + _lane_sum(qf * decayed_k[j_t, j_s]) * vs[j_s]
        out.extend(acc)
    return out


def _scan_operands(q, f, v, isub):
    tiles = range(TILES)
    kk = [1.0 - f[t] for t in tiles]
    pre = [jnp.maximum(f[t], F_FLOOR) for t in tiles]
    suf = [None] * TILES
    local = _local_outputs(q, pre, kk, v)
    zero = jnp.zeros_like(q[0])
    lhs, rhs = [], []
    for bit in range(TIME_BITS):
        if bit >= LOCAL_BITS:
            ksuf = [kk[t] if suf[t] is None else kk[t] * suf[t] for t in tiles]
        if bit in TILE_BITS:
            tb = 1 << TILE_BITS.index(bit)
            if bit >= LOCAL_BITS:
                lhs.append([q[t] * pre[t] if t & tb else zero for t in tiles])
                rhs.append([zero if t & tb else ksuf[t] for t in tiles])
        else:
            sb = 1 << SUBLANE_BITS.index(bit)
            upper = (isub & sb) != 0
            lhs.append([jnp.where(upper, q[t] * pre[t], 0.0) for t in tiles])
            rhs.append([jnp.where(upper, 0.0, ksuf[t]) for t in tiles])
        new_pre, new_suf = list(pre), list(suf)
        if bit in TILE_BITS:
            totals = {}
            for t in tiles:
                src = (t ^ tb) | (tb - 1)
                if src not in totals:
                    totals[src] = _last_row(pre[src]) if bit > max(SUBLANE_BITS) else pre[src]
                if t & tb:
                    new_pre[t] = pre[t] * totals[src]
                else:
                    new_suf[t] = totals[src] if suf[t] is None else suf[t] * totals[src]
        else:
            for base in range(0, TILES, TILES_PER_GROUP):
                partner = _sublane_partner(pre[base + TILES_PER_GROUP - 1], sb, isub)
                pre_mul = jnp.where(upper, partner, 1.0)
                suf_mul = jnp.where(upper, 1.0, partner)
                for t in range(base, base + TILES_PER_GROUP):
                    new_pre[t] = pre[t] * pre_mul
                    new_suf[t] = suf_mul if suf[t] is None else suf[t] * suf_mul
        pre, suf = new_pre, new_suf
    lhs.append([q[t] * pre[t] for t in tiles])
    rhs.append([kk[t] * suf[t] for t in tiles])
    return lhs, rhs, pre[TILES - 1], local


def _layer_kernel(x_ref, ada_ref, gpre_ref, gpost_ref, lbl_ref, pscale_ref, hng_ref, *refs, layer):
    win_refs = refs[:len(IN_SIZES)]
    (poolw_ref, wpo_ref, who_ref, wout_ref, o_ref,
     q_ref, f_ref, v_ref, g_ref, ob_ref, ext_ref, state_ref,
     lhs_ref, rhs_ref, vb_ref, tot_ref, loc_ref) = refs[len(IN_SIZES):]
    ts = x_ref.shape[0]
    si = pl.program_id(1)

    @pl.when(si == 0)
    def _():
        state_ref[...] = jnp.zeros_like(state_ref)
        ext_ref[0:HALO, :] = jnp.zeros((HALO, POOL_WIDTH), jnp.float32)

    x = x_ref[...]
    shift = ada_ref[0:1, :]
    scale = ada_ref[1:2, :]
    gate = ada_ref[2:3, :]
    hb = ((_rms(x) * gpre_ref[...]) * (1.0 + scale) + shift).astype(jnp.bfloat16)

    def proj(section):
        return jnp.dot(hb, win_refs[section][...], preferred_element_type=jnp.float32)

    u = proj(SEC_PV)
    ext_ref[HALO:HALO + ts, :] = u
    pos = (si * ts + 1 + lax.broadcasted_iota(jnp.int32, (ts, 1), 0)).astype(jnp.float32)
    mixed = []
    for gi, w in enumerate(POOL_WINDOWS):
        cols = slice(gi * POOL_GROUP_DIM, (gi + 1) * POOL_GROUP_DIM)
        acc = ext_ref[:, cols]
        step = 1
        while step < w:
            acc = acc + _roll_rows(acc, step)
            step *= 2
        inv_count = 1.0 / jnp.minimum(pos, float(w))
        pooled = acc[HALO:, :] * inv_count - u[:, cols]
        mixed.append(_bdot(pooled, poolw_ref[gi]))
    ext_ref[0:HALO, :] = u[ts - HALO:, :]
    pooled = jnp.concatenate(mixed, axis=-1) * pscale_ref[...]
    branch_a = _bdot(pooled * _silu(proj(SEC_PG)), wpo_ref[...])

    logits = lbl_ref[...]
    e = jnp.exp(logits - jnp.max(logits, axis=0, keepdims=True))
    lb = jnp.zeros((1, HGRN_WIDTH), jnp.float32)
    for j in range(1, layer + 1):
        lb = lb + e[j:j + 1, :]
    lb = jnp.clip(lb / jnp.sum(e, axis=0, keepdims=True), 0.0, 1.0)

    def to_heads(ref, a):
        for hd in range(HEADS):
            ref[hd] = a[:, hd * HEAD_DIM:(hd + 1) * HEAD_DIM]

    to_heads(q_ref, _silu(proj(SEC_HQ)))
    to_heads(f_ref, lb + (1.0 - lb) * _sigmoid(proj(SEC_HF)))
    to_heads(v_ref, proj(SEC_HI))
    to_heads(g_ref, _silu(proj(SEC_HG)))

    isub = lax.broadcasted_iota(jnp.int32, (SUBLANES, HEAD_DIM), 0)
    cat = functools.partial(jnp.concatenate, axis=0)

    def tile_rows(ci):
        return [pl.ds(ci * CHUNK + 32 * (t // TILES_PER_GROUP) + t % TILES_PER_GROUP, SUBLANES,
                      stride=ROW_STRIDE) for t in range(TILES)]

    def scan_chunk(ci):
        rows_of = tile_rows(ci)
        for hd in range(HEADS):
            q, f, v = ([ref[hd, rows, :] for rows in rows_of] for ref in (q_ref, f_ref, v_ref))
            lhs, rhs, total, local = _scan_operands(q, f, v, isub)
            for i in range(N_OPS):
                lhs_ref[ci, hd, i] = cat(lhs[i]).astype(jnp.bfloat16)
                rhs_ref[ci, hd, i] = cat(rhs[i]).astype(jnp.bfloat16)
            vb_ref[ci, hd] = cat(v).astype(jnp.bfloat16)
            tot_ref[ci, hd] = _last_row(total)
            loc_ref[ci, hd] = cat(local)

    r2 = _token_of_row(lax.broadcasted_iota(jnp.int32, (CHUNK, CHUNK), 0))
    c2 = _token_of_row(lax.broadcasted_iota(jnp.int32, (CHUNK, CHUNK), 1))
    blk_eq = [(r2 >> (bit + 1)) == (c2 >> (bit + 1)) for bit in range(TIME_BITS)]
    hng = hng_ref[...]

    def output_chunk(ci):
        rows_of = tile_rows(ci)
        heads = range(HEADS)

        def nt(hd, i):
            return lax.dot_general(lhs_ref[ci, hd, i], rhs_ref[ci, hd, i], NT,
                                   preferred_element_type=jnp.float32)

        scores = [None] * HEADS
        for bit in range(LOCAL_BITS, TIME_BITS):
            for hd in heads:
                p = nt(hd, bit - LOCAL_BITS)
                if bit < TIME_BITS - 1:
                    p = jnp.where(blk_eq[bit], p, 0.0)
                scores[hd] = p if scores[hd] is None else scores[hd] + p
        state_t = [state_ref[hd] for hd in heads]
        o_inter = [lax.dot_general(lhs_ref[ci, hd, N_OPS - 1], state_t[hd].astype(jnp.bfloat16), NT,
                                   preferred_element_type=jnp.float32) for hd in heads]
        for hd in heads:
            state_ref[hd] = (state_t[hd] * tot_ref[ci, hd][0:1, :]
                             + lax.dot_general(vb_ref[ci, hd], rhs_ref[ci, hd, N_OPS - 1], TN,
                                               preferred_element_type=jnp.float32))
        for hd in heads:
            o = (o_inter[hd] + loc_ref[ci, hd]
                 + jnp.dot(scores[hd].astype(jnp.bfloat16), vb_ref[ci, hd],
                           preferred_element_type=jnp.float32))
            for t, rows in enumerate(rows_of):
                o_tile = o[SUBLANES * t:SUBLANES * (t + 1), :]
                ob_ref[hd, rows, :] = (_rms(o_tile) * hng) * g_ref[hd, rows, :]

    for ci in range(ts // CHUNK):
        scan_chunk(ci)
    for ci in range(ts // CHUNK):
        output_chunk(ci)
    gated = jnp.concatenate([ob_ref[hd] for hd in range(HEADS)], axis=-1)
    branch_b = _bdot(gated, who_ref[...])

    merged = _sigmoid(proj(SEC_MGP)) * branch_a + _sigmoid(proj(SEC_MGH)) * branch_b
    y = _bdot(merged, wout_ref[...])
    o_ref[...] = x + gate * (_rms(y) * gpost_ref[...])


def _resident(shape, index_map):
    return pl.BlockSpec(shape, index_map, pipeline_mode=pl.Buffered(1))


def _layer_call(layer, x, ada, g_pre, g_post, lb_logits, pool_scale, hgrn_norm_g,
                w_in_sections, pool_w, w_pool_o, w_hgrn_o, w_out):
    batch, seq, d = x.shape
    depth = lb_logits.shape[0]
    ts = SEQ_TILE
    assert seq % ts == 0 and ts % CHUNK == 0 and d == D_MODEL
    chunks = ts // CHUNK

    def const2(b, s):
        return (0, 0)

    def per_layer3(b, s):
        return (layer, 0, 0)

    in_specs = [
        pl.BlockSpec((None, ts, d), lambda b, s: (b, s, 0)),
        pl.BlockSpec((None, 3, d), lambda b, s: (b, 0, 0)),
        _resident((1, d), const2),
        _resident((1, d), const2),
        _resident((depth, HGRN_WIDTH), const2),
        _resident((1, POOL_WIDTH), const2),
        _resident((1, HEAD_DIM), const2),
        *[_resident((None, d, size), per_layer3) for size in IN_SIZES],
        _resident((None, POOL_GROUPS, POOL_GROUP_DIM, POOL_GROUP_DIM), lambda b, s: (layer, 0, 0, 0)),
        _resident((None, POOL_WIDTH, d), per_layer3),
        _resident((None, HGRN_WIDTH, d), per_layer3),
        _resident((None, d, d), per_layer3),
    ]
    scratch = [
        pltpu.VMEM((HEADS, ts, HEAD_DIM), jnp.float32),
        pltpu.VMEM((HEADS, ts, HEAD_DIM), jnp.float32),
        pltpu.VMEM((HEADS, ts, HEAD_DIM), jnp.float32),
        pltpu.VMEM((HEADS, ts, HEAD_DIM), jnp.float32),
        pltpu.VMEM((HEADS, ts, HEAD_DIM), jnp.float32),
        pltpu.VMEM((HALO + ts, POOL_WIDTH), jnp.float32),
        pltpu.VMEM((HEADS, HEAD_DIM, HEAD_DIM), jnp.float32),
        pltpu.VMEM((chunks, HEADS, N_OPS, CHUNK, HEAD_DIM), jnp.bfloat16),
        pltpu.VMEM((chunks, HEADS, N_OPS, CHUNK, HEAD_DIM), jnp.bfloat16),
        pltpu.VMEM((chunks, HEADS, CHUNK, HEAD_DIM), jnp.bfloat16),
        pltpu.VMEM((chunks, HEADS, SUBLANES, HEAD_DIM), jnp.float32),
        pltpu.VMEM((chunks, HEADS, CHUNK, HEAD_DIM), jnp.float32),
    ]
    return pl.pallas_call(
        functools.partial(_layer_kernel, layer=layer),
        grid=(batch, seq // ts),
        in_specs=in_specs,
        out_specs=pl.BlockSpec((None, ts, d), lambda b, s: (b, s, 0)),
        out_shape=jax.ShapeDtypeStruct(x.shape, x.dtype),
        scratch_shapes=scratch,
        compiler_params=pltpu.CompilerParams(
            dimension_semantics=("arbitrary", "arbitrary"),
            vmem_limit_bytes=VMEM_LIMIT_BYTES),
        name=f"hybrid_layer{layer}",
    )(x, ada, g_pre[layer][None, :], g_post[layer][None, :], lb_logits,
      pool_scale[layer][None, :], hgrn_norm_g[layer][None, :],
      *w_in_sections, pool_w, w_pool_o, w_hgrn_o, w_out)


def _ada_call(c, w_ada, b_ada):
    depth, d, width = w_ada.shape
    batch = c.shape[0]
    assert width % ADA_TILE == 0
    return pl.pallas_call(
        _ada_kernel,
        grid=(depth, width // ADA_TILE),
        in_specs=[
            pl.BlockSpec((batch, d), lambda l, n: (0, 0)),
            pl.BlockSpec((None, d, ADA_TILE), lambda l, n: (l, 0, n)),
            pl.BlockSpec((None, 1, ADA_TILE), lambda l, n: (l, 0, n)),
        ],
        out_specs=pl.BlockSpec((None, batch, ADA_TILE), lambda l, n: (l, 0, n)),
        out_shape=jax.ShapeDtypeStruct((depth, batch, width), jnp.float32),
        name="adaln_vectors",
    )(c, w_ada, b_ada[:, None, :])


def kernel(x, c, w_ada, b_ada, g_pre, g_post, w_in, pool_w, pool_scale, lb_logits,
           hgrn_norm_g, w_pool_o, w_hgrn_o, w_out):
    depth = w_in.shape[0]
    batch, _, d = x.shape
    ada = _ada_call(c, w_ada, b_ada).reshape(depth, batch, 3, d)
    bf = jnp.bfloat16
    w_in_b = [w_in[:, :, off:off + size].astype(bf) for off, size in zip(IN_OFFSETS, IN_SIZES)]
    pool_w_b = pool_w.astype(bf)
    w_pool_o_b, w_hgrn_o_b, w_out_b = w_pool_o.astype(bf), w_hgrn_o.astype(bf), w_out.astype(bf)
    for layer in range(depth):
        x = _layer_call(layer, x, ada[layer], g_pre, g_post, lb_logits, pool_scale, hgrn_norm_g,
                        w_in_b, pool_w_b, w_pool_o_b, w_hgrn_o_b, w_out_b)
    return x
```

```python
import functools

import jax
import jax.numpy as jnp
from jax import lax
from jax.experimental import pallas as pl
from jax.experimental.pallas import tpu as pltpu

D_MODEL = 1024
POOL_WINDOWS = (2, 4, 8, 16)
POOL_GROUPS = len(POOL_WINDOWS)
POOL_GROUP_DIM = D_MODEL // 8
POOL_WIDTH = POOL_GROUPS * POOL_GROUP_DIM
HEAD_DIM = 128
HEADS = D_MODEL // HEAD_DIM
HGRN_WIDTH = HEADS * HEAD_DIM
NORM_EPS = 1e-6
F_FLOOR = 1e-30
NEG_LOG2_E = -1.4426950408889634
IN_SIZES = (POOL_WIDTH, POOL_WIDTH, HGRN_WIDTH, HGRN_WIDTH, HGRN_WIDTH, HGRN_WIDTH, D_MODEL, D_MODEL)
IN_OFFSETS = tuple(sum(IN_SIZES[:i]) for i in range(len(IN_SIZES)))
(SEC_PV, SEC_PG, SEC_HQ, SEC_HF, SEC_HI, SEC_HG, SEC_MGP, SEC_MGH) = range(len(IN_SIZES))

GRID_TILE = 512
SEQ_TILE = 256
CHUNK = 64
HALO = max(POOL_WINDOWS)
SUBLANES = 8
TILES = CHUNK // SUBLANES
TILE_BITS = (0, 1, 5)
SUBLANE_BITS = (2, 3, 4)
TIME_BITS = 6
TILES_PER_GROUP = 4
ROW_STRIDE = 4
assert CHUNK == 1 << TIME_BITS
LOCAL_BITS = 2
assert TILE_BITS[:LOCAL_BITS] == tuple(range(LOCAL_BITS)) and TILES_PER_GROUP == 1 << LOCAL_BITS
N_OPS = TIME_BITS - LOCAL_BITS + 1
ADA_TILE = 512
VMEM_LIMIT_BYTES = 56 * 1024 * 1024

NT = (((1,), (1,)), ((), ()))
TN = (((0,), (0,)), ((), ()))


def _bdot(a, b):
    return jnp.dot(a.astype(jnp.bfloat16), b.astype(jnp.bfloat16),
                   preferred_element_type=jnp.float32)


def _sigmoid(z):
    return 1.0 / (1.0 + jnp.exp2(z * NEG_LOG2_E))


def _silu(z):
    return z * _sigmoid(z)


def _rms(x):
    return x * lax.rsqrt(jnp.mean(x * x, axis=-1, keepdims=True) + NORM_EPS)


def _roll_rows(a, shift):
    return pltpu.roll(a, shift % a.shape[0], axis=0)


def _token_of_row(p):
    return ((p >> 5) << 5) | ((p & 7) << 2) | ((p >> 3) & 3)


def _ada_kernel(c_ref, w_ref, b_ref, o_ref):
    c = c_ref[...]
    o_ref[...] = jnp.dot(_silu(c), w_ref[...], preferred_element_type=jnp.float32) + b_ref[...]


def _sublane_partner(p, sb, isub):
    k = 1
    while k < sb:
        p = jnp.where((isub & k) != 0, p, _roll_rows(p, -k))
        k *= 2
    if 2 * sb == p.shape[0]:
        return _roll_rows(p, sb)
    return jnp.where((isub & sb) != 0, _roll_rows(p, sb), _roll_rows(p, -sb))


def _last_row(a):
    return jnp.broadcast_to(a[SUBLANES - 1:SUBLANES, :], a.shape)


def _lane_sum(a):
    return jnp.sum(a, axis=-1, keepdims=True)


def _local_outputs(q, pf, kk, v):
    out = []
    for base in range(0, TILES, TILES_PER_GROUP):
        qs, fs, ks, vs = (a[base:base + TILES_PER_GROUP] for a in (q, pf, kk, v))
        acc = [_lane_sum(qs[j] * ks[j]) * vs[j] for j in range(TILES_PER_GROUP)]
        decayed_k = {}
        for j_t in range(1, TILES_PER_GROUP):
            qf = qs[j_t] * fs[j_t]
            for j_s in range(j_t - 1, -1, -1):
                decayed_k[j_t, j_s] = (ks[j_s] if j_s == j_t - 1
                                       else decayed_k[j_t - 1, j_s] * fs[j_t - 1])
                acc[j_t] = acc[j_t] + _lane_sum(qf * decayed_k[j_t, j_s]) * vs[j_s]
        out.extend(acc)
    return out


def _scan_operands(q, f, v, isub):
    tiles = range(TILES)
    kk = [1.0 - f[t] for t in tiles]
    pre = [jnp.maximum(f[t], F_FLOOR) for t in tiles]
    suf = [None] * TILES
    local = _local_outputs(q, pre, kk, v)
    zero = jnp.zeros_like(q[0])
    lhs, rhs = [], []
    for bit in range(TIME_BITS):
        if bit >= LOCAL_BITS:
            ksuf = [kk[t] if suf[t] is None else kk[t] * suf[t] for t in tiles]
        if bit in TILE_BITS:
            tb = 1 << TILE_BITS.index(bit)
            if bit >= LOCAL_BITS:
                lhs.append([q[t] * pre[t] if t & tb else zero for t in tiles])
                rhs.append([zero if t & tb else ksuf[t] for t in tiles])
        else:
            sb = 1 << SUBLANE_BITS.index(bit)
            upper = (isub & sb) != 0
            lhs.append([jnp.where(upper, q[t] * pre[t], 0.0) for t in tiles])
            rhs.append([jnp.where(upper, 0.0, ksuf[t]) for t in tiles])
        new_pre, new_suf = list(pre), list(suf)
        if bit in TILE_BITS:
            totals = {}
            for t in tiles:
                src = (t ^ tb) | (tb - 1)
                if src not in totals:
                    totals[src] = _last_row(pre[src]) if bit > max(SUBLANE_BITS) else pre[src]
                if t & tb:
                    new_pre[t] = pre[t] * totals[src]
                else:
                    new_suf[t] = totals[src] if suf[t] is None else suf[t] * totals[src]
        else:
            for base in range(0, TILES, TILES_PER_GROUP):
                partner = _sublane_partner(pre[base + TILES_PER_GROUP - 1], sb, isub)
                pre_mul = jnp.where(upper, partner, 1.0)
                suf_mul = jnp.where(upper, 1.0, partner)
                for t in range(base, base + TILES_PER_GROUP):
                    new_pre[t] = pre[t] * pre_mul
                    new_suf[t] = suf_mul if suf[t] is None else suf[t] * suf_mul
        pre, suf = new_pre, new_suf
    lhs.append([q[t] * pre[t] for t in tiles])
    rhs.append([kk[t] * suf[t] for t in tiles])
    return lhs, rhs, pre[TILES - 1], local


def _layer_kernel(x_ref, ada_ref, gpre_ref, gpost_ref, lbl_ref, pscale_ref, hng_ref, *refs, layer):
    win_refs = refs[:len(IN_SIZES)]
    (poolw_ref, wpo_ref, who_ref, wout_ref, o_ref,
     q_ref, f_ref, v_ref, g_ref, ob_ref, ext_ref, state_ref,
     lhs_ref, rhs_ref, vb_ref, tot_ref, loc_ref) = refs[len(IN_SIZES):]
    ts = SEQ_TILE
    si = pl.program_id(1)

    @pl.when(si == 0)
    def _():
        state_ref[...] = jnp.zeros_like(state_ref)
        ext_ref[0:HALO, :] = jnp.zeros((HALO, POOL_WIDTH), jnp.float32)

    def tile_pass(part, carry):
        r0 = pl.multiple_of(part * SEQ_TILE, SEQ_TILE)
        x = x_ref[pl.ds(r0, SEQ_TILE), :]
        shift = ada_ref[0:1, :]
        scale = ada_ref[1:2, :]
        gate = ada_ref[2:3, :]
        hb = ((_rms(x) * gpre_ref[...]) * (1.0 + scale) + shift).astype(jnp.bfloat16)

        def proj(section):
            return jnp.dot(hb, win_refs[section][...], preferred_element_type=jnp.float32)

        u = proj(SEC_PV)
        ext_ref[HALO:HALO + ts, :] = u
        first = si * GRID_TILE + r0 + 1
        pos = (first + lax.broadcasted_iota(jnp.int32, (ts, 1), 0)).astype(jnp.float32)
        mixed = []
        for gi, w in enumerate(POOL_WINDOWS):
            cols = slice(gi * POOL_GROUP_DIM, (gi + 1) * POOL_GROUP_DIM)
            acc = ext_ref[:, cols]
            step = 1
            while step < w:
                acc = acc + _roll_rows(acc, step)
                step *= 2
            inv_count = 1.0 / jnp.minimum(pos, float(w))
            pooled = acc[HALO:, :] * inv_count - u[:, cols]
            mixed.append(_bdot(pooled, poolw_ref[gi]))
        ext_ref[0:HALO, :] = u[ts - HALO:, :]
        pooled = jnp.concatenate(mixed, axis=-1) * pscale_ref[...]
        branch_a = _bdot(pooled * _silu(proj(SEC_PG)), wpo_ref[...])

        logits = lbl_ref[...]
        e = jnp.exp(logits - jnp.max(logits, axis=0, keepdims=True))
        lb = jnp.zeros((1, HGRN_WIDTH), jnp.float32)
        for j in range(1, layer + 1):
            lb = lb + e[j:j + 1, :]
        lb = jnp.clip(lb / jnp.sum(e, axis=0, keepdims=True), 0.0, 1.0)

        def to_heads(ref, a):
            for hd in range(HEADS):
                ref[hd] = a[:, hd * HEAD_DIM:(hd + 1) * HEAD_DIM]

        to_heads(q_ref, _silu(proj(SEC_HQ)))
        to_heads(f_ref, lb + (1.0 - lb) * _sigmoid(proj(SEC_HF)))
        to_heads(v_ref, proj(SEC_HI))
        to_heads(g_ref, _silu(proj(SEC_HG)))

        isub = lax.broadcasted_iota(jnp.int32, (SUBLANES, HEAD_DIM), 0)
        cat = functools.partial(jnp.concatenate, axis=0)

        def tile_rows(ci):
            return [pl.ds(ci * CHUNK + 32 * (t // TILES_PER_GROUP) + t % TILES_PER_GROUP, SUBLANES,
                          stride=ROW_STRIDE) for t in range(TILES)]

        def scan_chunk(ci):
            rows_of = tile_rows(ci)
            for hd in range(HEADS):
                q, f, v = ([ref[hd, rows, :] for rows in rows_of] for ref in (q_ref, f_ref, v_ref))
                lhs, rhs, total, local = _scan_operands(q, f, v, isub)
                for i in range(N_OPS):
                    lhs_ref[ci, hd, i] = cat(lhs[i]).astype(jnp.bfloat16)
                    rhs_ref[ci, hd, i] = cat(rhs[i]).astype(jnp.bfloat16)
                vb_ref[ci, hd] = cat(v).astype(jnp.bfloat16)
                tot_ref[ci, hd] = _last_row(total)
                loc_ref[ci, hd] = cat(local)

        r2 = _token_of_row(lax.broadcasted_iota(jnp.int32, (CHUNK, CHUNK), 0))
        c2 = _token_of_row(lax.broadcasted_iota(jnp.int32, (CHUNK, CHUNK), 1))
        blk_eq = [(r2 >> (bit + 1)) == (c2 >> (bit + 1)) for bit in range(TIME_BITS)]
        hng = hng_ref[...]

        def output_chunk(ci):
            rows_of = tile_rows(ci)
            heads = range(HEADS)

            def nt(hd, i):
                return lax.dot_general(lhs_ref[ci, hd, i], rhs_ref[ci, hd, i], NT,
                                       preferred_element_type=jnp.float32)

            scores = [None] * HEADS
            for bit in range(LOCAL_BITS, TIME_BITS):
                for hd in heads:
                    p = nt(hd, bit - LOCAL_BITS)
                    if bit < TIME_BITS - 1:
                        p = jnp.where(blk_eq[bit], p, 0.0)
                    scores[hd] = p if scores[hd] is None else scores[hd] + p
            state_t = [state_ref[hd] for hd in heads]
            o_inter = [lax.dot_general(lhs_ref[ci, hd, N_OPS - 1], state_t[hd].astype(jnp.bfloat16), NT,
                                       preferred_element_type=jnp.float32) for hd in heads]
            for hd in heads:
                state_ref[hd] = (state_t[hd] * tot_ref[ci, hd][0:1, :]
                                 + lax.dot_general(vb_ref[ci, hd], rhs_ref[ci, hd, N_OPS - 1], TN,
                                                   preferred_element_type=jnp.float32))
            for hd in heads:
                o = (o_inter[hd] + loc_ref[ci, hd]
                     + jnp.dot(scores[hd].astype(jnp.bfloat16), vb_ref[ci, hd],
                               preferred_element_type=jnp.float32))
                for t, rows in enumerate(rows_of):
                    o_tile = o[SUBLANES * t:SUBLANES * (t + 1), :]
                    ob_ref[hd, rows, :] = (_rms(o_tile) * hng) * g_ref[hd, rows, :]

        for ci in range(ts // CHUNK):
            scan_chunk(ci)
        for ci in range(ts // CHUNK):
            output_chunk(ci)
        gated = jnp.concatenate([ob_ref[hd] for hd in range(HEADS)], axis=-1)
        branch_b = _bdot(gated, who_ref[...])

        merged = _sigmoid(proj(SEC_MGP)) * branch_a + _sigmoid(proj(SEC_MGH)) * branch_b
        y = _bdot(merged, wout_ref[...])
        o_ref[pl.ds(r0, SEQ_TILE), :] = x + gate * (_rms(y) * gpost_ref[...])
        return carry

    lax.fori_loop(0, GRID_TILE // SEQ_TILE, tile_pass, 0)


def _resident(shape, index_map):
    return pl.BlockSpec(shape, index_map, pipeline_mode=pl.Buffered(1))


def _layer_call(layer, x, ada, g_pre, g_post, lb_logits, pool_scale, hgrn_norm_g,
                w_in_sections, pool_w, w_pool_o, w_hgrn_o, w_out):
    batch, seq, d = x.shape
    depth = lb_logits.shape[0]
    ts = SEQ_TILE
    assert seq % GRID_TILE == 0 and GRID_TILE % ts == 0 and ts % CHUNK == 0 and d == D_MODEL
    chunks = ts // CHUNK

    def const2(b, s):
        return (0, 0)

    def per_layer3(b, s):
        return (layer, 0, 0)

    in_specs = [
        pl.BlockSpec((None, GRID_TILE, d), lambda b, s: (b, s, 0)),
        pl.BlockSpec((None, 3, d), lambda b, s: (b, 0, 0)),
        _resident((1, d), const2),
        _resident((1, d), const2),
        _resident((depth, HGRN_WIDTH), const2),
        _resident((1, POOL_WIDTH), const2),
        _resident((1, HEAD_DIM), const2),
        *[_resident((None, d, size), per_layer3) for size in IN_SIZES],
        _resident((None, POOL_GROUPS, POOL_GROUP_DIM, POOL_GROUP_DIM), lambda b, s: (layer, 0, 0, 0)),
        _resident((None, POOL_WIDTH, d), per_layer3),
        _resident((None, HGRN_WIDTH, d), per_layer3),
        _resident((None, d, d), per_layer3),
    ]
    scratch = [
        pltpu.VMEM((HEADS, ts, HEAD_DIM), jnp.float32),
        pltpu.VMEM((HEADS, ts, HEAD_DIM), jnp.float32),
        pltpu.VMEM((HEADS, ts, HEAD_DIM), jnp.float32),
        pltpu.VMEM((HEADS, ts, HEAD_DIM), jnp.float32),
        pltpu.VMEM((HEADS, ts, HEAD_DIM), jnp.float32),
        pltpu.VMEM((HALO + ts, POOL_WIDTH), jnp.float32),
        pltpu.VMEM((HEADS, HEAD_DIM, HEAD_DIM), jnp.float32),
        pltpu.VMEM((chunks, HEADS, N_OPS, CHUNK, HEAD_DIM), jnp.bfloat16),
        pltpu.VMEM((chunks, HEADS, N_OPS, CHUNK, HEAD_DIM), jnp.bfloat16),
        pltpu.VMEM((chunks, HEADS, CHUNK, HEAD_DIM), jnp.bfloat16),
        pltpu.VMEM((chunks, HEADS, SUBLANES, HEAD_DIM), jnp.float32),
        pltpu.VMEM((chunks, HEADS, CHUNK, HEAD_DIM), jnp.float32),
    ]
    return pl.pallas_call(
        functools.partial(_layer_kernel, layer=layer),
        grid=(batch, seq // GRID_TILE),
        in_specs=in_specs,
        out_specs=pl.BlockSpec((None, GRID_TILE, d), lambda b, s: (b, s, 0)),
        out_shape=jax.ShapeDtypeStruct(x.shape, x.dtype),
        scratch_shapes=scratch,
        compiler_params=pltpu.CompilerParams(
            dimension_semantics=("arbitrary", "arbitrary"),
            vmem_limit_bytes=VMEM_LIMIT_BYTES),
        name=f"hybrid_layer{layer}",
    )(x, ada, g_pre[layer][None, :], g_post[layer][None, :], lb_logits,
      pool_scale[layer][None, :], hgrn_norm_g[layer][None, :],
      *w_in_sections, pool_w, w_pool_o, w_hgrn_o, w_out)


def _ada_call(c, w_ada, b_ada):
    depth, d, width = w_ada.shape
    batch = c.shape[0]
    assert width % ADA_TILE == 0
    return pl.pallas_call(
        _ada_kernel,
        grid=(depth, width // ADA_TILE),
        in_specs=[
            pl.BlockSpec((batch, d), lambda l, n: (0, 0)),
            pl.BlockSpec((None, d, ADA_TILE), lambda l, n: (l, 0, n)),
            pl.BlockSpec((None, 1, ADA_TILE), lambda l, n: (l, 0, n)),
        ],
        out_specs=pl.BlockSpec((None, batch, ADA_TILE), lambda l, n: (l, 0, n)),
        out_shape=jax.ShapeDtypeStruct((depth, batch, width), jnp.float32),
        name="adaln_vectors",
    )(c, w_ada, b_ada[:, None, :])


def kernel(x, c, w_ada, b_ada, g_pre, g_post, w_in, pool_w, pool_scale, lb_logits,
           hgrn_norm_g, w_pool_o, w_hgrn_o, w_out):
    depth = w_in.shape[0]
    batch, _, d = x.shape
    ada = _ada_call(c, w_ada, b_ada).reshape(depth, batch, 3, d)
    bf = jnp.bfloat16
    w_in_b = [w_in[:, :, off:off + size].astype(bf) for off, size in zip(IN_OFFSETS, IN_SIZES)]
    pool_w_b = pool_w.astype(bf)
    w_pool_o_b, w_hgrn_o_b, w_out_b = w_pool_o.astype(bf), w_hgrn_o.astype(bf), w_out.astype(bf)
    for layer in range(depth):
        x = _layer_call(layer, x, ada[layer], g_pre, g_post, lb_logits, pool_scale, hgrn_norm_g,
                        w_in_b, pool_w_b, w_pool_o_b, w_hgrn_o_b, w_out_b)
    return x
```

```python
import functools

import jax
import jax.numpy as jnp
from jax import lax
from jax.experimental import pallas as pl
from jax.experimental.pallas import tpu as pltpu

D_MODEL = 1024
POOL_WINDOWS = (2, 4, 8, 16)
POOL_GROUPS = len(POOL_WINDOWS)
POOL_GROUP_DIM = D_MODEL // 8
POOL_WIDTH = POOL_GROUPS * POOL_GROUP_DIM
HEAD_DIM = 128
HEADS = D_MODEL // HEAD_DIM
HGRN_WIDTH = HEADS * HEAD_DIM
NORM_EPS = 1e-6
F_FLOOR = 1e-30
NEG_LOG2_E = -1.4426950408889634
IN_SIZES = (POOL_WIDTH, POOL_WIDTH, HGRN_WIDTH, HGRN_WIDTH, HGRN_WIDTH, HGRN_WIDTH, D_MODEL, D_MODEL)
IN_OFFSETS = tuple(sum(IN_SIZES[:i]) for i in range(len(IN_SIZES)))
(SEC_PV, SEC_PG, SEC_HQ, SEC_HF, SEC_HI, SEC_HG, SEC_MGP, SEC_MGH) = range(len(IN_SIZES))

GRID_TILE = 512
SEQ_TILE = 256
CHUNK = 64
HALO = max(POOL_WINDOWS)
SUBLANES = 8
TILES = CHUNK // SUBLANES
TILE_BITS = (0, 1, 5)
SUBLANE_BITS = (2, 3, 4)
TIME_BITS = 6
TILES_PER_GROUP = 4
ROW_STRIDE = 4
assert CHUNK == 1 << TIME_BITS
LOCAL_BITS = 2
assert TILE_BITS[:LOCAL_BITS] == tuple(range(LOCAL_BITS)) and TILES_PER_GROUP == 1 << LOCAL_BITS
N_OPS = TIME_BITS - LOCAL_BITS + 1
ADA_TILE = 512
VMEM_LIMIT_BYTES = 56 * 1024 * 1024

NT = (((1,), (1,)), ((), ()))
TN = (((0,), (0,)), ((), ()))


def _bdot(a, b):
    return jnp.dot(a.astype(jnp.bfloat16), b.astype(jnp.bfloat16),
                   preferred_element_type=jnp.float32)


def _sigmoid(z):
    return 1.0 / (1.0 + jnp.exp2(z * NEG_LOG2_E))


def _silu(z):
    return z * _sigmoid(z)


def _rms(x):
    return x * lax.rsqrt(jnp.mean(x * x, axis=-1, keepdims=True) + NORM_EPS)


def _roll_rows(a, shift):
    return pltpu.roll(a, shift % a.shape[0], axis=0)


def _token_of_row(p):
    return ((p >> 5) << 5) | ((p & 7) << 2) | ((p >> 3) & 3)


def _ada_kernel(c_ref, w_ref, b_ref, o_ref):
    c = c_ref[...]
    o_ref[...] = jnp.dot(_silu(c), w_ref[...], preferred_element_type=jnp.float32) + b_ref[...]


def _sublane_partner(p, sb, isub):
    k = 1
    while k < sb:
        p = jnp.where((isub & k) != 0, p, _roll_rows(p, -k))
        k *= 2
    if 2 * sb == p.shape[0]:
        return _roll_rows(p, sb)
    return jnp.where((isub & sb) != 0, _roll_rows(p, sb), _roll_rows(p, -sb))


def _last_row(a):
    return jnp.broadcast_to(a[SUBLANES - 1:SUBLANES, :], a.shape)


def _lane_sum(a):
    return jnp.sum(a, axis=-1, keepdims=True)


def _local_outputs(q, pf, kk, v):
    out = []
    for base in range(0, TILES, TILES_PER_GROUP):
        qs, fs, ks, vs = (a[base:base + TILES_PER_GROUP] for a in (q, pf, kk, v))
        acc = [_lane_sum(qs[j] * ks[j]) * vs[j] for j in range(TILES_PER_GROUP)]
        decayed_k = {}
        for j_t in range(1, TILES_PER_GROUP):
            qf = qs[j_t] * fs[j_t]
            for j_s in range(j_t - 1, -1, -1):
                decayed_k[j_t, j_s] = (ks[j_s] if j_s == j_t - 1
                                       else decayed_k[j_t - 1, j_s] * fs[j_t - 1])
                acc[j_t] = acc[j_t] + _lane_sum(qf * decayed_k[j_t, j_s]) * vs[j_s]
        out.extend(acc)
    return out


def _scan_operands(q, f, v, isub):
    tiles = range(TILES)
    kk = [1.0 - f[t] for t in tiles]
    pre = [jnp.maximum(f[t], F_FLOOR) for t in tiles]
    suf = [None] * TILES
    local = _local_outputs(q, pre, kk, v)
    zero = jnp.zeros_like(q[0])
    lhs, rhs = [], []
    for bit in range(TIME_BITS):
        if bit >= LOCAL_BITS:
            ksuf = [kk[t] if suf[t] is None else kk[t] * suf[t] for t in tiles]
        if bit in TILE_BITS:
            tb = 1 << TILE_BITS.index(bit)
            if bit >= LOCAL_BITS:
                lhs.append([q[t] * pre[t] if t & tb else zero for t in tiles])
                rhs.append([zero if t & tb else ksuf[t] for t in tiles])
        else:
            sb = 1 << SUBLANE_BITS.index(bit)
            upper = (isub & sb) != 0
            lhs.append([jnp.where(upper, q[t] * pre[t], 0.0) for t in tiles])
            rhs.append([jnp.where(upper, 0.0, ksuf[t]) for t in tiles])
        new_pre, new_suf = list(pre), list(suf)
        if bit in TILE_BITS:
            totals = {}
            for t in tiles:
                src = (t ^ tb) | (tb - 1)
                if src not in totals:
                    totals[src] = _last_row(pre[src]) if bit > max(SUBLANE_BITS) else pre[src]
                if t & tb:
                    new_pre[t] = pre[t] * totals[src]
                else:
                    new_suf[t] = totals[src] if suf[t] is None else suf[t] * totals[src]
        else:
            for base in range(0, TILES, TILES_PER_GROUP):
                partner = _sublane_partner(pre[base + TILES_PER_GROUP - 1], sb, isub)
                pre_mul = jnp.where(upper, partner, 1.0)
                suf_mul = jnp.where(upper, 1.0, partner)
                for t in range(base, base + TILES_PER_GROUP):
                    new_pre[t] = pre[t] * pre_mul
                    new_suf[t] = suf_mul if suf[t] is None else suf[t] * suf_mul
        pre, suf = new_pre, new_suf
    lhs.append([q[t] * pre[t] for t in tiles])
    rhs.append([kk[t] * suf[t] for t in tiles])
    return lhs, rhs, pre[TILES - 1], local


def _layer_kernel(x_ref, ada_ref, gpre_ref, gpost_ref, lbl_ref, pscale_ref, hng_ref, *refs, layer):
    win_refs = refs[:len(IN_SIZES)]
    (poolw_ref, wpo_ref, who_ref, wout_ref, o_ref,
     q_ref, f_ref, v_ref, g_ref, ob_ref, ext_ref, state_ref,
     lhs_ref, rhs_ref, vb_ref, tot_ref, loc_ref) = refs[len(IN_SIZES):]
    ts = SEQ_TILE
    si = pl.program_id(1)

    @pl.when(si == 0)
    def _():
        state_ref[...] = jnp.zeros_like(state_ref)
        ext_ref[0:HALO, :] = jnp.zeros((HALO, POOL_WIDTH), jnp.float32)

    def tile_pass(part, carry):
        r0 = pl.multiple_of(part * SEQ_TILE, SEQ_TILE)
        x = x_ref[pl.ds(r0, SEQ_TILE), :]
        shift = ada_ref[0:1, :]
        scale = ada_ref[1:2, :]
        gate = ada_ref[2:3, :]
        hb = ((_rms(x) * gpre_ref[...]) * (1.0 + scale) + shift).astype(jnp.bfloat16)

        def proj(section):
            return jnp.dot(hb, win_refs[section][...], preferred_element_type=jnp.float32)

        u = proj(SEC_PV)
        ext_ref[HALO:HALO + ts, :] = u
        first = si * GRID_TILE + r0 + 1
        pos = (first + lax.broadcasted_iota(jnp.int32, (ts, 1), 0)).astype(jnp.float32)
        mixed = []
        for gi, w in enumerate(POOL_WINDOWS):
            cols = slice(gi * POOL_GROUP_DIM, (gi + 1) * POOL_GROUP_DIM)
            acc = ext_ref[:, cols]
            step = 1
            while step < w:
                acc = acc + _roll_rows(acc, step)
                step *= 2
            inv_count = 1.0 / jnp.minimum(pos, float(w))
            pooled = acc[HALO:, :] * inv_count - u[:, cols]
            mixed.append(_bdot(pooled, poolw_ref[gi]))
        ext_ref[0:HALO, :] = u[ts - HALO:, :]
        pooled = jnp.concatenate(mixed, axis=-1) * pscale_ref[...]
        branch_a = _bdot(pooled * _silu(proj(SEC_PG)), wpo_ref[...])

        logits = lbl_ref[...]
        e = jnp.exp(logits - jnp.max(logits, axis=0, keepdims=True))
        lb = jnp.zeros((1, HGRN_WIDTH), jnp.float32)
        for j in range(1, layer + 1):
            lb = lb + e[j:j + 1, :]
        lb = jnp.clip(lb / jnp.sum(e, axis=0, keepdims=True), 0.0, 1.0)

        def to_heads(ref, a):
            for hd in range(HEADS):
                ref[hd] = a[:, hd * HEAD_DIM:(hd + 1) * HEAD_DIM]

        to_heads(q_ref, _silu(proj(SEC_HQ)))
        to_heads(f_ref, lb + (1.0 - lb) * _sigmoid(proj(SEC_HF)))
        to_heads(v_ref, proj(SEC_HI))
        to_heads(g_ref, _silu(proj(SEC_HG)))

        isub = lax.broadcasted_iota(jnp.int32, (SUBLANES, HEAD_DIM), 0)
        cat = functools.partial(jnp.concatenate, axis=0)

        def tile_rows(ci):
            return [pl.ds(ci * CHUNK + 32 * (t // TILES_PER_GROUP) + t % TILES_PER_GROUP, SUBLANES,
                          stride=ROW_STRIDE) for t in range(TILES)]

        def scan_chunk(ci):
            rows_of = tile_rows(ci)
            for hd in range(HEADS):
                q, f, v = ([ref[hd, rows, :] for rows in rows_of] for ref in (q_ref, f_ref, v_ref))
                lhs, rhs, total, local = _scan_operands(q, f, v, isub)
                for i in range(N_OPS):
                    lhs_ref[ci, hd, i] = cat(lhs[i]).astype(jnp.bfloat16)
                    rhs_ref[ci, hd, i] = cat(rhs[i]).astype(jnp.bfloat16)
                vb_ref[ci, hd] = cat(v).astype(jnp.bfloat16)
                tot_ref[ci, hd] = _last_row(total)
                loc_ref[ci, hd] = cat(local)

        r2 = _token_of_row(lax.broadcasted_iota(jnp.int32, (CHUNK, CHUNK), 0))
        c2 = _token_of_row(lax.broadcasted_iota(jnp.int32, (CHUNK, CHUNK), 1))
        blk_eq = [(r2 >> (bit + 1)) == (c2 >> (bit + 1)) for bit in range(TIME_BITS)]
        hng = hng_ref[...]

        def output_chunk(ci):
            rows_of = tile_rows(ci)
            heads = range(HEADS)

            def nt(hd, i):
                return lax.dot_general(lhs_ref[ci, hd, i], rhs_ref[ci, hd, i], NT,
                                       preferred_element_type=jnp.float32)

            scores = [None] * HEADS
            for bit in range(LOCAL_BITS, TIME_BITS):
                for hd in heads:
                    p = nt(hd, bit - LOCAL_BITS)
                    if bit < TIME_BITS - 1:
                        p = jnp.where(blk_eq[bit], p, 0.0)
                    scores[hd] = p if scores[hd] is None else scores[hd] + p
            state_t = [state_ref[hd] for hd in heads]
            o_inter = [lax.dot_general(lhs_ref[ci, hd, N_OPS - 1], state_t[hd].astype(jnp.bfloat16), NT,
                                       preferred_element_type=jnp.float32) for hd in heads]
            for hd in heads:
                state_ref[hd] = (state_t[hd] * tot_ref[ci, hd][0:1, :]
                                 + lax.dot_general(vb_ref[ci, hd], rhs_ref[ci, hd, N_OPS - 1], TN,
                                                   preferred_element_type=jnp.float32))
            for hd in heads:
                o = (o_inter[hd] + loc_ref[ci, hd]
                     + jnp.dot(scores[hd].astype(jnp.bfloat16), vb_ref[ci, hd],
                               preferred_element_type=jnp.float32))
                for t, rows in enumerate(rows_of):
                    o_tile = o[SUBLANES * t:SUBLANES * (t + 1), :]
                    ob_ref[hd, rows, :] = (_rms(o_tile) * hng) * g_ref[hd, rows, :]

        for ci in range(ts // CHUNK):
            scan_chunk(ci)
        for ci in range(ts // CHUNK):
            output_chunk(ci)
        gated = jnp.concatenate([ob_ref[hd] for hd in range(HEADS)], axis=-1)
        branch_b = _bdot(gated, who_ref[...])

        merged = _sigmoid(proj(SEC_MGP)) * branch_a + _sigmoid(proj(SEC_MGH)) * branch_b
        y = _bdot(merged, wout_ref[...])
        o_ref[pl.ds(r0, SEQ_TILE), :] = x + gate * (_rms(y) * gpost_ref[...])
        return carry

    lax.fori_loop(0, GRID_TILE // SEQ_TILE, tile_pass, 0)


def _resident(shape, index_map):
    return pl.BlockSpec(shape, index_map, pipeline_mode=pl.Buffered(1))


def _layer_call(layer, x, ada, g_pre, g_post, lb_logits, pool_scale, hgrn_norm_g,
                w_in_sections, pool_w, w_pool_o, w_hgrn_o, w_out):
    batch, seq, d = x.shape
    depth = lb_logits.shape[0]
    ts = SEQ_TILE
    assert seq % GRID_TILE == 0 and GRID_TILE % ts == 0 and ts % CHUNK == 0 and d == D_MODEL
    chunks = ts // CHUNK

    def const2(b, s):
        return (0, 0)

    def per_layer3(b, s):
        return (layer, 0, 0)

    in_specs = [
        pl.BlockSpec((None, GRID_TILE, d), lambda b, s: (b, s, 0)),
        pl.BlockSpec((None, 3, d), lambda b, s: (b, 0, 0)),
        _resident((1, d), const2),
        _resident((1, d), const2),
        _resident((depth, HGRN_WIDTH), const2),
        _resident((1, POOL_WIDTH), const2),
        _resident((1, HEAD_DIM), const2),
        *[_resident((None, d, size), per_layer3) for size in IN_SIZES],
        _resident((None, POOL_GROUPS, POOL_GROUP_DIM, POOL_GROUP_DIM), lambda b, s: (layer, 0, 0, 0)),
        _resident((None, POOL_WIDTH, d), per_layer3),
        _resident((None, HGRN_WIDTH, d), per_layer3),
        _resident((None, d, d), per_layer3),
    ]
    scratch = [
        pltpu.VMEM((HEADS, ts, HEAD_DIM), jnp.float32),
        pltpu.VMEM((HEADS, ts, HEAD_DIM), jnp.float32),
        pltpu.VMEM((HEADS, ts, HEAD_DIM), jnp.float32),
        pltpu.VMEM((HEADS, ts, HEAD_DIM), jnp.float32),
        pltpu.VMEM((HEADS, ts, HEAD_DIM), jnp.float32),
        pltpu.VMEM((HALO + ts, POOL_WIDTH), jnp.float32),
        pltpu.VMEM((HEADS, HEAD_DIM, HEAD_DIM), jnp.float32),
        pltpu.VMEM((chunks, HEADS, N_OPS, CHUNK, HEAD_DIM), jnp.bfloat16),
        pltpu.VMEM((chunks, HEADS, N_OPS, CHUNK, HEAD_DIM), jnp.bfloat16),
        pltpu.VMEM((chunks, HEADS, CHUNK, HEAD_DIM), jnp.bfloat16),
        pltpu.VMEM((chunks, HEADS, SUBLANES, HEAD_DIM), jnp.float32),
        pltpu.VMEM((chunks, HEADS, CHUNK, HEAD_DIM), jnp.float32),
    ]
    return pl.pallas_call(
        functools.partial(_layer_kernel, layer=layer),
        grid=(batch, seq // GRID_TILE),
        in_specs=in_specs,
        out_specs=pl.BlockSpec((None, GRID_TILE, d), lambda b, s: (b, s, 0)),
        out_shape=jax.ShapeDtypeStruct(x.shape, x.dtype),
        scratch_shapes=scratch,
        compiler_params=pltpu.CompilerParams(
            dimension_semantics=("arbitrary", "arbitrary"),
            vmem_limit_bytes=VMEM_LIMIT_BYTES),
        name=f"hybrid_layer{layer}",
    )(x, ada, g_pre[layer][None, :], g_post[layer][None, :], lb_logits,
      pool_scale[layer][None, :], hgrn_norm_g[layer][None, :],
      *w_in_sections, pool_w, w_pool_o, w_hgrn_o, w_out)


def _ada_call(c, w_ada, b_ada):
    depth, d, width = w_ada.shape
    batch = c.shape[0]
    assert width % ADA_TILE == 0
    return pl.pallas_call(
        _ada_kernel,
        grid=(depth, width // ADA_TILE),
        in_specs=[
            pl.BlockSpec((batch, d), lambda l, n: (0, 0)),
            pl.BlockSpec((None, d, ADA_TILE), lambda l, n: (l, 0, n)),
            pl.BlockSpec((None, 1, ADA_TILE), lambda l, n: (l, 0, n)),
        ],
        out_specs=pl.BlockSpec((None, batch, ADA_TILE), lambda l, n: (l, 0, n)),
        out_shape=jax.ShapeDtypeStruct((depth, batch, width), jnp.float32),
        name="adaln_vectors",
    )(c, w_ada, b_ada[:, None, :])


def _probe_kernel(*refs):
    o_ref = refs[-1]
    acc = jnp.zeros((8, 128), jnp.float32)
    for r in refs[:-1]:
        acc = acc + r[0:8, 0:128].astype(jnp.float32)
    o_ref[...] = acc


def _probe_call(weights):
    return pl.pallas_call(
        _probe_kernel,
        grid=(1,),
        in_specs=[pl.BlockSpec((None,) + w.shape[1:], lambda i: (0, 0, 0), pipeline_mode=pl.Buffered(1))
                  for w in weights],
        out_specs=pl.BlockSpec((8, 128), lambda i: (0, 0)),
        out_shape=jax.ShapeDtypeStruct((8, 128), jnp.float32),
        compiler_params=pltpu.CompilerParams(vmem_limit_bytes=VMEM_LIMIT_BYTES),
        name="weight_load_probe",
    )(*weights)


def kernel(x, c, w_ada, b_ada, g_pre, g_post, w_in, pool_w, pool_scale, lb_logits,
           hgrn_norm_g, w_pool_o, w_hgrn_o, w_out):
    depth = w_in.shape[0]
    batch, _, d = x.shape
    ada = _ada_call(c, w_ada, b_ada).reshape(depth, batch, 3, d)
    bf = jnp.bfloat16
    w_in_b = [w_in[:, :, off:off + size].astype(bf) for off, size in zip(IN_OFFSETS, IN_SIZES)]
    pool_w_b = pool_w.astype(bf)
    w_pool_o_b, w_hgrn_o_b, w_out_b = w_pool_o.astype(bf), w_hgrn_o.astype(bf), w_out.astype(bf)
    for layer in range(depth):
        x = _layer_call(layer, x, ada[layer], g_pre, g_post, lb_logits, pool_scale, hgrn_norm_g,
                        w_in_b, pool_w_b, w_pool_o_b, w_hgrn_o_b, w_out_b)
    probe = _probe_call([*w_in_b, w_pool_o_b, w_hgrn_o_b, w_out_b])
    return x + 0.0 * probe[0, 0]
```

```python
import functools

import jax
import jax.numpy as jnp
from jax import lax
from jax.experimental import pallas as pl
from jax.experimental.pallas import tpu as pltpu

D_MODEL = 1024
POOL_WINDOWS = (2, 4, 8, 16)
POOL_GROUPS = len(POOL_WINDOWS)
POOL_GROUP_DIM = D_MODEL // 8
POOL_WIDTH = POOL_GROUPS * POOL_GROUP_DIM
HEAD_DIM = 128
HEADS = D_MODEL // HEAD_DIM
HGRN_WIDTH = HEADS * HEAD_DIM
NORM_EPS = 1e-6
F_FLOOR = 1e-30
NEG_LOG2_E = -1.4426950408889634
IN_SIZES = (POOL_WIDTH, POOL_WIDTH, HGRN_WIDTH, HGRN_WIDTH, HGRN_WIDTH, HGRN_WIDTH, D_MODEL, D_MODEL)
IN_OFFSETS = tuple(sum(IN_SIZES[:i]) for i in range(len(IN_SIZES)))
(SEC_PV, SEC_PG, SEC_HQ, SEC_HF, SEC_HI, SEC_HG, SEC_MGP, SEC_MGH) = range(len(IN_SIZES))

GRID_TILE = 512
SEQ_TILE = 256
CHUNK = 64
HALO = max(POOL_WINDOWS)
SUBLANES = 8
TILES = CHUNK // SUBLANES
TILE_BITS = (0, 1, 5)
SUBLANE_BITS = (2, 3, 4)
TIME_BITS = 6
TILES_PER_GROUP = 4
ROW_STRIDE = 4
assert CHUNK == 1 << TIME_BITS
LOCAL_BITS = 2
assert TILE_BITS[:LOCAL_BITS] == tuple(range(LOCAL_BITS)) and TILES_PER_GROUP == 1 << LOCAL_BITS
N_OPS = TIME_BITS - LOCAL_BITS + 1
ADA_TILE = 512
VMEM_LIMIT_BYTES = 56 * 1024 * 1024

NT = (((1,), (1,)), ((), ()))
TN = (((0,), (0,)), ((), ()))


def _bdot(a, b):
    return jnp.dot(a.astype(jnp.bfloat16), b.astype(jnp.bfloat16),
                   preferred_element_type=jnp.float32)


def _sigmoid(z):
    return 1.0 / (1.0 + jnp.exp2(z * NEG_LOG2_E))


def _silu(z):
    return z * _sigmoid(z)


def _rms(x):
    return x * lax.rsqrt(jnp.mean(x * x, axis=-1, keepdims=True) + NORM_EPS)


def _roll_rows(a, shift):
    return pltpu.roll(a, shift % a.shape[0], axis=0)


def _token_of_row(p):
    return ((p >> 5) << 5) | ((p & 7) << 2) | ((p >> 3) & 3)


def _ada_kernel(c_ref, w_ref, b_ref, o_ref):
    c = c_ref[...]
    o_ref[...] = jnp.dot(_silu(c), w_ref[...], preferred_element_type=jnp.float32) + b_ref[...]


def _sublane_partner(p, sb, isub):
    k = 1
    while k < sb:
        p = jnp.where((isub & k) != 0, p, _roll_rows(p, -k))
        k *= 2
    if 2 * sb == p.shape[0]:
        return _roll_rows(p, sb)
    return jnp.where((isub & sb) != 0, _roll_rows(p, sb), _roll_rows(p, -sb))


def _last_row(a):
    return jnp.broadcast_to(a[SUBLANES - 1:SUBLANES, :], a.shape)


def _lane_sum(a):
    return jnp.sum(a, axis=-1, keepdims=True)


def _local_outputs(q, pf, kk, v):
    out = []
    for base in range(0, TILES, TILES_PER_GROUP):
        qs, fs, ks, vs = (a[base:base + TILES_PER_GROUP] for a in (q, pf, kk, v))
        acc = [_lane_sum(qs[j] * ks[j]) * vs[j] for j in range(TILES_PER_GROUP)]
        decayed_k = {}
        for j_t in range(1, TILES_PER_GROUP):
            qf = qs[j_t] * fs[j_t]
            for j_s in range(j_t - 1, -1, -1):
                decayed_k[j_t, j_s] = (ks[j_s] if j_s == j_t - 1
                                       else decayed_k[j_t - 1, j_s] * fs[j_t - 1])
                acc[j_t] = acc[j_t] + _lane_sum(qf * decayed_k[j_t, j_s]) * vs[j_s]
        out.extend(acc)
    return out


def _scan_operands(q, f, v, isub):
    tiles = range(TILES)
    kk = [1.0 - f[t] for t in tiles]
    pre = [jnp.maximum(f[t], F_FLOOR) for t in tiles]
    suf = [None] * TILES
    local = _local_outputs(q, pre, kk, v)
    zero = jnp.zeros_like(q[0])
    lhs, rhs = [], []
    for bit in range(TIME_BITS):
        if bit >= LOCAL_BITS:
            ksuf = [kk[t] if suf[t] is None else kk[t] * suf[t] for t in tiles]
        if bit in TILE_BITS:
            tb = 1 << TILE_BITS.index(bit)
            if bit >= LOCAL_BITS:
                lhs.append([q[t] * pre[t] if t & tb else zero for t in tiles])
                rhs.append([zero if t & tb else ksuf[t] for t in tiles])
        else:
            sb = 1 << SUBLANE_BITS.index(bit)
            upper = (isub & sb) != 0
            lhs.append([jnp.where(upper, q[t] * pre[t], 0.0) for t in tiles])
            rhs.append([jnp.where(upper, 0.0, ksuf[t]) for t in tiles])
        new_pre, new_suf = list(pre), list(suf)
        if bit in TILE_BITS:
            totals = {}
            for t in tiles:
                src = (t ^ tb) | (tb - 1)
                if src not in totals:
                    totals[src] = _last_row(pre[src]) if bit > max(SUBLANE_BITS) else pre[src]
                if t & tb:
                    new_pre[t] = pre[t] * totals[src]
                else:
                    new_suf[t] = totals[src] if suf[t] is None else suf[t] * totals[src]
        else:
            for base in range(0, TILES, TILES_PER_GROUP):
                partner = _sublane_partner(pre[base + TILES_PER_GROUP - 1], sb, isub)
                pre_mul = jnp.where(upper, partner, 1.0)
                suf_mul = jnp.where(upper, 1.0, partner)
                for t in range(base, base + TILES_PER_GROUP):
                    new_pre[t] = pre[t] * pre_mul
                    new_suf[t] = suf_mul if suf[t] is None else suf[t] * suf_mul
        pre, suf = new_pre, new_suf
    lhs.append([q[t] * pre[t] for t in tiles])
    rhs.append([kk[t] * suf[t] for t in tiles])
    return lhs, rhs, pre[TILES - 1], local


def _layer_kernel(x_ref, ada_ref, gpre_ref, gpost_ref, lbl_ref, pscale_ref, hng_ref,
                  win_ref, poolw_ref, wpo_ref, who_ref, wout_ref, o_ref,
                  q_ref, f_ref, v_ref, g_ref, ob_ref, ext_ref, state_ref,
                  lhs_ref, rhs_ref, vb_ref, tot_ref, loc_ref, *, layer):
    ts = SEQ_TILE
    si = pl.program_id(1)

    @pl.when(si == 0)
    def _():
        state_ref[...] = jnp.zeros_like(state_ref)
        ext_ref[0:HALO, :] = jnp.zeros((HALO, POOL_WIDTH), jnp.float32)

    def tile_pass(part, carry):
        r0 = pl.multiple_of(part * SEQ_TILE, SEQ_TILE)
        x = x_ref[pl.ds(r0, SEQ_TILE), :]
        shift = ada_ref[0:1, :]
        scale = ada_ref[1:2, :]
        gate = ada_ref[2:3, :]
        hb = ((_rms(x) * gpre_ref[...]) * (1.0 + scale) + shift).astype(jnp.bfloat16)

        def proj(section):
            cols = slice(IN_OFFSETS[section], IN_OFFSETS[section] + IN_SIZES[section])
            return jnp.dot(hb, win_ref[:, cols], preferred_element_type=jnp.float32)

        logits = lbl_ref[...]
        e = jnp.exp(logits - jnp.max(logits, axis=0, keepdims=True))
        lb = jnp.zeros((1, HGRN_WIDTH), jnp.float32)
        for j in range(1, layer + 1):
            lb = lb + e[j:j + 1, :]
        lb = jnp.clip(lb / jnp.sum(e, axis=0, keepdims=True), 0.0, 1.0)

        def to_heads(ref, a):
            for hd in range(HEADS):
                ref[hd] = a[:, hd * HEAD_DIM:(hd + 1) * HEAD_DIM]

        to_heads(q_ref, _silu(proj(SEC_HQ)))
        to_heads(f_ref, lb + (1.0 - lb) * _sigmoid(proj(SEC_HF)))
        to_heads(v_ref, proj(SEC_HI))
        to_heads(g_ref, _silu(proj(SEC_HG)))

        u = proj(SEC_PV)
        ext_ref[HALO:HALO + ts, :] = u
        first = si * GRID_TILE + r0 + 1
        pos = (first + lax.broadcasted_iota(jnp.int32, (ts, 1), 0)).astype(jnp.float32)
        mixed = []
        for gi, w in enumerate(POOL_WINDOWS):
            cols = slice(gi * POOL_GROUP_DIM, (gi + 1) * POOL_GROUP_DIM)
            acc = ext_ref[:, cols]
            step = 1
            while step < w:
                acc = acc + _roll_rows(acc, step)
                step *= 2
            inv_count = 1.0 / jnp.minimum(pos, float(w))
            pooled = acc[HALO:, :] * inv_count - u[:, cols]
            mixed.append(_bdot(pooled, poolw_ref[gi]))
        ext_ref[0:HALO, :] = u[ts - HALO:, :]
        pooled = jnp.concatenate(mixed, axis=-1) * pscale_ref[...]
        branch_a = _bdot(pooled * _silu(proj(SEC_PG)), wpo_ref[...])

        isub = lax.broadcasted_iota(jnp.int32, (SUBLANES, HEAD_DIM), 0)
        cat = functools.partial(jnp.concatenate, axis=0)

        def tile_rows(ci):
            return [pl.ds(ci * CHUNK + 32 * (t // TILES_PER_GROUP) + t % TILES_PER_GROUP, SUBLANES,
                          stride=ROW_STRIDE) for t in range(TILES)]

        def scan_chunk(ci):
            rows_of = tile_rows(ci)
            for hd in range(HEADS):
                q, f, v = ([ref[hd, rows, :] for rows in rows_of] for ref in (q_ref, f_ref, v_ref))
                lhs, rhs, total, local = _scan_operands(q, f, v, isub)
                for i in range(N_OPS):
                    lhs_ref[ci, hd, i] = cat(lhs[i]).astype(jnp.bfloat16)
                    rhs_ref[ci, hd, i] = cat(rhs[i]).astype(jnp.bfloat16)
                vb_ref[ci, hd] = cat(v).astype(jnp.bfloat16)
                tot_ref[ci, hd] = _last_row(total)
                loc_ref[ci, hd] = cat(local)

        r2 = _token_of_row(lax.broadcasted_iota(jnp.int32, (CHUNK, CHUNK), 0))
        c2 = _token_of_row(lax.broadcasted_iota(jnp.int32, (CHUNK, CHUNK), 1))
        blk_eq = [(r2 >> (bit + 1)) == (c2 >> (bit + 1)) for bit in range(TIME_BITS)]
        hng = hng_ref[...]

        def output_chunk(ci):
            rows_of = tile_rows(ci)
            heads = range(HEADS)

            def nt(hd, i):
                return lax.dot_general(lhs_ref[ci, hd, i], rhs_ref[ci, hd, i], NT,
                                       preferred_element_type=jnp.float32)

            scores = [None] * HEADS
            for bit in range(LOCAL_BITS, TIME_BITS):
                for hd in heads:
                    p = nt(hd, bit - LOCAL_BITS)
                    if bit < TIME_BITS - 1:
                        p = jnp.where(blk_eq[bit], p, 0.0)
                    scores[hd] = p if scores[hd] is None else scores[hd] + p
            state_t = [state_ref[hd] for hd in heads]
            o_inter = [lax.dot_general(lhs_ref[ci, hd, N_OPS - 1], state_t[hd].astype(jnp.bfloat16), NT,
                                       preferred_element_type=jnp.float32) for hd in heads]
            for hd in heads:
                state_ref[hd] = (state_t[hd] * tot_ref[ci, hd][0:1, :]
                                 + lax.dot_general(vb_ref[ci, hd], rhs_ref[ci, hd, N_OPS - 1], TN,
                                                   preferred_element_type=jnp.float32))
            for hd in heads:
                o = (o_inter[hd] + loc_ref[ci, hd]
                     + jnp.dot(scores[hd].astype(jnp.bfloat16), vb_ref[ci, hd],
                               preferred_element_type=jnp.float32))
                for t, rows in enumerate(rows_of):
                    o_tile = o[SUBLANES * t:SUBLANES * (t + 1), :]
                    ob_ref[hd, rows, :] = (_rms(o_tile) * hng) * g_ref[hd, rows, :]

        for ci in range(ts // CHUNK):
            scan_chunk(ci)
        for ci in range(ts // CHUNK):
            output_chunk(ci)
        gated = jnp.concatenate([ob_ref[hd] for hd in range(HEADS)], axis=-1)
        branch_b = _bdot(gated, who_ref[...])

        merged = _sigmoid(proj(SEC_MGP)) * branch_a + _sigmoid(proj(SEC_MGH)) * branch_b
        y = _bdot(merged, wout_ref[...])
        o_ref[pl.ds(r0, SEQ_TILE), :] = x + gate * (_rms(y) * gpost_ref[...])
        return carry

    lax.fori_loop(0, GRID_TILE // SEQ_TILE, tile_pass, 0)


def _resident(shape, index_map):
    return pl.BlockSpec(shape, index_map, pipeline_mode=pl.Buffered(1))


def _layer_call(layer, x, ada, g_pre, g_post, lb_logits, pool_scale, hgrn_norm_g,
                w_in, pool_w, w_pool_o, w_hgrn_o, w_out):
    batch, seq, d = x.shape
    depth = lb_logits.shape[0]
    ts = SEQ_TILE
    assert seq % GRID_TILE == 0 and GRID_TILE % ts == 0 and ts % CHUNK == 0 and d == D_MODEL
    chunks = ts // CHUNK

    def const2(b, s):
        return (0, 0)

    def per_layer3(b, s):
        return (layer, 0, 0)

    in_specs = [
        pl.BlockSpec((None, GRID_TILE, d), lambda b, s: (b, s, 0)),
        pl.BlockSpec((None, 3, d), lambda b, s: (b, 0, 0)),
        _resident((1, d), const2),
        _resident((1, d), const2),
        _resident((depth, HGRN_WIDTH), const2),
        _resident((1, POOL_WIDTH), const2),
        _resident((1, HEAD_DIM), const2),
        _resident((None, d, sum(IN_SIZES)), per_layer3),
        _resident((None, POOL_GROUPS, POOL_GROUP_DIM, POOL_GROUP_DIM), lambda b, s: (layer, 0, 0, 0)),
        _resident((None, POOL_WIDTH, d), per_layer3),
        _resident((None, HGRN_WIDTH, d), per_layer3),
        _resident((None, d, d), per_layer3),
    ]
    scratch = [
        pltpu.VMEM((HEADS, ts, HEAD_DIM), jnp.float32),
        pltpu.VMEM((HEADS, ts, HEAD_DIM), jnp.float32),
        pltpu.VMEM((HEADS, ts, HEAD_DIM), jnp.float32),
        pltpu.VMEM((HEADS, ts, HEAD_DIM), jnp.float32),
        pltpu.VMEM((HEADS, ts, HEAD_DIM), jnp.float32),
        pltpu.VMEM((HALO + ts, POOL_WIDTH), jnp.float32),
        pltpu.VMEM((HEADS, HEAD_DIM, HEAD_DIM), jnp.float32),
        pltpu.VMEM((chunks, HEADS, N_OPS, CHUNK, HEAD_DIM), jnp.bfloat16),
        pltpu.VMEM((chunks, HEADS, N_OPS, CHUNK, HEAD_DIM), jnp.bfloat16),
        pltpu.VMEM((chunks, HEADS, CHUNK, HEAD_DIM), jnp.bfloat16),
        pltpu.VMEM((chunks, HEADS, SUBLANES, HEAD_DIM), jnp.float32),
        pltpu.VMEM((chunks, HEADS, CHUNK, HEAD_DIM), jnp.float32),
    ]
    return pl.pallas_call(
        functools.partial(_layer_kernel, layer=layer),
        grid=(batch, seq // GRID_TILE),
        in_specs=in_specs,
        out_specs=pl.BlockSpec((None, GRID_TILE, d), lambda b, s: (b, s, 0)),
        out_shape=jax.ShapeDtypeStruct(x.shape, x.dtype),
        scratch_shapes=scratch,
        compiler_params=pltpu.CompilerParams(
            dimension_semantics=("arbitrary", "arbitrary"),
            vmem_limit_bytes=VMEM_LIMIT_BYTES),
        name=f"hybrid_layer{layer}",
    )(x, ada, g_pre[layer][None, :], g_post[layer][None, :], lb_logits,
      pool_scale[layer][None, :], hgrn_norm_g[layer][None, :],
      w_in, pool_w, w_pool_o, w_hgrn_o, w_out)


def _ada_call(c, w_ada, b_ada):
    depth, d, width = w_ada.shape
    batch = c.shape[0]
    assert width % ADA_TILE == 0
    return pl.pallas_call(
        _ada_kernel,
        grid=(depth, width // ADA_TILE),
        in_specs=[
            pl.BlockSpec((batch, d), lambda l, n: (0, 0)),
            pl.BlockSpec((None, d, ADA_TILE), lambda l, n: (l, 0, n)),
            pl.BlockSpec((None, 1, ADA_TILE), lambda l, n: (l, 0, n)),
        ],
        out_specs=pl.BlockSpec((None, batch, ADA_TILE), lambda l, n: (l, 0, n)),
        out_shape=jax.ShapeDtypeStruct((depth, batch, width), jnp.float32),
        name="adaln_vectors",
    )(c, w_ada, b_ada[:, None, :])


def kernel(x, c, w_ada, b_ada, g_pre, g_post, w_in, pool_w, pool_scale, lb_logits,
           hgrn_norm_g, w_pool_o, w_hgrn_o, w_out):
    depth = w_in.shape[0]
    batch, _, d = x.shape
    ada = _ada_call(c, w_ada, b_ada).reshape(depth, batch, 3, d)
    bf = jnp.bfloat16
    w_in_b, pool_w_b = w_in.astype(bf), pool_w.astype(bf)
    w_pool_o_b, w_hgrn_o_b, w_out_b = w_pool_o.astype(bf), w_hgrn_o.astype(bf), w_out.astype(bf)
    for layer in range(depth):
        x = _layer_call(layer, x, ada[layer], g_pre, g_post, lb_logits, pool_scale, hgrn_norm_g,
                        w_in_b, pool_w_b, w_pool_o_b, w_hgrn_o_b, w_out_b)
    return x
```

```python
import functools
import math

import jax
import jax.numpy as jnp
from jax import lax
from jax.experimental import pallas as pl
from jax.experimental.pallas import tpu as pltpu

D_MODEL = 1024
POOL_WINDOWS = (2, 4, 8, 16)
POOL_GROUPS = len(POOL_WINDOWS)
POOL_GROUP_DIM = D_MODEL // 8
POOL_WIDTH = POOL_GROUPS * POOL_GROUP_DIM
HEAD_DIM = 128
HEADS = D_MODEL // HEAD_DIM
HGRN_WIDTH = HEADS * HEAD_DIM
NORM_EPS = 1e-6
F_FLOOR = 1e-30
NEG_LOG2_E = -1.4426950408889634
IN_SIZES = (POOL_WIDTH, POOL_WIDTH, HGRN_WIDTH, HGRN_WIDTH, HGRN_WIDTH, HGRN_WIDTH, D_MODEL, D_MODEL)
IN_OFFSETS = tuple(sum(IN_SIZES[:i]) for i in range(len(IN_SIZES)))
(SEC_PV, SEC_PG, SEC_HQ, SEC_HF, SEC_HI, SEC_HG, SEC_MGP, SEC_MGH) = range(len(IN_SIZES))

GRID_TILE = 512
SEQ_TILE = 256
CHUNK = 64
HALO = max(POOL_WINDOWS)
SUBLANES = 8
TILES = CHUNK // SUBLANES
TILE_BITS = (0, 1, 5)
SUBLANE_BITS = (2, 3, 4)
TIME_BITS = 6
TILES_PER_GROUP = 4
ROW_STRIDE = TILES_PER_GROUP
GROUP_TOKENS = TILES_PER_GROUP * SUBLANES
assert CHUNK == 1 << TIME_BITS
LOCAL_BITS = 2
assert TILE_BITS[:LOCAL_BITS] == tuple(range(LOCAL_BITS)) and TILES_PER_GROUP == 1 << LOCAL_BITS
N_OPS = TIME_BITS - LOCAL_BITS + 1
ADA_TILE = 512
V7X_VMEM_BYTES = 64 * 1024 * 1024
COMPILER_TEMP_BYTES = 3 * SEQ_TILE * D_MODEL * 4 * 4

NT = (((1,), (1,)), ((), ()))
TN = (((0,), (0,)), ((), ()))


def _bdot(a, b):
    return jnp.dot(a.astype(jnp.bfloat16), b.astype(jnp.bfloat16),
                   preferred_element_type=jnp.float32)


def _sigmoid(z):
    return 1.0 / (1.0 + jnp.exp2(z * NEG_LOG2_E))


def _silu(z):
    return z * _sigmoid(z)


def _rms(x):
    return x * lax.rsqrt(jnp.mean(x * x, axis=-1, keepdims=True) + NORM_EPS)


def _roll_rows(a, shift):
    return pltpu.roll(a, shift % a.shape[0], axis=0)


def _token_of_row(p):
    group, j, i = p // GROUP_TOKENS, (p // SUBLANES) % TILES_PER_GROUP, p % SUBLANES
    return group * GROUP_TOKENS + i * ROW_STRIDE + j


def _ada_kernel(c_ref, w_ref, b_ref, o_ref):
    c = c_ref[...]
    o_ref[...] = jnp.dot(_silu(c), w_ref[...], preferred_element_type=jnp.float32) + b_ref[...]


def _sublane_partner(p, sb, isub):
    k = 1
    while k < sb:
        p = jnp.where((isub & k) != 0, p, _roll_rows(p, -k))
        k *= 2
    if 2 * sb == p.shape[0]:
        return _roll_rows(p, sb)
    return jnp.where((isub & sb) != 0, _roll_rows(p, sb), _roll_rows(p, -sb))


def _last_row(a):
    return jnp.broadcast_to(a[SUBLANES - 1:SUBLANES, :], a.shape)


def _lane_sum(a):
    return jnp.sum(a, axis=-1, keepdims=True)


def _local_outputs(q, pf, kk, v):
    out = []
    for base in range(0, TILES, TILES_PER_GROUP):
        qs, fs, ks, vs = (a[base:base + TILES_PER_GROUP] for a in (q, pf, kk, v))
        acc = [_lane_sum(qs[j] * ks[j]) * vs[j] for j in range(TILES_PER_GROUP)]
        decayed_k = {}
        for j_t in range(1, TILES_PER_GROUP):
            qf = qs[j_t] * fs[j_t]
            for j_s in range(j_t - 1, -1, -1):
                decayed_k[j_t, j_s] = (ks[j_s] if j_s == j_t - 1
                                       else decayed_k[j_t - 1, j_s] * fs[j_t - 1])
                acc[j_t] = acc[j_t] + _lane_sum(qf * decayed_k[j_t, j_s]) * vs[j_s]
        out.extend(acc)
    return out


def _scan_operands(q, f, v, isub):
    tiles = range(TILES)
    kk = [1.0 - f[t] for t in tiles]
    pre = [jnp.maximum(f[t], F_FLOOR) for t in tiles]
    suf = [None] * TILES
    local = _local_outputs(q, pre, kk, v)
    zero = jnp.zeros_like(q[0])
    lhs, rhs = [], []
    for bit in range(TIME_BITS):
        if bit >= LOCAL_BITS:
            ksuf = [kk[t] if suf[t] is None else kk[t] * suf[t] for t in tiles]
        if bit in TILE_BITS:
            tb = 1 << TILE_BITS.index(bit)
            if bit >= LOCAL_BITS:
                lhs.append([q[t] * pre[t] if t & tb else zero for t in tiles])
                rhs.append([zero if t & tb else ksuf[t] for t in tiles])
        else:
            sb = 1 << SUBLANE_BITS.index(bit)
            upper = (isub & sb) != 0
            lhs.append([jnp.where(upper, q[t] * pre[t], 0.0) for t in tiles])
            rhs.append([jnp.where(upper, 0.0, ksuf[t]) for t in tiles])
        new_pre, new_suf = list(pre), list(suf)
        if bit in TILE_BITS:
            totals = {}
            for t in tiles:
                src = (t ^ tb) | (tb - 1)
                if src not in totals:
                    totals[src] = _last_row(pre[src]) if bit > max(SUBLANE_BITS) else pre[src]
                if t & tb:
                    new_pre[t] = pre[t] * totals[src]
                else:
                    new_suf[t] = totals[src] if suf[t] is None else suf[t] * totals[src]
        else:
            for base in range(0, TILES, TILES_PER_GROUP):
                partner = _sublane_partner(pre[base + TILES_PER_GROUP - 1], sb, isub)
                pre_mul = jnp.where(upper, partner, 1.0)
                suf_mul = jnp.where(upper, 1.0, partner)
                for t in range(base, base + TILES_PER_GROUP):
                    new_pre[t] = pre[t] * pre_mul
                    new_suf[t] = suf_mul if suf[t] is None else suf[t] * suf_mul
        pre, suf = new_pre, new_suf
    lhs.append([q[t] * pre[t] for t in tiles])
    rhs.append([kk[t] * suf[t] for t in tiles])
    return lhs, rhs, pre[TILES - 1], local


def _layer_kernel(x_ref, ada_ref, gpre_ref, gpost_ref, lbl_ref, pscale_ref, hng_ref,
                  win_ref, poolw_ref, wpo_ref, who_ref, wout_ref, o_ref,
                  q_ref, f_ref, v_ref, g_ref, ob_ref, ext_ref, state_ref,
                  lhs_ref, rhs_ref, vb_ref, tot_ref, loc_ref, *, layer):
    ts = SEQ_TILE
    si = pl.program_id(1)

    @pl.when(si == 0)
    def _():
        state_ref[...] = jnp.zeros_like(state_ref)
        ext_ref[0:HALO, :] = jnp.zeros((HALO, POOL_WIDTH), jnp.float32)

    def tile_pass(part, carry):
        r0 = pl.multiple_of(part * SEQ_TILE, SEQ_TILE)
        x = x_ref[pl.ds(r0, SEQ_TILE), :]
        shift = ada_ref[0:1, :]
        scale = ada_ref[1:2, :]
        gate = ada_ref[2:3, :]
        hb = ((_rms(x) * gpre_ref[...]) * (1.0 + scale) + shift).astype(jnp.bfloat16)

        def proj(section):
            cols = slice(IN_OFFSETS[section], IN_OFFSETS[section] + IN_SIZES[section])
            return jnp.dot(hb, win_ref[:, cols], preferred_element_type=jnp.float32)

        logits = lbl_ref[...]
        e = jnp.exp(logits - jnp.max(logits, axis=0, keepdims=True))
        lb = jnp.zeros((1, HGRN_WIDTH), jnp.float32)
        for j in range(1, layer + 1):
            lb = lb + e[j:j + 1, :]
        lb = jnp.clip(lb / jnp.sum(e, axis=0, keepdims=True), 0.0, 1.0)

        def to_heads(ref, a):
            for hd in range(HEADS):
                ref[hd] = a[:, hd * HEAD_DIM:(hd + 1) * HEAD_DIM]

        to_heads(q_ref, _silu(proj(SEC_HQ)))
        to_heads(f_ref, lb + (1.0 - lb) * _sigmoid(proj(SEC_HF)))
        to_heads(v_ref, proj(SEC_HI))
        to_heads(g_ref, _silu(proj(SEC_HG)))

        u = proj(SEC_PV)
        ext_ref[HALO:HALO + ts, :] = u
        first = si * GRID_TILE + r0 + 1
        pos = (first + lax.broadcasted_iota(jnp.int32, (ts, 1), 0)).astype(jnp.float32)
        mixed = []
        for gi, w in enumerate(POOL_WINDOWS):
            cols = slice(gi * POOL_GROUP_DIM, (gi + 1) * POOL_GROUP_DIM)
            acc = ext_ref[:, cols]
            step = 1
            while step < w:
                acc = acc + _roll_rows(acc, step)
                step *= 2
            inv_count = 1.0 / jnp.minimum(pos, float(w))
            pooled = acc[HALO:, :] * inv_count - u[:, cols]
            mixed.append(_bdot(pooled, poolw_ref[gi]))
        ext_ref[0:HALO, :] = u[ts - HALO:, :]
        pooled = jnp.concatenate(mixed, axis=-1) * pscale_ref[...]
        branch_a = _bdot(pooled * _silu(proj(SEC_PG)), wpo_ref[...])

        isub = lax.broadcasted_iota(jnp.int32, (SUBLANES, HEAD_DIM), 0)
        cat = functools.partial(jnp.concatenate, axis=0)

        def tile_rows(ci):
            return [pl.ds(ci * CHUNK + GROUP_TOKENS * (t // TILES_PER_GROUP) + t % TILES_PER_GROUP, SUBLANES,
                          stride=ROW_STRIDE) for t in range(TILES)]

        def scan_chunk(ci):
            rows_of = tile_rows(ci)
            for hd in range(HEADS):
                q, f, v = ([ref[hd, rows, :] for rows in rows_of] for ref in (q_ref, f_ref, v_ref))
                lhs, rhs, total, local = _scan_operands(q, f, v, isub)
                for i in range(N_OPS):
                    lhs_ref[ci, hd, i] = cat(lhs[i]).astype(jnp.bfloat16)
                    rhs_ref[ci, hd, i] = cat(rhs[i]).astype(jnp.bfloat16)
                vb_ref[ci, hd] = cat(v).astype(jnp.bfloat16)
                decay_row = total[SUBLANES - 1:SUBLANES, :]
                tot_ref[ci, hd] = jnp.broadcast_to(decay_row, (HEAD_DIM, HEAD_DIM)).T
                loc_ref[ci, hd] = cat(local)

        r2 = _token_of_row(lax.broadcasted_iota(jnp.int32, (CHUNK, CHUNK), 0))
        c2 = _token_of_row(lax.broadcasted_iota(jnp.int32, (CHUNK, CHUNK), 1))
        blk_eq = [(r2 >> (bit + 1)) == (c2 >> (bit + 1)) for bit in range(TIME_BITS)]
        hng = hng_ref[...]

        def output_chunk(ci):
            rows_of = tile_rows(ci)
            heads = range(HEADS)

            def nt(hd, i):
                return lax.dot_general(lhs_ref[ci, hd, i], rhs_ref[ci, hd, i], NT,
                                       preferred_element_type=jnp.float32)

            scores = [None] * HEADS
            for bit in range(LOCAL_BITS, TIME_BITS):
                for hd in heads:
                    p = nt(hd, bit - LOCAL_BITS)
                    if bit < TIME_BITS - 1:
                        p = jnp.where(blk_eq[bit], p, 0.0)
                    scores[hd] = p if scores[hd] is None else scores[hd] + p
            state = [state_ref[hd] for hd in heads]
            outs = []
            for hd in heads:
                q_and_scores = jnp.concatenate(
                    [lhs_ref[ci, hd, N_OPS - 1], scores[hd].astype(jnp.bfloat16)], axis=1)
                state_and_v = jnp.concatenate([state[hd].astype(jnp.bfloat16), vb_ref[ci, hd]], axis=0)
                outs.append(jnp.dot(q_and_scores, state_and_v, preferred_element_type=jnp.float32))
            for hd in heads:
                state_ref[hd] = (state[hd] * tot_ref[ci, hd]
                                 + lax.dot_general(rhs_ref[ci, hd, N_OPS - 1], vb_ref[ci, hd], TN,
                                                   preferred_element_type=jnp.float32))
            for hd in heads:
                o = outs[hd] + loc_ref[ci, hd]
                for t, rows in enumerate(rows_of):
                    o_tile = o[SUBLANES * t:SUBLANES * (t + 1), :]
                    ob_ref[hd, rows, :] = (_rms(o_tile) * hng) * g_ref[hd, rows, :]

        for ci in range(ts // CHUNK):
            scan_chunk(ci)
        for ci in range(ts // CHUNK):
            output_chunk(ci)
        gated = jnp.concatenate([ob_ref[hd] for hd in range(HEADS)], axis=-1)
        branch_b = _bdot(gated, who_ref[...])

        merged = _sigmoid(proj(SEC_MGP)) * branch_a + _sigmoid(proj(SEC_MGH)) * branch_b
        y = _bdot(merged, wout_ref[...])
        o_ref[pl.ds(r0, SEQ_TILE), :] = x + gate * (_rms(y) * gpost_ref[...])
        return carry

    lax.fori_loop(0, GRID_TILE // SEQ_TILE, tile_pass, 0)


def _resident(shape, index_map):
    return pl.BlockSpec(shape, index_map, pipeline_mode=pl.Buffered(1))


def _nbytes(shape, dtype):
    return math.prod(shape) * jnp.dtype(dtype).itemsize


def _layer_call(layer, x, ada, g_pre, g_post, lb_logits, pool_scale, hgrn_norm_g,
                w_in, pool_w, w_pool_o, w_hgrn_o, w_out):
    batch, seq, d = x.shape
    depth = lb_logits.shape[0]
    ts = SEQ_TILE
    assert seq % GRID_TILE == 0 and GRID_TILE % ts == 0 and ts % CHUNK == 0 and d == D_MODEL
    chunks = ts // CHUNK

    def const2(b, s):
        return (0, 0)

    def per_layer3(b, s):
        return (layer, 0, 0)

    in_specs = [
        pl.BlockSpec((None, GRID_TILE, d), lambda b, s: (b, s, 0)),
        pl.BlockSpec((None, 3, d), lambda b, s: (b, 0, 0)),
        _resident((1, d), const2),
        _resident((1, d), const2),
        _resident((depth, HGRN_WIDTH), const2),
        _resident((1, POOL_WIDTH), const2),
        _resident((1, HEAD_DIM), const2),
        _resident((None, d, sum(IN_SIZES)), per_layer3),
        _resident((None, POOL_GROUPS, POOL_GROUP_DIM, POOL_GROUP_DIM), lambda b, s: (layer, 0, 0, 0)),
        _resident((None, POOL_WIDTH, d), per_layer3),
        _resident((None, HGRN_WIDTH, d), per_layer3),
        _resident((None, d, d), per_layer3),
    ]
    scratch = [
        pltpu.VMEM((HEADS, ts, HEAD_DIM), jnp.float32),
        pltpu.VMEM((HEADS, ts, HEAD_DIM), jnp.float32),
        pltpu.VMEM((HEADS, ts, HEAD_DIM), jnp.float32),
        pltpu.VMEM((HEADS, ts, HEAD_DIM), jnp.float32),
        pltpu.VMEM((HEADS, ts, HEAD_DIM), jnp.float32),
        pltpu.VMEM((HALO + ts, POOL_WIDTH), jnp.float32),
        pltpu.VMEM((HEADS, HEAD_DIM, HEAD_DIM), jnp.float32),
        pltpu.VMEM((chunks, HEADS, N_OPS, CHUNK, HEAD_DIM), jnp.bfloat16),
        pltpu.VMEM((chunks, HEADS, N_OPS, CHUNK, HEAD_DIM), jnp.bfloat16),
        pltpu.VMEM((chunks, HEADS, CHUNK, HEAD_DIM), jnp.bfloat16),
        pltpu.VMEM((chunks, HEADS, HEAD_DIM, HEAD_DIM), jnp.float32),
        pltpu.VMEM((chunks, HEADS, CHUNK, HEAD_DIM), jnp.float32),
    ]
    weights = (w_in, pool_w, w_pool_o, w_hgrn_o, w_out)
    vmem_bytes = (sum(_nbytes(w.shape[1:], w.dtype) for w in weights)
                  + 2 * 2 * _nbytes((GRID_TILE, d), x.dtype)
                  + sum(_nbytes(s.shape, s.dtype) for s in scratch)
                  + COMPILER_TEMP_BYTES)
    assert vmem_bytes <= V7X_VMEM_BYTES, vmem_bytes
    return pl.pallas_call(
        functools.partial(_layer_kernel, layer=layer),
        grid=(batch, seq // GRID_TILE),
        in_specs=in_specs,
        out_specs=pl.BlockSpec((None, GRID_TILE, d), lambda b, s: (b, s, 0)),
        out_shape=jax.ShapeDtypeStruct(x.shape, x.dtype),
        scratch_shapes=scratch,
        compiler_params=pltpu.CompilerParams(
            dimension_semantics=("arbitrary", "arbitrary"),
            vmem_limit_bytes=vmem_bytes),
        name=f"hybrid_layer{layer}",
    )(x, ada, g_pre[layer][None, :], g_post[layer][None, :], lb_logits,
      pool_scale[layer][None, :], hgrn_norm_g[layer][None, :],
      w_in, pool_w, w_pool_o, w_hgrn_o, w_out)


def _ada_call(c, w_ada, b_ada):
    depth, d, width = w_ada.shape
    batch = c.shape[0]
    assert width % ADA_TILE == 0
    return pl.pallas_call(
        _ada_kernel,
        grid=(depth, width // ADA_TILE),
        in_specs=[
            pl.BlockSpec((batch, d), lambda l, n: (0, 0)),
            pl.BlockSpec((None, d, ADA_TILE), lambda l, n: (l, 0, n)),
            pl.BlockSpec((None, 1, ADA_TILE), lambda l, n: (l, 0, n)),
        ],
        out_specs=pl.BlockSpec((None, batch, ADA_TILE), lambda l, n: (l, 0, n)),
        out_shape=jax.ShapeDtypeStruct((depth, batch, width), jnp.float32),
        name="adaln_vectors",
    )(c, w_ada, b_ada[:, None, :])


def kernel(x, c, w_ada, b_ada, g_pre, g_post, w_in, pool_w, pool_scale, lb_logits,
           hgrn_norm_g, w_pool_o, w_hgrn_o, w_out):
    depth = w_in.shape[0]
    batch, _, d = x.shape
    ada = _ada_call(c, w_ada, b_ada).reshape(depth, batch, 3, d)
    bf = jnp.bfloat16
    w_in_b, pool_w_b = w_in.astype(bf), pool_w.astype(bf)
    w_pool_o_b, w_hgrn_o_b, w_out_b = w_pool_o.astype(bf), w_hgrn_o.astype(bf), w_out.astype(bf)
    for layer in range(depth):
        x = _layer_call(layer, x, ada[layer], g_pre, g_post, lb_logits, pool_scale, hgrn_norm_g,
                        w_in_b, pool_w_b, w_pool_o_b, w_hgrn_o_b, w_out_b)
    return x
```

```python
import functools
import math

import jax
import jax.numpy as jnp
from jax import lax
from jax.experimental import pallas as pl
from jax.experimental.pallas import tpu as pltpu

D_MODEL = 1024
POOL_WINDOWS = (2, 4, 8, 16)
POOL_GROUPS = len(POOL_WINDOWS)
POOL_GROUP_DIM = D_MODEL // 8
POOL_WIDTH = POOL_GROUPS * POOL_GROUP_DIM
HEAD_DIM = 128
HEADS = D_MODEL // HEAD_DIM
HGRN_WIDTH = HEADS * HEAD_DIM
NORM_EPS = 1e-6
F_FLOOR = 1e-30
NEG_LOG2_E = -1.4426950408889634
IN_SIZES = (POOL_WIDTH, POOL_WIDTH, HGRN_WIDTH, HGRN_WIDTH, HGRN_WIDTH, HGRN_WIDTH, D_MODEL, D_MODEL)
IN_OFFSETS = tuple(sum(IN_SIZES[:i]) for i in range(len(IN_SIZES)))
(SEC_PV, SEC_PG, SEC_HQ, SEC_HF, SEC_HI, SEC_HG, SEC_MGP, SEC_MGH) = range(len(IN_SIZES))

GRID_TILE = 512
SEQ_TILE = 256
CHUNK = 64
HALO = max(POOL_WINDOWS)
SUBLANES = 8
TILES = CHUNK // SUBLANES
TILE_BITS = (0, 1, 5)
SUBLANE_BITS = (2, 3, 4)
TIME_BITS = 6
TILES_PER_GROUP = 4
ROW_STRIDE = TILES_PER_GROUP
GROUP_TOKENS = TILES_PER_GROUP * SUBLANES
assert CHUNK == 1 << TIME_BITS
LOCAL_BITS = 2
assert TILE_BITS[:LOCAL_BITS] == tuple(range(LOCAL_BITS)) and TILES_PER_GROUP == 1 << LOCAL_BITS
N_OPS = TIME_BITS - LOCAL_BITS + 1
ADA_TILE = 512
V7X_VMEM_BYTES = 64 * 1024 * 1024
COMPILER_TEMP_BYTES = 3 * SEQ_TILE * D_MODEL * 4 * 4

NT = (((1,), (1,)), ((), ()))
TN = (((0,), (0,)), ((), ()))


def _bdot(a, b):
    return jnp.dot(a.astype(jnp.bfloat16), b.astype(jnp.bfloat16),
                   preferred_element_type=jnp.float32)


def _sigmoid(z):
    return 1.0 / (1.0 + jnp.exp2(z * NEG_LOG2_E))


def _silu(z):
    return z * _sigmoid(z)


def _rms(x):
    return x * lax.rsqrt(jnp.mean(x * x, axis=-1, keepdims=True) + NORM_EPS)


def _roll_rows(a, shift):
    return pltpu.roll(a, shift % a.shape[0], axis=0)


def _token_of_row(p):
    group, j, i = p // GROUP_TOKENS, (p // SUBLANES) % TILES_PER_GROUP, p % SUBLANES
    return group * GROUP_TOKENS + i * ROW_STRIDE + j


def _ada_kernel(c_ref, w_ref, b_ref, o_ref):
    c = c_ref[...]
    o_ref[...] = jnp.dot(_silu(c), w_ref[...], preferred_element_type=jnp.float32) + b_ref[...]


def _sublane_partner(p, sb, isub):
    k = 1
    while k < sb:
        p = jnp.where((isub & k) != 0, p, _roll_rows(p, -k))
        k *= 2
    if 2 * sb == p.shape[0]:
        return _roll_rows(p, sb)
    return jnp.where((isub & sb) != 0, _roll_rows(p, sb), _roll_rows(p, -sb))


def _last_row(a):
    return jnp.broadcast_to(a[SUBLANES - 1:SUBLANES, :], a.shape)


def _lane_sum(a):
    return jnp.sum(a, axis=-1, keepdims=True)


def _local_outputs(q, pf, kk, v):
    out = []
    for base in range(0, TILES, TILES_PER_GROUP):
        qs, fs, ks, vs = (a[base:base + TILES_PER_GROUP] for a in (q, pf, kk, v))
        acc = [_lane_sum(qs[j] * ks[j]) * vs[j] for j in range(TILES_PER_GROUP)]
        decayed_k = {}
        for j_t in range(1, TILES_PER_GROUP):
            qf = qs[j_t] * fs[j_t]
            for j_s in range(j_t - 1, -1, -1):
                decayed_k[j_t, j_s] = (ks[j_s] if j_s == j_t - 1
                                       else decayed_k[j_t - 1, j_s] * fs[j_t - 1])
                acc[j_t] = acc[j_t] + _lane_sum(qf * decayed_k[j_t, j_s]) * vs[j_s]
        out.extend(acc)
    return out


def _scan_operands(q, f, v, isub):
    tiles = range(TILES)
    kk = [1.0 - f[t] for t in tiles]
    pre = [jnp.maximum(f[t], F_FLOOR) for t in tiles]
    suf = [None] * TILES
    local = _local_outputs(q, pre, kk, v)
    zero = jnp.zeros_like(q[0])
    lhs, rhs = [], []
    for bit in range(TIME_BITS):
        if bit >= LOCAL_BITS:
            ksuf = [kk[t] if suf[t] is None else kk[t] * suf[t] for t in tiles]
        if bit in TILE_BITS:
            tb = 1 << TILE_BITS.index(bit)
            if bit >= LOCAL_BITS:
                lhs.append([q[t] * pre[t] if t & tb else zero for t in tiles])
                rhs.append([zero if t & tb else ksuf[t] for t in tiles])
        else:
            sb = 1 << SUBLANE_BITS.index(bit)
            upper = (isub & sb) != 0
            lhs.append([jnp.where(upper, q[t] * pre[t], 0.0) for t in tiles])
            rhs.append([jnp.where(upper, 0.0, ksuf[t]) for t in tiles])
        new_pre, new_suf = list(pre), list(suf)
        if bit in TILE_BITS:
            totals = {}
            for t in tiles:
                src = (t ^ tb) | (tb - 1)
                if src not in totals:
                    totals[src] = _last_row(pre[src]) if bit > max(SUBLANE_BITS) else pre[src]
                if t & tb:
                    new_pre[t] = pre[t] * totals[src]
                else:
                    new_suf[t] = totals[src] if suf[t] is None else suf[t] * totals[src]
        else:
            for base in range(0, TILES, TILES_PER_GROUP):
                partner = _sublane_partner(pre[base + TILES_PER_GROUP - 1], sb, isub)
                pre_mul = jnp.where(upper, partner, 1.0)
                suf_mul = jnp.where(upper, 1.0, partner)
                for t in range(base, base + TILES_PER_GROUP):
                    new_pre[t] = pre[t] * pre_mul
                    new_suf[t] = suf_mul if suf[t] is None else suf[t] * suf_mul
        pre, suf = new_pre, new_suf
    lhs.append([q[t] * pre[t] for t in tiles])
    rhs.append([kk[t] * suf[t] for t in tiles])
    return lhs, rhs, pre[TILES - 1], local


def _layer_kernel(x_ref, ada_ref, gpre_ref, gpost_ref, lbl_ref, pscale_ref, hng_ref,
                  win_ref, poolw_ref, wpo_ref, who_ref, wout_ref, o_ref,
                  q_ref, f_ref, v_ref, g_ref, ob_ref, ext_ref, state_ref,
                  lhs_ref, rhs_ref, vb_ref, tot_ref, loc_ref, *, layer):
    ts = SEQ_TILE
    si = pl.program_id(1)

    @pl.when(si == 0)
    def _():
        state_ref[...] = jnp.zeros_like(state_ref)
        ext_ref[0:HALO, :] = jnp.zeros((HALO, POOL_WIDTH), jnp.float32)

    def tile_pass(part, carry):
        r0 = pl.multiple_of(part * SEQ_TILE, SEQ_TILE)
        x = x_ref[pl.ds(r0, SEQ_TILE), :]
        shift = ada_ref[0:1, :]
        scale = ada_ref[1:2, :]
        gate = ada_ref[2:3, :]
        hb = ((_rms(x) * gpre_ref[...]) * (1.0 + scale) + shift).astype(jnp.bfloat16)

        def proj(section):
            cols = slice(IN_OFFSETS[section], IN_OFFSETS[section] + IN_SIZES[section])
            return jnp.dot(hb, win_ref[:, cols], preferred_element_type=jnp.float32)

        logits = lbl_ref[...]
        e = jnp.exp(logits - jnp.max(logits, axis=0, keepdims=True))
        lb = jnp.zeros((1, HGRN_WIDTH), jnp.float32)
        for j in range(1, layer + 1):
            lb = lb + e[j:j + 1, :]
        lb = jnp.clip(lb / jnp.sum(e, axis=0, keepdims=True), 0.0, 1.0)

        def to_heads(ref, a):
            for hd in range(HEADS):
                ref[hd] = a[:, hd * HEAD_DIM:(hd + 1) * HEAD_DIM]

        to_heads(q_ref, _silu(proj(SEC_HQ)))
        to_heads(f_ref, lb + (1.0 - lb) * _sigmoid(proj(SEC_HF)))
        to_heads(v_ref, proj(SEC_HI))
        to_heads(g_ref, _silu(proj(SEC_HG)))

        u = proj(SEC_PV)
        ext_ref[HALO:HALO + ts, :] = u
        first = si * GRID_TILE + r0 + 1
        pos = (first + lax.broadcasted_iota(jnp.int32, (ts, 1), 0)).astype(jnp.float32)
        mixed = []
        for gi, w in enumerate(POOL_WINDOWS):
            cols = slice(gi * POOL_GROUP_DIM, (gi + 1) * POOL_GROUP_DIM)
            acc = ext_ref[:, cols]
            step = 1
            while step < w:
                acc = acc + _roll_rows(acc, step)
                step *= 2
            inv_count = 1.0 / jnp.minimum(pos, float(w))
            pooled = acc[HALO:, :] * inv_count - u[:, cols]
            mixed.append(_bdot(pooled, poolw_ref[gi]))
        ext_ref[0:HALO, :] = u[ts - HALO:, :]
        pooled = jnp.concatenate(mixed, axis=-1) * pscale_ref[...]
        branch_a = _bdot(pooled * _silu(proj(SEC_PG)), wpo_ref[...])

        isub = lax.broadcasted_iota(jnp.int32, (SUBLANES, HEAD_DIM), 0)
        cat = functools.partial(jnp.concatenate, axis=0)

        def tile_rows(ci):
            return [pl.ds(ci * CHUNK + GROUP_TOKENS * (t // TILES_PER_GROUP) + t % TILES_PER_GROUP, SUBLANES,
                          stride=ROW_STRIDE) for t in range(TILES)]

        def scan_chunk(ci):
            rows_of = tile_rows(ci)
            for hd in range(HEADS):
                q, f, v = ([ref[hd, rows, :] for rows in rows_of] for ref in (q_ref, f_ref, v_ref))
                lhs, rhs, total, local = _scan_operands(q, f, v, isub)
                for i in range(N_OPS):
                    lhs_ref[ci, hd, i] = cat(lhs[i]).astype(jnp.bfloat16)
                    rhs_ref[ci, hd, i] = cat(rhs[i]).astype(jnp.bfloat16)
                vb_ref[ci, hd] = cat(v).astype(jnp.bfloat16)
                tot_ref[ci, hd] = _last_row(total)
                loc_ref[ci, hd] = cat(local)

        r2 = _token_of_row(lax.broadcasted_iota(jnp.int32, (CHUNK, CHUNK), 0))
        c2 = _token_of_row(lax.broadcasted_iota(jnp.int32, (CHUNK, CHUNK), 1))
        blk_eq = [(r2 >> (bit + 1)) == (c2 >> (bit + 1)) for bit in range(TIME_BITS)]
        hng = hng_ref[...]

        def output_chunk(ci):
            rows_of = tile_rows(ci)
            heads = range(HEADS)

            def nt(hd, i):
                return lax.dot_general(lhs_ref[ci, hd, i], rhs_ref[ci, hd, i], NT,
                                       preferred_element_type=jnp.float32)

            scores = [None] * HEADS
            for bit in range(LOCAL_BITS, TIME_BITS):
                for hd in heads:
                    p = nt(hd, bit - LOCAL_BITS)
                    if bit < TIME_BITS - 1:
                        p = jnp.where(blk_eq[bit], p, 0.0)
                    scores[hd] = p if scores[hd] is None else scores[hd] + p
            state_t = [state_ref[hd] for hd in heads]
            o_inter = [lax.dot_general(lhs_ref[ci, hd, N_OPS - 1], state_t[hd].astype(jnp.bfloat16), NT,
                                       preferred_element_type=jnp.float32) for hd in heads]
            for hd in heads:
                state_ref[hd] = (state_t[hd] * tot_ref[ci, hd][0:1, :]
                                 + lax.dot_general(vb_ref[ci, hd], rhs_ref[ci, hd, N_OPS - 1], TN,
                                                   preferred_element_type=jnp.float32))
            for hd in heads:
                o = (o_inter[hd] + loc_ref[ci, hd]
                     + jnp.dot(scores[hd].astype(jnp.bfloat16), vb_ref[ci, hd],
                               preferred_element_type=jnp.float32))
                for t, rows in enumerate(rows_of):
                    o_tile = o[SUBLANES * t:SUBLANES * (t + 1), :]
                    ob_ref[hd, rows, :] = (_rms(o_tile) * hng) * g_ref[hd, rows, :]

        for ci in range(ts // CHUNK):
            scan_chunk(ci)
        for ci in range(ts // CHUNK):
            output_chunk(ci)
        gated = jnp.concatenate([ob_ref[hd] for hd in range(HEADS)], axis=-1)
        branch_b = _bdot(gated, who_ref[...])

        merged = _sigmoid(proj(SEC_MGP)) * branch_a + _sigmoid(proj(SEC_MGH)) * branch_b
        y = _bdot(merged, wout_ref[...])
        o_ref[pl.ds(r0, SEQ_TILE), :] = x + gate * (_rms(y) * gpost_ref[...])
        return carry

    lax.fori_loop(0, GRID_TILE // SEQ_TILE, tile_pass, 0)


def _resident(shape, index_map):
    return pl.BlockSpec(shape, index_map, pipeline_mode=pl.Buffered(1))


def _nbytes(shape, dtype):
    return math.prod(shape) * jnp.dtype(dtype).itemsize


def _layer_call(layer, x, ada, g_pre, g_post, lb_logits, pool_scale, hgrn_norm_g,
                w_in, pool_w, w_pool_o, w_hgrn_o, w_out):
    batch, seq, d = x.shape
    depth = lb_logits.shape[0]
    ts = SEQ_TILE
    assert seq % GRID_TILE == 0 and GRID_TILE % ts == 0 and ts % CHUNK == 0 and d == D_MODEL
    chunks = ts // CHUNK

    def const2(b, s):
        return (0, 0)

    def per_layer3(b, s):
        return (layer, 0, 0)

    in_specs = [
        pl.BlockSpec((None, GRID_TILE, d), lambda b, s: (b, s, 0)),
        pl.BlockSpec((None, 3, d), lambda b, s: (b, 0, 0)),
        _resident((1, d), const2),
        _resident((1, d), const2),
        _resident((depth, HGRN_WIDTH), const2),
        _resident((1, POOL_WIDTH), const2),
        _resident((1, HEAD_DIM), const2),
        _resident((None, d, sum(IN_SIZES)), per_layer3),
        _resident((None, POOL_GROUPS, POOL_GROUP_DIM, POOL_GROUP_DIM), lambda b, s: (layer, 0, 0, 0)),
        _resident((None, POOL_WIDTH, d), per_layer3),
        _resident((None, HGRN_WIDTH, d), per_layer3),
        _resident((None, d, d), per_layer3),
    ]
    scratch = [
        pltpu.VMEM((HEADS, ts, HEAD_DIM), jnp.float32),
        pltpu.VMEM((HEADS, ts, HEAD_DIM), jnp.float32),
        pltpu.VMEM((HEADS, ts, HEAD_DIM), jnp.float32),
        pltpu.VMEM((HEADS, ts, HEAD_DIM), jnp.float32),
        pltpu.VMEM((HEADS, ts, HEAD_DIM), jnp.float32),
        pltpu.VMEM((HALO + ts, POOL_WIDTH), jnp.float32),
        pltpu.VMEM((HEADS, HEAD_DIM, HEAD_DIM), jnp.float32),
        pltpu.VMEM((chunks, HEADS, N_OPS, CHUNK, HEAD_DIM), jnp.bfloat16),
        pltpu.VMEM((chunks, HEADS, N_OPS, CHUNK, HEAD_DIM), jnp.bfloat16),
        pltpu.VMEM((chunks, HEADS, CHUNK, HEAD_DIM), jnp.bfloat16),
        pltpu.VMEM((chunks, HEADS, SUBLANES, HEAD_DIM), jnp.float32),
        pltpu.VMEM((chunks, HEADS, CHUNK, HEAD_DIM), jnp.float32),
    ]
    weights = (w_in, pool_w, w_pool_o, w_hgrn_o, w_out)
    vmem_bytes = (sum(_nbytes(w.shape[1:], w.dtype) for w in weights)
                  + 2 * 2 * _nbytes((GRID_TILE, d), x.dtype)
                  + sum(_nbytes(s.shape, s.dtype) for s in scratch)
                  + COMPILER_TEMP_BYTES)
    assert vmem_bytes <= V7X_VMEM_BYTES, vmem_bytes
    return pl.pallas_call(
        functools.partial(_layer_kernel, layer=layer),
        grid=(batch, seq // GRID_TILE),
        in_specs=in_specs,
        out_specs=pl.BlockSpec((None, GRID_TILE, d), lambda b, s: (b, s, 0)),
        out_shape=jax.ShapeDtypeStruct(x.shape, x.dtype),
        scratch_shapes=scratch,
        compiler_params=pltpu.CompilerParams(
            dimension_semantics=("arbitrary", "arbitrary"),
            vmem_limit_bytes=vmem_bytes),
        name=f"hybrid_layer{layer}",
    )(x, ada, g_pre[layer][None, :], g_post[layer][None, :], lb_logits,
      pool_scale[layer][None, :], hgrn_norm_g[layer][None, :],
      w_in, pool_w, w_pool_o, w_hgrn_o, w_out)


def _ada_call(c, w_ada, b_ada):
    depth, d, width = w_ada.shape
    batch = c.shape[0]
    assert width % ADA_TILE == 0
    return pl.pallas_call(
        _ada_kernel,
        grid=(depth, width // ADA_TILE),
        in_specs=[
            pl.BlockSpec((batch, d), lambda l, n: (0, 0)),
            pl.BlockSpec((None, d, ADA_TILE), lambda l, n: (l, 0, n)),
            pl.BlockSpec((None, 1, ADA_TILE), lambda l, n: (l, 0, n)),
        ],
        out_specs=pl.BlockSpec((None, batch, ADA_TILE), lambda l, n: (l, 0, n)),
        out_shape=jax.ShapeDtypeStruct((depth, batch, width), jnp.float32),
        name="adaln_vectors",
    )(c, w_ada, b_ada[:, None, :])


def kernel(x, c, w_ada, b_ada, g_pre, g_post, w_in, pool_w, pool_scale, lb_logits,
           hgrn_norm_g, w_pool_o, w_hgrn_o, w_out):
    depth = w_in.shape[0]
    batch, _, d = x.shape
    ada = _ada_call(c, w_ada, b_ada).reshape(depth, batch, 3, d)
    bf = jnp.bfloat16
    w_in_b, pool_w_b = w_in.astype(bf), pool_w.astype(bf)
    w_pool_o_b, w_hgrn_o_b, w_out_b = w_pool_o.astype(bf), w_hgrn_o.astype(bf), w_out.astype(bf)
    for layer in range(depth):
        x = _layer_call(layer, x, ada[layer], g_pre, g_post, lb_logits, pool_scale, hgrn_norm_g,
                        w_in_b, pool_w_b, w_pool_o_b, w_hgrn_o_b, w_out_b)
    return x
```

```python
import functools
import math

import jax
import jax.numpy as jnp
from jax import lax
from jax.experimental import pallas as pl
from jax.experimental.pallas import tpu as pltpu

D_MODEL = 1024
POOL_WINDOWS = (2, 4, 8, 16)
POOL_GROUPS = len(POOL_WINDOWS)
POOL_GROUP_DIM = D_MODEL // 8
POOL_WIDTH = POOL_GROUPS * POOL_GROUP_DIM
HEAD_DIM = 128
HEADS = D_MODEL // HEAD_DIM
HGRN_WIDTH = HEADS * HEAD_DIM
NORM_EPS = 1e-6
F_FLOOR = 1e-30
NEG_LOG2_E = -1.4426950408889634
IN_SIZES = (POOL_WIDTH, POOL_WIDTH, HGRN_WIDTH, HGRN_WIDTH, HGRN_WIDTH, HGRN_WIDTH, D_MODEL, D_MODEL)
IN_OFFSETS = tuple(sum(IN_SIZES[:i]) for i in range(len(IN_SIZES)))
(SEC_PV, SEC_PG, SEC_HQ, SEC_HF, SEC_HI, SEC_HG, SEC_MGP, SEC_MGH) = range(len(IN_SIZES))

GRID_TILE = 512
SEQ_TILE = 256
CHUNK = 64
HALO = max(POOL_WINDOWS)
SUBLANES = 8
TILES = CHUNK // SUBLANES
TILE_BITS = (0, 1, 5)
SUBLANE_BITS = (2, 3, 4)
TIME_BITS = 6
TILES_PER_GROUP = 4
ROW_STRIDE = TILES_PER_GROUP
GROUP_TOKENS = TILES_PER_GROUP * SUBLANES
assert CHUNK == 1 << TIME_BITS
LOCAL_BITS = 2
assert TILE_BITS[:LOCAL_BITS] == tuple(range(LOCAL_BITS)) and TILES_PER_GROUP == 1 << LOCAL_BITS
N_OPS = TIME_BITS - LOCAL_BITS + 1
ADA_TILE = 1024
V7X_VMEM_BYTES = 64 * 1024 * 1024
COMPILER_TEMP_BYTES = 3 * SEQ_TILE * D_MODEL * 4 * 4

NT = (((1,), (1,)), ((), ()))
TN = (((0,), (0,)), ((), ()))


def _bdot(a, b):
    return jnp.dot(a.astype(jnp.bfloat16), b.astype(jnp.bfloat16),
                   preferred_element_type=jnp.float32)


def _sigmoid(z):
    return 1.0 / (1.0 + jnp.exp2(z * NEG_LOG2_E))


def _silu(z):
    return z * _sigmoid(z)


def _rms(x):
    return x * lax.rsqrt(jnp.mean(x * x, axis=-1, keepdims=True) + NORM_EPS)


def _roll_rows(a, shift):
    return pltpu.roll(a, shift % a.shape[0], axis=0)


def _token_of_row(p):
    group, j, i = p // GROUP_TOKENS, (p // SUBLANES) % TILES_PER_GROUP, p % SUBLANES
    return group * GROUP_TOKENS + i * ROW_STRIDE + j


def _ada_kernel(c_ref, w_ref, b_ref, o_ref):
    c = c_ref[...]
    o_ref[...] = jnp.dot(_silu(c), w_ref[...], preferred_element_type=jnp.float32) + b_ref[...]


def _sublane_partner(p, sb, isub):
    k = 1
    while k < sb:
        p = jnp.where((isub & k) != 0, p, _roll_rows(p, -k))
        k *= 2
    if 2 * sb == p.shape[0]:
        return _roll_rows(p, sb)
    return jnp.where((isub & sb) != 0, _roll_rows(p, sb), _roll_rows(p, -sb))


def _last_row(a):
    return jnp.broadcast_to(a[SUBLANES - 1:SUBLANES, :], a.shape)


def _lane_sum(a):
    return jnp.sum(a, axis=-1, keepdims=True)


def _local_outputs(q, pf, kk, v):
    out = []
    for base in range(0, TILES, TILES_PER_GROUP):
        qs, fs, ks, vs = (a[base:base + TILES_PER_GROUP] for a in (q, pf, kk, v))
        acc = [_lane_sum(qs[j] * ks[j]) * vs[j] for j in range(TILES_PER_GROUP)]
        decayed_k = {}
        for j_t in range(1, TILES_PER_GROUP):
            qf = qs[j_t] * fs[j_t]
            for j_s in range(j_t - 1, -1, -1):
                decayed_k[j_t, j_s] = (ks[j_s] if j_s == j_t - 1
                                       else decayed_k[j_t - 1, j_s] * fs[j_t - 1])
                acc[j_t] = acc[j_t] + _lane_sum(qf * decayed_k[j_t, j_s]) * vs[j_s]
        out.extend(acc)
    return out


def _scan_operands(q, f, v, isub):
    tiles = range(TILES)
    kk = [1.0 - f[t] for t in tiles]
    pre = [jnp.maximum(f[t], F_FLOOR) for t in tiles]
    suf = [None] * TILES
    local = _local_outputs(q, pre, kk, v)
    zero = jnp.zeros_like(q[0])
    lhs, rhs = [], []
    for bit in range(TIME_BITS):
        if bit >= LOCAL_BITS:
            ksuf = [kk[t] if suf[t] is None else kk[t] * suf[t] for t in tiles]
        if bit in TILE_BITS:
            tb = 1 << TILE_BITS.index(bit)
            if bit >= LOCAL_BITS:
                lhs.append([q[t] * pre[t] if t & tb else zero for t in tiles])
                rhs.append([zero if t & tb else ksuf[t] for t in tiles])
        else:
            sb = 1 << SUBLANE_BITS.index(bit)
            upper = (isub & sb) != 0
            lhs.append([jnp.where(upper, q[t] * pre[t], 0.0) for t in tiles])
            rhs.append([jnp.where(upper, 0.0, ksuf[t]) for t in tiles])
        new_pre, new_suf = list(pre), list(suf)
        if bit in TILE_BITS:
            totals = {}
            for t in tiles:
                src = (t ^ tb) | (tb - 1)
                if src not in totals:
                    totals[src] = _last_row(pre[src]) if bit > max(SUBLANE_BITS) else pre[src]
                if t & tb:
                    new_pre[t] = pre[t] * totals[src]
                else:
                    new_suf[t] = totals[src] if suf[t] is None else suf[t] * totals[src]
        else:
            for base in range(0, TILES, TILES_PER_GROUP):
                partner = _sublane_partner(pre[base + TILES_PER_GROUP - 1], sb, isub)
                pre_mul = jnp.where(upper, partner, 1.0)
                suf_mul = jnp.where(upper, 1.0, partner)
                for t in range(base, base + TILES_PER_GROUP):
                    new_pre[t] = pre[t] * pre_mul
                    new_suf[t] = suf_mul if suf[t] is None else suf[t] * suf_mul
        pre, suf = new_pre, new_suf
    lhs.append([q[t] * pre[t] for t in tiles])
    rhs.append([kk[t] * suf[t] for t in tiles])
    return lhs, rhs, pre[TILES - 1], local


def _layer_kernel(x_ref, ada_ref, gpre_ref, gpost_ref, lbl_ref, pscale_ref, hng_ref,
                  win_ref, poolw_ref, wpo_ref, who_ref, wout_ref, o_ref,
                  q_ref, f_ref, v_ref, g_ref, ob_ref, ext_ref, state_ref,
                  lhs_ref, rhs_ref, vb_ref, tot_ref, loc_ref, *, layer):
    ts = SEQ_TILE
    si = pl.program_id(1)

    @pl.when(si == 0)
    def _():
        state_ref[...] = jnp.zeros_like(state_ref)
        ext_ref[0:HALO, :] = jnp.zeros((HALO, POOL_WIDTH), jnp.float32)

    def tile_pass(part, carry):
        r0 = pl.multiple_of(part * SEQ_TILE, SEQ_TILE)
        x = x_ref[pl.ds(r0, SEQ_TILE), :]
        shift = ada_ref[0:1, :]
        scale = ada_ref[1:2, :]
        gate = ada_ref[2:3, :]
        hb = (_rms(x) * (gpre_ref[...] * (1.0 + scale)) + shift).astype(jnp.bfloat16)

        def proj(section):
            cols = slice(IN_OFFSETS[section], IN_OFFSETS[section] + IN_SIZES[section])
            return jnp.dot(hb, win_ref[:, cols], preferred_element_type=jnp.float32)

        logits = lbl_ref[...]
        e = jnp.exp(logits - jnp.max(logits, axis=0, keepdims=True))
        lb = jnp.zeros((1, HGRN_WIDTH), jnp.float32)
        for j in range(1, layer + 1):
            lb = lb + e[j:j + 1, :]
        lb = jnp.clip(lb / jnp.sum(e, axis=0, keepdims=True), 0.0, 1.0)

        def to_heads(ref, a):
            for hd in range(HEADS):
                ref[hd] = a[:, hd * HEAD_DIM:(hd + 1) * HEAD_DIM]

        to_heads(q_ref, _silu(proj(SEC_HQ)))
        to_heads(f_ref, lb + (1.0 - lb) * _sigmoid(proj(SEC_HF)))
        to_heads(v_ref, proj(SEC_HI))
        to_heads(g_ref, _silu(proj(SEC_HG)))

        u = proj(SEC_PV)
        ext_ref[HALO:HALO + ts, :] = u
        first = si * GRID_TILE + r0 + 1
        pos = (first + lax.broadcasted_iota(jnp.int32, (ts, 1), 0)).astype(jnp.float32)
        mixed = []
        for gi, w in enumerate(POOL_WINDOWS):
            cols = slice(gi * POOL_GROUP_DIM, (gi + 1) * POOL_GROUP_DIM)
            acc = ext_ref[:, cols]
            step = 1
            while step < w:
                acc = acc + _roll_rows(acc, step)
                step *= 2
            inv_count = 1.0 / jnp.minimum(pos, float(w))
            pooled = acc[HALO:, :] * inv_count - u[:, cols]
            mixed.append(_bdot(pooled, poolw_ref[gi]))
        ext_ref[0:HALO, :] = u[ts - HALO:, :]
        pooled = jnp.concatenate(mixed, axis=-1) * pscale_ref[...]
        branch_a = _bdot(pooled * _silu(proj(SEC_PG)), wpo_ref[...])

        isub = lax.broadcasted_iota(jnp.int32, (SUBLANES, HEAD_DIM), 0)
        cat = functools.partial(jnp.concatenate, axis=0)

        def tile_rows(ci):
            return [pl.ds(ci * CHUNK + GROUP_TOKENS * (t // TILES_PER_GROUP) + t % TILES_PER_GROUP, SUBLANES,
                          stride=ROW_STRIDE) for t in range(TILES)]

        def scan_chunk(ci):
            rows_of = tile_rows(ci)
            for hd in range(HEADS):
                q, f, v = ([ref[hd, rows, :] for rows in rows_of] for ref in (q_ref, f_ref, v_ref))
                lhs, rhs, total, local = _scan_operands(q, f, v, isub)
                for i in range(N_OPS):
                    lhs_ref[ci, hd, i] = cat(lhs[i]).astype(jnp.bfloat16)
                    rhs_ref[ci, hd, i] = cat(rhs[i]).astype(jnp.bfloat16)
                vb_ref[ci, hd] = cat(v).astype(jnp.bfloat16)
                tot_ref[ci, hd] = _last_row(total)
                loc_ref[ci, hd] = cat(local)

        r2 = _token_of_row(lax.broadcasted_iota(jnp.int32, (CHUNK, CHUNK), 0))
        c2 = _token_of_row(lax.broadcasted_iota(jnp.int32, (CHUNK, CHUNK), 1))
        blk_eq = [(r2 >> (bit + 1)) == (c2 >> (bit + 1)) for bit in range(TIME_BITS)]
        hng = hng_ref[...]

        def output_chunk(ci):
            rows_of = tile_rows(ci)
            heads = range(HEADS)

            def nt(hd, i):
                return lax.dot_general(lhs_ref[ci, hd, i], rhs_ref[ci, hd, i], NT,
                                       preferred_element_type=jnp.float32)

            scores = [None] * HEADS
            for bit in range(LOCAL_BITS, TIME_BITS):
                for hd in heads:
                    p = nt(hd, bit - LOCAL_BITS)
                    if bit < TIME_BITS - 1:
                        p = jnp.where(blk_eq[bit], p, 0.0)
                    scores[hd] = p if scores[hd] is None else scores[hd] + p
            state_t = [state_ref[hd] for hd in heads]
            o_inter = [lax.dot_general(lhs_ref[ci, hd, N_OPS - 1], state_t[hd].astype(jnp.bfloat16), NT,
                                       preferred_element_type=jnp.float32) for hd in heads]
            for hd in heads:
                state_ref[hd] = (state_t[hd] * tot_ref[ci, hd][0:1, :]
                                 + lax.dot_general(vb_ref[ci, hd], rhs_ref[ci, hd, N_OPS - 1], TN,
                                                   preferred_element_type=jnp.float32))
            for hd in heads:
                o = (o_inter[hd] + loc_ref[ci, hd]
                     + jnp.dot(scores[hd].astype(jnp.bfloat16), vb_ref[ci, hd],
                               preferred_element_type=jnp.float32))
                for t, rows in enumerate(rows_of):
                    o_tile = o[SUBLANES * t:SUBLANES * (t + 1), :]
                    ob_ref[hd, rows, :] = (_rms(o_tile) * hng) * g_ref[hd, rows, :]

        for ci in range(ts // CHUNK):
            scan_chunk(ci)
        for ci in range(ts // CHUNK):
            output_chunk(ci)
        gated = jnp.concatenate([ob_ref[hd] for hd in range(HEADS)], axis=-1)
        branch_b = _bdot(gated, who_ref[...])

        merged = _sigmoid(proj(SEC_MGP)) * branch_a + _sigmoid(proj(SEC_MGH)) * branch_b
        y = _bdot(merged, wout_ref[...])
        o_ref[pl.ds(r0, SEQ_TILE), :] = x + _rms(y) * (gate * gpost_ref[...])
        return carry

    lax.fori_loop(0, GRID_TILE // SEQ_TILE, tile_pass, 0)


def _resident(shape, index_map):
    return pl.BlockSpec(shape, index_map, pipeline_mode=pl.Buffered(1))


def _nbytes(shape, dtype):
    return math.prod(shape) * jnp.dtype(dtype).itemsize


def _layer_call(layer, x, ada, g_pre, g_post, lb_logits, pool_scale, hgrn_norm_g,
                w_in, pool_w, w_pool_o, w_hgrn_o, w_out):
    batch, seq, d = x.shape
    depth = lb_logits.shape[0]
    ts = SEQ_TILE
    assert seq % GRID_TILE == 0 and GRID_TILE % ts == 0 and ts % CHUNK == 0 and d == D_MODEL
    chunks = ts // CHUNK

    def const2(b, s):
        return (0, 0)

    def per_layer3(b, s):
        return (layer, 0, 0)

    in_specs = [
        pl.BlockSpec((None, GRID_TILE, d), lambda b, s: (b, s, 0)),
        pl.BlockSpec((None, 3, d), lambda b, s: (b, 0, 0)),
        _resident((1, d), const2),
        _resident((1, d), const2),
        _resident((depth, HGRN_WIDTH), const2),
        _resident((1, POOL_WIDTH), const2),
        _resident((1, HEAD_DIM), const2),
        _resident((None, d, sum(IN_SIZES)), per_layer3),
        _resident((None, POOL_GROUPS, POOL_GROUP_DIM, POOL_GROUP_DIM), lambda b, s: (layer, 0, 0, 0)),
        _resident((None, POOL_WIDTH, d), per_layer3),
        _resident((None, HGRN_WIDTH, d), per_layer3),
        _resident((None, d, d), per_layer3),
    ]
    scratch = [
        pltpu.VMEM((HEADS, ts, HEAD_DIM), jnp.float32),
        pltpu.VMEM((HEADS, ts, HEAD_DIM), jnp.float32),
        pltpu.VMEM((HEADS, ts, HEAD_DIM), jnp.float32),
        pltpu.VMEM((HEADS, ts, HEAD_DIM), jnp.float32),
        pltpu.VMEM((HEADS, ts, HEAD_DIM), jnp.float32),
        pltpu.VMEM((HALO + ts, POOL_WIDTH), jnp.float32),
        pltpu.VMEM((HEADS, HEAD_DIM, HEAD_DIM), jnp.float32),
        pltpu.VMEM((chunks, HEADS, N_OPS, CHUNK, HEAD_DIM), jnp.bfloat16),
        pltpu.VMEM((chunks, HEADS, N_OPS, CHUNK, HEAD_DIM), jnp.bfloat16),
        pltpu.VMEM((chunks, HEADS, CHUNK, HEAD_DIM), jnp.bfloat16),
        pltpu.VMEM((chunks, HEADS, SUBLANES, HEAD_DIM), jnp.float32),
        pltpu.VMEM((chunks, HEADS, CHUNK, HEAD_DIM), jnp.float32),
    ]
    weights = (w_in, pool_w, w_pool_o, w_hgrn_o, w_out)
    vmem_bytes = (sum(_nbytes(w.shape[1:], w.dtype) for w in weights)
                  + 2 * 2 * _nbytes((GRID_TILE, d), x.dtype)
                  + sum(_nbytes(s.shape, s.dtype) for s in scratch)
                  + COMPILER_TEMP_BYTES)
    assert vmem_bytes <= V7X_VMEM_BYTES, vmem_bytes
    return pl.pallas_call(
        functools.partial(_layer_kernel, layer=layer),
        grid=(batch, seq // GRID_TILE),
        in_specs=in_specs,
        out_specs=pl.BlockSpec((None, GRID_TILE, d), lambda b, s: (b, s, 0)),
        out_shape=jax.ShapeDtypeStruct(x.shape, x.dtype),
        scratch_shapes=scratch,
        compiler_params=pltpu.CompilerParams(
            dimension_semantics=("arbitrary", "arbitrary"),
            vmem_limit_bytes=vmem_bytes),
        name=f"hybrid_layer{layer}",
    )(x, ada, g_pre[layer][None, :], g_post[layer][None, :], lb_logits,
      pool_scale[layer][None, :], hgrn_norm_g[layer][None, :],
      w_in, pool_w, w_pool_o, w_hgrn_o, w_out)


def _ada_call(c, w_ada, b_ada):
    depth, d, width = w_ada.shape
    batch = c.shape[0]
    assert width % ADA_TILE == 0
    return pl.pallas_call(
        _ada_kernel,
        grid=(depth, width // ADA_TILE),
        in_specs=[
            pl.BlockSpec((batch, d), lambda l, n: (0, 0)),
            pl.BlockSpec((None, d, ADA_TILE), lambda l, n: (l, 0, n)),
            pl.BlockSpec((None, 1, ADA_TILE), lambda l, n: (l, 0, n)),
        ],
        out_specs=pl.BlockSpec((None, batch, ADA_TILE), lambda l, n: (l, 0, n)),
        out_shape=jax.ShapeDtypeStruct((depth, batch, width), jnp.float32),
        name="adaln_vectors",
    )(c, w_ada, b_ada[:, None, :])


def kernel(x, c, w_ada, b_ada, g_pre, g_post, w_in, pool_w, pool_scale, lb_logits,
           hgrn_norm_g, w_pool_o, w_hgrn_o, w_out):
    depth = w_in.shape[0]
    batch, _, d = x.shape
    ada = _ada_call(c, w_ada, b_ada).reshape(depth, batch, 3, d)
    bf = jnp.bfloat16
    w_in_b, pool_w_b = w_in.astype(bf), pool_w.astype(bf)
    w_pool_o_b, w_hgrn_o_b, w_out_b = w_pool_o.astype(bf), w_hgrn_o.astype(bf), w_out.astype(bf)
    for layer in range(depth):
        x = _layer_call(layer, x, ada[layer], g_pre, g_post, lb_logits, pool_scale, hgrn_norm_g,
                        w_in_b, pool_w_b, w_pool_o_b, w_hgrn_o_b, w_out_b)
    return x
```

```python
import functools
import math

import jax
import jax.numpy as jnp
from jax import lax
from jax.experimental import pallas as pl
from jax.experimental.pallas import tpu as pltpu

D_MODEL = 1024
POOL_WINDOWS = (2, 4, 8, 16)
POOL_GROUPS = len(POOL_WINDOWS)
POOL_GROUP_DIM = D_MODEL // 8
POOL_WIDTH = POOL_GROUPS * POOL_GROUP_DIM
HEAD_DIM = 128
HEADS = D_MODEL // HEAD_DIM
HGRN_WIDTH = HEADS * HEAD_DIM
NORM_EPS = 1e-6
F_FLOOR = 1e-30
NEG_LOG2_E = -1.4426950408889634
IN_SIZES = (POOL_WIDTH, POOL_WIDTH, HGRN_WIDTH, HGRN_WIDTH, HGRN_WIDTH, HGRN_WIDTH, D_MODEL, D_MODEL)
IN_OFFSETS = tuple(sum(IN_SIZES[:i]) for i in range(len(IN_SIZES)))
(SEC_PV, SEC_PG, SEC_HQ, SEC_HF, SEC_HI, SEC_HG, SEC_MGP, SEC_MGH) = range(len(IN_SIZES))

GRID_TILE = 512
SEQ_TILE = 256
CHUNK = 64
HALO = max(POOL_WINDOWS)
SUBLANES = 8
TILES = CHUNK // SUBLANES
TILE_BITS = (0, 1, 5)
SUBLANE_BITS = (2, 3, 4)
TIME_BITS = 6
TILES_PER_GROUP = 4
ROW_STRIDE = TILES_PER_GROUP
GROUP_TOKENS = TILES_PER_GROUP * SUBLANES
assert CHUNK == 1 << TIME_BITS
LOCAL_BITS = 2
assert TILE_BITS[:LOCAL_BITS] == tuple(range(LOCAL_BITS)) and TILES_PER_GROUP == 1 << LOCAL_BITS
N_OPS = TIME_BITS - LOCAL_BITS + 1
ADA_TILE = 1024
V7X_VMEM_BYTES = 64 * 1024 * 1024
COMPILER_TEMP_BYTES = 3 * SEQ_TILE * D_MODEL * 4 * 4

NT = (((1,), (1,)), ((), ()))
TN = (((0,), (0,)), ((), ()))


def _bdot(a, b):
    return jnp.dot(a.astype(jnp.bfloat16), b.astype(jnp.bfloat16),
                   preferred_element_type=jnp.float32)


def _sigmoid(z):
    return 1.0 / (1.0 + jnp.exp2(z * NEG_LOG2_E))


def _silu(z):
    return z * _sigmoid(z)


def _rms(x):
    return x * lax.rsqrt(jnp.mean(x * x, axis=-1, keepdims=True) + NORM_EPS)


def _roll_rows(a, shift):
    return pltpu.roll(a, shift % a.shape[0], axis=0)


def _token_of_row(p):
    group, j, i = p // GROUP_TOKENS, (p // SUBLANES) % TILES_PER_GROUP, p % SUBLANES
    return group * GROUP_TOKENS + i * ROW_STRIDE + j


def _ada_kernel(c_ref, w_ref, b_ref, o_ref):
    c = c_ref[...]
    o_ref[...] = jnp.dot(_silu(c), w_ref[...], preferred_element_type=jnp.float32) + b_ref[...]


def _sublane_partner(p, sb, isub):
    k = 1
    while k < sb:
        p = jnp.where((isub & k) != 0, p, _roll_rows(p, -k))
        k *= 2
    if 2 * sb == p.shape[0]:
        return _roll_rows(p, sb)
    return jnp.where((isub & sb) != 0, _roll_rows(p, sb), _roll_rows(p, -sb))


def _last_row(a):
    return jnp.broadcast_to(a[SUBLANES - 1:SUBLANES, :], a.shape)


def _lane_sum(a):
    return jnp.sum(a, axis=-1, keepdims=True)


def _local_outputs(q, pf, kk, v):
    out = []
    for base in range(0, TILES, TILES_PER_GROUP):
        qs, fs, ks, vs = (a[base:base + TILES_PER_GROUP] for a in (q, pf, kk, v))
        acc = [_lane_sum(qs[j] * ks[j]) * vs[j] for j in range(TILES_PER_GROUP)]
        decayed_k = {}
        for j_t in range(1, TILES_PER_GROUP):
            qf = qs[j_t] * fs[j_t]
            for j_s in range(j_t - 1, -1, -1):
                decayed_k[j_t, j_s] = (ks[j_s] if j_s == j_t - 1
                                       else decayed_k[j_t - 1, j_s] * fs[j_t - 1])
                acc[j_t] = acc[j_t] + _lane_sum(qf * decayed_k[j_t, j_s]) * vs[j_s]
        out.extend(acc)
    return out


def _scan_operands(q, f, v, isub):
    tiles = range(TILES)
    kk = [1.0 - f[t] for t in tiles]
    pre = [jnp.maximum(f[t], F_FLOOR) for t in tiles]
    ks = list(kk)
    local = _local_outputs(q, pre, kk, v)
    zero = jnp.zeros_like(q[0])
    lhs, rhs = [], []
    for bit in range(TIME_BITS):
        if bit in TILE_BITS:
            tb = 1 << TILE_BITS.index(bit)
            if bit >= LOCAL_BITS:
                lhs.append([q[t] * pre[t] if t & tb else zero for t in tiles])
                rhs.append([zero if t & tb else ks[t] for t in tiles])
        else:
            sb = 1 << SUBLANE_BITS.index(bit)
            upper = (isub & sb) != 0
            lhs.append([jnp.where(upper, q[t] * pre[t], 0.0) for t in tiles])
            rhs.append([jnp.where(upper, 0.0, ks[t]) for t in tiles])
        new_pre, new_ks = list(pre), list(ks)
        if bit in TILE_BITS:
            totals = {}
            for t in tiles:
                src = (t ^ tb) | (tb - 1)
                if src not in totals:
                    totals[src] = _last_row(pre[src]) if bit > max(SUBLANE_BITS) else pre[src]
                if t & tb:
                    new_pre[t] = pre[t] * totals[src]
                else:
                    new_ks[t] = ks[t] * totals[src]
        else:
            for base in range(0, TILES, TILES_PER_GROUP):
                partner = _sublane_partner(pre[base + TILES_PER_GROUP - 1], sb, isub)
                pre_mul = jnp.where(upper, partner, 1.0)
                ks_mul = jnp.where(upper, 1.0, partner)
                for t in range(base, base + TILES_PER_GROUP):
                    new_pre[t] = pre[t] * pre_mul
                    new_ks[t] = ks[t] * ks_mul
        pre, ks = new_pre, new_ks
    lhs.append([q[t] * pre[t] for t in tiles])
    rhs.append(ks)
    return lhs, rhs, pre[TILES - 1], local


def _layer_kernel(x_ref, ada_ref, gpre_ref, gpost_ref, lbl_ref, pscale_ref, hng_ref,
                  win_ref, poolw_ref, wpo_ref, who_ref, wout_ref, o_ref,
                  q_ref, f_ref, v_ref, g_ref, ob_ref, ext_ref, state_ref,
                  lhs_ref, rhs_ref, vb_ref, tot_ref, loc_ref, *, layer):
    ts = SEQ_TILE
    si = pl.program_id(1)

    @pl.when(si == 0)
    def _():
        state_ref[...] = jnp.zeros_like(state_ref)
        ext_ref[0:HALO, :] = jnp.zeros((HALO, POOL_WIDTH), jnp.float32)

    def tile_pass(part, carry):
        r0 = pl.multiple_of(part * SEQ_TILE, SEQ_TILE)
        x = x_ref[pl.ds(r0, SEQ_TILE), :]
        shift = ada_ref[0:1, :]
        scale = ada_ref[1:2, :]
        gate = ada_ref[2:3, :]
        hb = (_rms(x) * (gpre_ref[...] * (1.0 + scale)) + shift).astype(jnp.bfloat16)

        def proj(section):
            cols = slice(IN_OFFSETS[section], IN_OFFSETS[section] + IN_SIZES[section])
            return jnp.dot(hb, win_ref[:, cols], preferred_element_type=jnp.float32)

        logits = lbl_ref[...]
        e = jnp.exp(logits - jnp.max(logits, axis=0, keepdims=True))
        lb = jnp.zeros((1, HGRN_WIDTH), jnp.float32)
        for j in range(1, layer + 1):
            lb = lb + e[j:j + 1, :]
        lb = jnp.clip(lb / jnp.sum(e, axis=0, keepdims=True), 0.0, 1.0)

        def to_heads(ref, a):
            for hd in range(HEADS):
                ref[hd] = a[:, hd * HEAD_DIM:(hd + 1) * HEAD_DIM]

        to_heads(q_ref, _silu(proj(SEC_HQ)))
        to_heads(f_ref, lb + (1.0 - lb) * _sigmoid(proj(SEC_HF)))
        to_heads(v_ref, proj(SEC_HI))
        to_heads(g_ref, _silu(proj(SEC_HG)))

        u = proj(SEC_PV)
        ext_ref[HALO:HALO + ts, :] = u
        first = si * GRID_TILE + r0 + 1
        pos = (first + lax.broadcasted_iota(jnp.int32, (ts, 1), 0)).astype(jnp.float32)
        mixed = []
        for gi, w in enumerate(POOL_WINDOWS):
            cols = slice(gi * POOL_GROUP_DIM, (gi + 1) * POOL_GROUP_DIM)
            acc = ext_ref[:, cols]
            step = 1
            while step < w:
                acc = acc + _roll_rows(acc, step)
                step *= 2
            inv_count = 1.0 / jnp.minimum(pos, float(w))
            pooled = acc[HALO:, :] * inv_count - u[:, cols]
            mixed.append(_bdot(pooled, poolw_ref[gi]))
        ext_ref[0:HALO, :] = u[ts - HALO:, :]
        pooled = jnp.concatenate(mixed, axis=-1) * pscale_ref[...]
        branch_a = _bdot(pooled * _silu(proj(SEC_PG)), wpo_ref[...])

        isub = lax.broadcasted_iota(jnp.int32, (SUBLANES, HEAD_DIM), 0)
        cat = functools.partial(jnp.concatenate, axis=0)

        def tile_rows(ci):
            return [pl.ds(ci * CHUNK + GROUP_TOKENS * (t // TILES_PER_GROUP) + t % TILES_PER_GROUP, SUBLANES,
                          stride=ROW_STRIDE) for t in range(TILES)]

        def scan_chunk(ci):
            rows_of = tile_rows(ci)
            for hd in range(HEADS):
                q, f, v = ([ref[hd, rows, :] for rows in rows_of] for ref in (q_ref, f_ref, v_ref))
                lhs, rhs, total, local = _scan_operands(q, f, v, isub)
                for i in range(N_OPS):
                    lhs_ref[ci, hd, i] = cat(lhs[i]).astype(jnp.bfloat16)
                    rhs_ref[ci, hd, i] = cat(rhs[i]).astype(jnp.bfloat16)
                vb_ref[ci, hd] = cat(v).astype(jnp.bfloat16)
                tot_ref[ci, hd] = _last_row(total)
                loc_ref[ci, hd] = cat(local)

        r2 = _token_of_row(lax.broadcasted_iota(jnp.int32, (CHUNK, CHUNK), 0))
        c2 = _token_of_row(lax.broadcasted_iota(jnp.int32, (CHUNK, CHUNK), 1))
        blk_eq = [(r2 >> (bit + 1)) == (c2 >> (bit + 1)) for bit in range(TIME_BITS)]
        hng = hng_ref[...]

        def output_chunk(ci):
            rows_of = tile_rows(ci)
            heads = range(HEADS)

            def nt(hd, i):
                return lax.dot_general(lhs_ref[ci, hd, i], rhs_ref[ci, hd, i], NT,
                                       preferred_element_type=jnp.float32)

            scores = [None] * HEADS
            for bit in range(LOCAL_BITS, TIME_BITS):
                for hd in heads:
                    p = nt(hd, bit - LOCAL_BITS)
                    if bit < TIME_BITS - 1:
                        p = jnp.where(blk_eq[bit], p, 0.0)
                    scores[hd] = p if scores[hd] is None else scores[hd] + p
            state_t = [state_ref[hd] for hd in heads]
            o_inter = [lax.dot_general(lhs_ref[ci, hd, N_OPS - 1], state_t[hd].astype(jnp.bfloat16), NT,
                                       preferred_element_type=jnp.float32) for hd in heads]
            for hd in heads:
                state_ref[hd] = (state_t[hd] * tot_ref[ci, hd][0:1, :]
                                 + lax.dot_general(vb_ref[ci, hd], rhs_ref[ci, hd, N_OPS - 1], TN,
                                                   preferred_element_type=jnp.float32))
            for hd in heads:
                o = (o_inter[hd] + loc_ref[ci, hd]
                     + jnp.dot(scores[hd].astype(jnp.bfloat16), vb_ref[ci, hd],
                               preferred_element_type=jnp.float32))
                for t, rows in enumerate(rows_of):
                    o_tile = o[SUBLANES * t:SUBLANES * (t + 1), :]
                    ob_ref[hd, rows, :] = (_rms(o_tile) * hng) * g_ref[hd, rows, :]

        for ci in range(ts // CHUNK):
            scan_chunk(ci)
        for ci in range(ts // CHUNK):
            output_chunk(ci)
        gated = jnp.concatenate([ob_ref[hd] for hd in range(HEADS)], axis=-1)
        branch_b = _bdot(gated, who_ref[...])

        merged = _sigmoid(proj(SEC_MGP)) * branch_a + _sigmoid(proj(SEC_MGH)) * branch_b
        y = _bdot(merged, wout_ref[...])
        o_ref[pl.ds(r0, SEQ_TILE), :] = x + _rms(y) * (gate * gpost_ref[...])
        return carry

    lax.fori_loop(0, GRID_TILE // SEQ_TILE, tile_pass, 0)


def _resident(shape, index_map):
    return pl.BlockSpec(shape, index_map, pipeline_mode=pl.Buffered(1))


def _nbytes(shape, dtype):
    return math.prod(shape) * jnp.dtype(dtype).itemsize


def _layer_call(layer, x, ada, g_pre, g_post, lb_logits, pool_scale, hgrn_norm_g,
                w_in, pool_w, w_pool_o, w_hgrn_o, w_out):
    batch, seq, d = x.shape
    depth = lb_logits.shape[0]
    ts = SEQ_TILE
    assert seq % GRID_TILE == 0 and GRID_TILE % ts == 0 and ts % CHUNK == 0 and d == D_MODEL
    chunks = ts // CHUNK

    def const2(b, s):
        return (0, 0)

    def per_layer3(b, s):
        return (layer, 0, 0)

    in_specs = [
        pl.BlockSpec((None, GRID_TILE, d), lambda b, s: (b, s, 0)),
        pl.BlockSpec((None, 3, d), lambda b, s: (b, 0, 0)),
        _resident((1, d), const2),
        _resident((1, d), const2),
        _resident((depth, HGRN_WIDTH), const2),
        _resident((1, POOL_WIDTH), const2),
        _resident((1, HEAD_DIM), const2),
        _resident((None, d, sum(IN_SIZES)), per_layer3),
        _resident((None, POOL_GROUPS, POOL_GROUP_DIM, POOL_GROUP_DIM), lambda b, s: (layer, 0, 0, 0)),
        _resident((None, POOL_WIDTH, d), per_layer3),
        _resident((None, HGRN_WIDTH, d), per_layer3),
        _resident((None, d, d), per_layer3),
    ]
    scratch = [
        pltpu.VMEM((HEADS, ts, HEAD_DIM), jnp.float32),
        pltpu.VMEM((HEADS, ts, HEAD_DIM), jnp.float32),
        pltpu.VMEM((HEADS, ts, HEAD_DIM), jnp.float32),
        pltpu.VMEM((HEADS, ts, HEAD_DIM), jnp.float32),
        pltpu.VMEM((HEADS, ts, HEAD_DIM), jnp.float32),
        pltpu.VMEM((HALO + ts, POOL_WIDTH), jnp.float32),
        pltpu.VMEM((HEADS, HEAD_DIM, HEAD_DIM), jnp.float32),
        pltpu.VMEM((chunks, HEADS, N_OPS, CHUNK, HEAD_DIM), jnp.bfloat16),
        pltpu.VMEM((chunks, HEADS, N_OPS, CHUNK, HEAD_DIM), jnp.bfloat16),
        pltpu.VMEM((chunks, HEADS, CHUNK, HEAD_DIM), jnp.bfloat16),
        pltpu.VMEM((chunks, HEADS, SUBLANES, HEAD_DIM), jnp.float32),
        pltpu.VMEM((chunks, HEADS, CHUNK, HEAD_DIM), jnp.float32),
    ]
    weights = (w_in, pool_w, w_pool_o, w_hgrn_o, w_out)
    vmem_bytes = (sum(_nbytes(w.shape[1:], w.dtype) for w in weights)
                  + 2 * 2 * _nbytes((GRID_TILE, d), x.dtype)
                  + sum(_nbytes(s.shape, s.dtype) for s in scratch)
                  + COMPILER_TEMP_BYTES)
    assert vmem_bytes <= V7X_VMEM_BYTES, vmem_bytes
    return pl.pallas_call(
        functools.partial(_layer_kernel, layer=layer),
        grid=(batch, seq // GRID_TILE),
        in_specs=in_specs,
        out_specs=pl.BlockSpec((None, GRID_TILE, d), lambda b, s: (b, s, 0)),
        out_shape=jax.ShapeDtypeStruct(x.shape, x.dtype),
        scratch_shapes=scratch,
        compiler_params=pltpu.CompilerParams(
            dimension_semantics=("arbitrary", "arbitrary"),
            vmem_limit_bytes=vmem_bytes),
        name=f"hybrid_layer{layer}",
    )(x, ada, g_pre[layer][None, :], g_post[layer][None, :], lb_logits,
      pool_scale[layer][None, :], hgrn_norm_g[layer][None, :],
      w_in, pool_w, w_pool_o, w_hgrn_o, w_out)


def _ada_call(c, w_ada, b_ada):
    depth, d, width = w_ada.shape
    batch = c.shape[0]
    assert width % ADA_TILE == 0
    return pl.pallas_call(
        _ada_kernel,
        grid=(depth, width // ADA_TILE),
        in_specs=[
            pl.BlockSpec((batch, d), lambda l, n: (0, 0)),
            pl.BlockSpec((None, d, ADA_TILE), lambda l, n: (l, 0, n)),
            pl.BlockSpec((None, 1, ADA_TILE), lambda l, n: (l, 0, n)),
        ],
        out_specs=pl.BlockSpec((None, batch, ADA_TILE), lambda l, n: (l, 0, n)),
        out_shape=jax.ShapeDtypeStruct((depth, batch, width), jnp.float32),
        name="adaln_vectors",
    )(c, w_ada, b_ada[:, None, :])


def kernel(x, c, w_ada, b_ada, g_pre, g_post, w_in, pool_w, pool_scale, lb_logits,
           hgrn_norm_g, w_pool_o, w_hgrn_o, w_out):
    depth = w_in.shape[0]
    batch, _, d = x.shape
    ada = _ada_call(c, w_ada, b_ada).reshape(depth, batch, 3, d)
    bf = jnp.bfloat16
    w_in_b, pool_w_b = w_in.astype(bf), pool_w.astype(bf)
    w_pool_o_b, w_hgrn_o_b, w_out_b = w_pool_o.astype(bf), w_hgrn_o.astype(bf), w_out.astype(bf)
    for layer in range(depth):
        x = _layer_call(layer, x, ada[layer], g_pre, g_post, lb_logits, pool_scale, hgrn_norm_g,
                        w_in_b, pool_w_b, w_pool_o_b, w_hgrn_o_b, w_out_b)
    return x
```

```python
import functools
import math

import jax
import jax.numpy as jnp
from jax import lax
from jax.experimental import pallas as pl
from jax.experimental.pallas import tpu as pltpu

D_MODEL = 1024
POOL_WINDOWS = (2, 4, 8, 16)
POOL_GROUPS = len(POOL_WINDOWS)
POOL_GROUP_DIM = D_MODEL // 8
POOL_WIDTH = POOL_GROUPS * POOL_GROUP_DIM
HEAD_DIM = 128
HEADS = D_MODEL // HEAD_DIM
HGRN_WIDTH = HEADS * HEAD_DIM
NORM_EPS = 1e-6
F_FLOOR = 1e-30
NEG_LOG2_E = -1.4426950408889634
IN_SIZES = (POOL_WIDTH, POOL_WIDTH, HGRN_WIDTH, HGRN_WIDTH, HGRN_WIDTH, HGRN_WIDTH, D_MODEL, D_MODEL)
IN_OFFSETS = tuple(sum(IN_SIZES[:i]) for i in range(len(IN_SIZES)))
(SEC_PV, SEC_PG, SEC_HQ, SEC_HF, SEC_HI, SEC_HG, SEC_MGP, SEC_MGH) = range(len(IN_SIZES))

GRID_TILE = 512
SEQ_TILE = 256
CHUNK = 64
HALO = max(POOL_WINDOWS)
POOL_PAD = 8
POOL_DATA = POOL_PAD + HALO
SUBLANES = 8
TILES = CHUNK // SUBLANES
TILE_BITS = (0, 1, 5)
SUBLANE_BITS = (2, 3, 4)
TIME_BITS = 6
TILES_PER_GROUP = 4
ROW_STRIDE = TILES_PER_GROUP
GROUP_TOKENS = TILES_PER_GROUP * SUBLANES
assert CHUNK == 1 << TIME_BITS
LOCAL_BITS = 2
assert TILE_BITS[:LOCAL_BITS] == tuple(range(LOCAL_BITS)) and TILES_PER_GROUP == 1 << LOCAL_BITS
N_OPS = TIME_BITS - LOCAL_BITS + 1
ADA_TILE = 1024
V7X_VMEM_BYTES = 64 * 1024 * 1024
COMPILER_TEMP_BYTES = 3 * SEQ_TILE * D_MODEL * 4 * 4

NT = (((1,), (1,)), ((), ()))
TN = (((0,), (0,)), ((), ()))


def _bdot(a, b):
    return jnp.dot(a.astype(jnp.bfloat16), b.astype(jnp.bfloat16),
                   preferred_element_type=jnp.float32)


def _sigmoid(z):
    return 1.0 / (1.0 + jnp.exp2(z * NEG_LOG2_E))


def _silu(z):
    return z * _sigmoid(z)


def _rms(x):
    return x * lax.rsqrt(jnp.mean(x * x, axis=-1, keepdims=True) + NORM_EPS)


def _roll_rows(a, shift):
    return pltpu.roll(a, shift % a.shape[0], axis=0)


def _token_of_row(p):
    group, j, i = p // GROUP_TOKENS, (p // SUBLANES) % TILES_PER_GROUP, p % SUBLANES
    return group * GROUP_TOKENS + i * ROW_STRIDE + j


def _ada_kernel(c_ref, w_ref, b_ref, o_ref):
    c = c_ref[...]
    o_ref[...] = jnp.dot(_silu(c), w_ref[...], preferred_element_type=jnp.float32) + b_ref[...]


def _sublane_partner(p, sb, isub):
    k = 1
    while k < sb:
        p = jnp.where((isub & k) != 0, p, _roll_rows(p, -k))
        k *= 2
    if 2 * sb == p.shape[0]:
        return _roll_rows(p, sb)
    return jnp.where((isub & sb) != 0, _roll_rows(p, sb), _roll_rows(p, -sb))


def _last_row(a):
    return jnp.broadcast_to(a[SUBLANES - 1:SUBLANES, :], a.shape)


def _lane_sum(a):
    return jnp.sum(a, axis=-1, keepdims=True)


def _local_outputs(q, qf, pf, kk, v):
    out = []
    for base in range(0, TILES, TILES_PER_GROUP):
        qs, qfs, fs, ks, vs = (a[base:base + TILES_PER_GROUP] for a in (q, qf, pf, kk, v))
        acc = [_lane_sum(qs[j] * ks[j]) * vs[j] for j in range(TILES_PER_GROUP)]
        decayed_k = {}
        for j_t in range(1, TILES_PER_GROUP):
            for j_s in range(j_t - 1, -1, -1):
                decayed_k[j_t, j_s] = (ks[j_s] if j_s == j_t - 1
                                       else decayed_k[j_t - 1, j_s] * fs[j_t - 1])
                acc[j_t] = acc[j_t] + _lane_sum(qfs[j_t] * decayed_k[j_t, j_s]) * vs[j_s]
        out.extend(acc)
    return out


def _scan_operands(q, f, v, isub):
    tiles = range(TILES)
    kk = [1.0 - f[t] for t in tiles]
    pre = [jnp.maximum(f[t], F_FLOOR) for t in tiles]
    qp = [q[t] * pre[t] for t in tiles]
    ks = list(kk)
    local = _local_outputs(q, qp, pre, kk, v)
    zero = jnp.zeros_like(q[0])
    lhs, rhs = [], []
    for bit in range(TIME_BITS):
        if bit in TILE_BITS:
            tb = 1 << TILE_BITS.index(bit)
            if bit >= LOCAL_BITS:
                lhs.append([qp[t] if t & tb else zero for t in tiles])
                rhs.append([zero if t & tb else ks[t] for t in tiles])
        else:
            sb = 1 << SUBLANE_BITS.index(bit)
            upper = (isub & sb) != 0
            lhs.append([jnp.where(upper, qp[t], 0.0) for t in tiles])
            rhs.append([jnp.where(upper, 0.0, ks[t]) for t in tiles])
        new_pre, new_qp, new_ks = list(pre), list(qp), list(ks)
        if bit in TILE_BITS:
            totals = {}
            for t in tiles:
                src = (t ^ tb) | (tb - 1)
                if src not in totals:
                    totals[src] = _last_row(pre[src]) if bit > max(SUBLANE_BITS) else pre[src]
                if t & tb:
                    new_pre[t] = pre[t] * totals[src]
                    new_qp[t] = qp[t] * totals[src]
                else:
                    new_ks[t] = ks[t] * totals[src]
        else:
            for base in range(0, TILES, TILES_PER_GROUP):
                partner = _sublane_partner(pre[base + TILES_PER_GROUP - 1], sb, isub)
                pre_mul = jnp.where(upper, partner, 1.0)
                ks_mul = jnp.where(upper, 1.0, partner)
                for t in range(base, base + TILES_PER_GROUP):
                    new_pre[t] = pre[t] * pre_mul
                    new_qp[t] = qp[t] * pre_mul
                    new_ks[t] = ks[t] * ks_mul
        pre, qp, ks = new_pre, new_qp, new_ks
    lhs.append(qp)
    rhs.append(ks)
    return lhs, rhs, pre[TILES - 1], local


def _layer_kernel(x_ref, ada_ref, gpre_ref, gpost_ref, lbl_ref, pscale_ref, hng_ref,
                  win_ref, poolw_ref, wpo_ref, who_ref, wout_ref, o_ref,
                  q_ref, f_ref, v_ref, g_ref, ob_ref, ext_ref, pool_ref, state_ref,
                  lhs_ref, rhs_ref, vb_ref, tot_ref, loc_ref, *, layer):
    ts = SEQ_TILE
    si = pl.program_id(1)

    @pl.when(si == 0)
    def _():
        state_ref[...] = jnp.zeros_like(state_ref)
        ext_ref[0:POOL_DATA, :] = jnp.zeros((POOL_DATA, POOL_WIDTH), jnp.float32)
        pool_ref[:, 0:POOL_PAD, :] = jnp.zeros((2, POOL_PAD, POOL_GROUP_DIM), jnp.float32)

    def tile_pass(part, carry):
        r0 = pl.multiple_of(part * SEQ_TILE, SEQ_TILE)
        x = x_ref[pl.ds(r0, SEQ_TILE), :]
        shift = ada_ref[0:1, :]
        scale = ada_ref[1:2, :]
        gate = ada_ref[2:3, :]
        hb = (_rms(x) * (gpre_ref[...] * (1.0 + scale)) + shift).astype(jnp.bfloat16)

        def proj(section):
            cols = slice(IN_OFFSETS[section], IN_OFFSETS[section] + IN_SIZES[section])
            return jnp.dot(hb, win_ref[:, cols], preferred_element_type=jnp.float32)

        logits = lbl_ref[...]
        e = jnp.exp(logits - jnp.max(logits, axis=0, keepdims=True))
        lb = jnp.zeros((1, HGRN_WIDTH), jnp.float32)
        for j in range(1, layer + 1):
            lb = lb + e[j:j + 1, :]
        lb = jnp.clip(lb / jnp.sum(e, axis=0, keepdims=True), 0.0, 1.0)

        def to_heads(ref, a):
            for hd in range(HEADS):
                ref[hd] = a[:, hd * HEAD_DIM:(hd + 1) * HEAD_DIM]

        to_heads(q_ref, _silu(proj(SEC_HQ)))
        to_heads(f_ref, lb + (1.0 - lb) * _sigmoid(proj(SEC_HF)))
        to_heads(v_ref, proj(SEC_HI))
        to_heads(g_ref, _silu(proj(SEC_HG)))

        u = proj(SEC_PV)
        rows = POOL_DATA + ts
        ext_ref[POOL_DATA:rows, :] = u
        first = si * GRID_TILE + r0 + 1
        pos = (first + lax.broadcasted_iota(jnp.int32, (ts, 1), 0)).astype(jnp.float32)
        mixed = []
        for gi, w in enumerate(POOL_WINDOWS):
            cols = slice(gi * POOL_GROUP_DIM, (gi + 1) * POOL_GROUP_DIM)
            src, src_cols, span, stage = ext_ref, cols, 1, 0
            while span < w:
                lo = POOL_DATA if 2 * span == w else POOL_PAD
                partial = src[lo:rows, src_cols] + src[lo - span:rows - span, src_cols]
                if 2 * span < w:
                    pool_ref[stage % 2, lo:rows, :] = partial
                    src, src_cols = pool_ref.at[stage % 2], slice(None)
                span, stage = 2 * span, stage + 1
            inv_count = 1.0 / jnp.minimum(pos, float(w))
            pooled = partial * inv_count - u[:, cols]
            mixed.append(_bdot(pooled, poolw_ref[gi]))
        ext_ref[POOL_PAD:POOL_DATA, :] = u[ts - HALO:, :]
        pooled = jnp.concatenate(mixed, axis=-1) * pscale_ref[...]
        branch_a = _bdot(pooled * _silu(proj(SEC_PG)), wpo_ref[...])

        isub = lax.broadcasted_iota(jnp.int32, (SUBLANES, HEAD_DIM), 0)
        cat = functools.partial(jnp.concatenate, axis=0)

        def tile_rows(ci):
            return [pl.ds(ci * CHUNK + GROUP_TOKENS * (t // TILES_PER_GROUP) + t % TILES_PER_GROUP, SUBLANES,
                          stride=ROW_STRIDE) for t in range(TILES)]

        def scan_chunk(ci):
            rows_of = tile_rows(ci)
            for hd in range(HEADS):
                q, f, v = ([ref[hd, rows, :] for rows in rows_of] for ref in (q_ref, f_ref, v_ref))
                lhs, rhs, total, local = _scan_operands(q, f, v, isub)
                for i in range(N_OPS):
                    lhs_ref[ci, hd, i] = cat(lhs[i]).astype(jnp.bfloat16)
                    rhs_ref[ci, hd, i] = cat(rhs[i]).astype(jnp.bfloat16)
                vb_ref[ci, hd] = cat(v).astype(jnp.bfloat16)
                tot_ref[ci, hd] = _last_row(total)
                loc_ref[ci, hd] = cat(local)

        r2 = _token_of_row(lax.broadcasted_iota(jnp.int32, (CHUNK, CHUNK), 0))
        c2 = _token_of_row(lax.broadcasted_iota(jnp.int32, (CHUNK, CHUNK), 1))
        blk_eq = [(r2 >> (bit + 1)) == (c2 >> (bit + 1)) for bit in range(TIME_BITS)]
        hng = hng_ref[...]

        def output_chunk(ci):
            rows_of = tile_rows(ci)
            heads = range(HEADS)

            def nt(hd, i):
                return lax.dot_general(lhs_ref[ci, hd, i], rhs_ref[ci, hd, i], NT,
                                       preferred_element_type=jnp.float32)

            scores = [None] * HEADS
            for bit in range(LOCAL_BITS, TIME_BITS):
                for hd in heads:
                    p = nt(hd, bit - LOCAL_BITS)
                    if bit < TIME_BITS - 1:
                        p = jnp.where(blk_eq[bit], p, 0.0)
                    scores[hd] = p if scores[hd] is None else scores[hd] + p
            state_t = [state_ref[hd] for hd in heads]
            o_inter = [lax.dot_general(lhs_ref[ci, hd, N_OPS - 1], state_t[hd].astype(jnp.bfloat16), NT,
                                       preferred_element_type=jnp.float32) for hd in heads]
            for hd in heads:
                state_ref[hd] = (state_t[hd] * tot_ref[ci, hd][0:1, :]
                                 + lax.dot_general(vb_ref[ci, hd], rhs_ref[ci, hd, N_OPS - 1], TN,
                                                   preferred_element_type=jnp.float32))
            for hd in heads:
                o = (o_inter[hd] + loc_ref[ci, hd]
                     + jnp.dot(scores[hd].astype(jnp.bfloat16), vb_ref[ci, hd],
                               preferred_element_type=jnp.float32))
                for t, rows in enumerate(rows_of):
                    o_tile = o[SUBLANES * t:SUBLANES * (t + 1), :]
                    ob_ref[hd, rows, :] = (_rms(o_tile) * hng) * g_ref[hd, rows, :]

        for ci in range(ts // CHUNK):
            scan_chunk(ci)
        for ci in range(ts // CHUNK):
            output_chunk(ci)
        gated = jnp.concatenate([ob_ref[hd] for hd in range(HEADS)], axis=-1)
        branch_b = _bdot(gated, who_ref[...])

        merged = _sigmoid(proj(SEC_MGP)) * branch_a + _sigmoid(proj(SEC_MGH)) * branch_b
        y = _bdot(merged, wout_ref[...])
        o_ref[pl.ds(r0, SEQ_TILE), :] = x + _rms(y) * (gate * gpost_ref[...])
        return carry

    lax.fori_loop(0, GRID_TILE // SEQ_TILE, tile_pass, 0)


def _resident(shape, index_map):
    return pl.BlockSpec(shape, index_map, pipeline_mode=pl.Buffered(1))


def _nbytes(shape, dtype):
    return math.prod(shape) * jnp.dtype(dtype).itemsize


def _layer_call(layer, x, ada, g_pre, g_post, lb_logits, pool_scale, hgrn_norm_g,
                w_in, pool_w, w_pool_o, w_hgrn_o, w_out):
    batch, seq, d = x.shape
    depth = lb_logits.shape[0]
    ts = SEQ_TILE
    assert seq % GRID_TILE == 0 and GRID_TILE % ts == 0 and ts % CHUNK == 0 and d == D_MODEL
    chunks = ts // CHUNK

    def const2(b, s):
        return (0, 0)

    def per_layer3(b, s):
        return (layer, 0, 0)

    in_specs = [
        pl.BlockSpec((None, GRID_TILE, d), lambda b, s: (b, s, 0)),
        pl.BlockSpec((None, 3, d), lambda b, s: (b, 0, 0)),
        _resident((1, d), const2),
        _resident((1, d), const2),
        _resident((depth, HGRN_WIDTH), const2),
        _resident((1, POOL_WIDTH), const2),
        _resident((1, HEAD_DIM), const2),
        _resident((None, d, sum(IN_SIZES)), per_layer3),
        _resident((None, POOL_GROUPS, POOL_GROUP_DIM, POOL_GROUP_DIM), lambda b, s: (layer, 0, 0, 0)),
        _resident((None, POOL_WIDTH, d), per_layer3),
        _resident((None, HGRN_WIDTH, d), per_layer3),
        _resident((None, d, d), per_layer3),
    ]
    scratch = [
        pltpu.VMEM((HEADS, ts, HEAD_DIM), jnp.float32),
        pltpu.VMEM((HEADS, ts, HEAD_DIM), jnp.float32),
        pltpu.VMEM((HEADS, ts, HEAD_DIM), jnp.float32),
        pltpu.VMEM((HEADS, ts, HEAD_DIM), jnp.float32),
        pltpu.VMEM((HEADS, ts, HEAD_DIM), jnp.float32),
        pltpu.VMEM((POOL_DATA + ts, POOL_WIDTH), jnp.float32),
        pltpu.VMEM((2, POOL_DATA + ts, POOL_GROUP_DIM), jnp.float32),
        pltpu.VMEM((HEADS, HEAD_DIM, HEAD_DIM), jnp.float32),
        pltpu.VMEM((chunks, HEADS, N_OPS, CHUNK, HEAD_DIM), jnp.bfloat16),
        pltpu.VMEM((chunks, HEADS, N_OPS, CHUNK, HEAD_DIM), jnp.bfloat16),
        pltpu.VMEM((chunks, HEADS, CHUNK, HEAD_DIM), jnp.bfloat16),
        pltpu.VMEM((chunks, HEADS, SUBLANES, HEAD_DIM), jnp.float32),
        pltpu.VMEM((chunks, HEADS, CHUNK, HEAD_DIM), jnp.float32),
    ]
    weights = (w_in, pool_w, w_pool_o, w_hgrn_o, w_out)
    vmem_bytes = (sum(_nbytes(w.shape[1:], w.dtype) for w in weights)
                  + 2 * 2 * _nbytes((GRID_TILE, d), x.dtype)
                  + sum(_nbytes(s.shape, s.dtype) for s in scratch)
                  + COMPILER_TEMP_BYTES)
    assert vmem_bytes <= V7X_VMEM_BYTES, vmem_bytes
    return pl.pallas_call(
        functools.partial(_layer_kernel, layer=layer),
        grid=(batch, seq // GRID_TILE),
        in_specs=in_specs,
        out_specs=pl.BlockSpec((None, GRID_TILE, d), lambda b, s: (b, s, 0)),
        out_shape=jax.ShapeDtypeStruct(x.shape, x.dtype),
        scratch_shapes=scratch,
        compiler_params=pltpu.CompilerParams(
            dimension_semantics=("arbitrary", "arbitrary"),
            vmem_limit_bytes=vmem_bytes),
        name=f"hybrid_layer{layer}",
    )(x, ada, g_pre[layer][None, :], g_post[layer][None, :], lb_logits,
      pool_scale[layer][None, :], hgrn_norm_g[layer][None, :],
      w_in, pool_w, w_pool_o, w_hgrn_o, w_out)


def _ada_call(c, w_ada, b_ada):
    depth, d, width = w_ada.shape
    batch = c.shape[0]
    assert width % ADA_TILE == 0
    return pl.pallas_call(
        _ada_kernel,
        grid=(depth, width // ADA_TILE),
        in_specs=[
            pl.BlockSpec((batch, d), lambda l, n: (0, 0)),
            pl.BlockSpec((None, d, ADA_TILE), lambda l, n: (l, 0, n)),
            pl.BlockSpec((None, 1, ADA_TILE), lambda l, n: (l, 0, n)),
        ],
        out_specs=pl.BlockSpec((None, batch, ADA_TILE), lambda l, n: (l, 0, n)),
        out_shape=jax.ShapeDtypeStruct((depth, batch, width), jnp.float32),
        name="adaln_vectors",
    )(c, w_ada, b_ada[:, None, :])


def kernel(x, c, w_ada, b_ada, g_pre, g_post, w_in, pool_w, pool_scale, lb_logits,
           hgrn_norm_g, w_pool_o, w_hgrn_o, w_out):
    depth = w_in.shape[0]
    batch, _, d = x.shape
    ada = _ada_call(c, w_ada, b_ada).reshape(depth, batch, 3, d)
    bf = jnp.bfloat16
    w_in_b, pool_w_b = w_in.astype(bf), pool_w.astype(bf)
    w_pool_o_b, w_hgrn_o_b, w_out_b = w_pool_o.astype(bf), w_hgrn_o.astype(bf), w_out.astype(bf)
    for layer in range(depth):
        x = _layer_call(layer, x, ada[layer], g_pre, g_post, lb_logits, pool_scale, hgrn_norm_g,
                        w_in_b, pool_w_b, w_pool_o_b, w_hgrn_o_b, w_out_b)
    return x
```

```python
import functools
import math

import jax
import jax.numpy as jnp
from jax import lax
from jax.experimental import pallas as pl
from jax.experimental.pallas import tpu as pltpu

D_MODEL = 1024
POOL_WINDOWS = (2, 4, 8, 16)
POOL_GROUPS = len(POOL_WINDOWS)
POOL_GROUP_DIM = D_MODEL // 8
POOL_WIDTH = POOL_GROUPS * POOL_GROUP_DIM
HEAD_DIM = 128
HEADS = D_MODEL // HEAD_DIM
HGRN_WIDTH = HEADS * HEAD_DIM
NORM_EPS = 1e-6
F_FLOOR = 1e-30
NEG_LOG2_E = -1.4426950408889634
IN_SIZES = (POOL_WIDTH, POOL_WIDTH, HGRN_WIDTH, HGRN_WIDTH, HGRN_WIDTH, HGRN_WIDTH, D_MODEL, D_MODEL)
IN_OFFSETS = tuple(sum(IN_SIZES[:i]) for i in range(len(IN_SIZES)))
(SEC_PV, SEC_PG, SEC_HQ, SEC_HF, SEC_HI, SEC_HG, SEC_MGP, SEC_MGH) = range(len(IN_SIZES))

GRID_TILE = 512
SEQ_TILE = 256
CHUNK = 64
HALO = max(POOL_WINDOWS)
SUBLANES = 8
TILES = CHUNK // SUBLANES
TILE_BITS = (0, 1, 5)
SUBLANE_BITS = (2, 3, 4)
TIME_BITS = 6
TILES_PER_GROUP = 4
ROW_STRIDE = TILES_PER_GROUP
GROUP_TOKENS = TILES_PER_GROUP * SUBLANES
assert CHUNK == 1 << TIME_BITS
LOCAL_BITS = 2
assert TILE_BITS[:LOCAL_BITS] == tuple(range(LOCAL_BITS)) and TILES_PER_GROUP == 1 << LOCAL_BITS
N_OPS = TIME_BITS - LOCAL_BITS + 1
ADA_TILE = 1024
V7X_VMEM_BYTES = 64 * 1024 * 1024
COMPILER_TEMP_BYTES = 3 * SEQ_TILE * D_MODEL * 4 * 4

NT = (((1,), (1,)), ((), ()))
TN = (((0,), (0,)), ((), ()))


def _bdot(a, b):
    return jnp.dot(a.astype(jnp.bfloat16), b.astype(jnp.bfloat16),
                   preferred_element_type=jnp.float32)


def _sigmoid(z):
    return 1.0 / (1.0 + jnp.exp2(z * NEG_LOG2_E))


def _silu(z):
    return z * _sigmoid(z)


def _rms(x):
    return x * lax.rsqrt(jnp.mean(x * x, axis=-1, keepdims=True) + NORM_EPS)


def _roll_rows(a, shift):
    return pltpu.roll(a, shift % a.shape[0], axis=0)


def _token_of_row(p):
    group, j, i = p // GROUP_TOKENS, (p // SUBLANES) % TILES_PER_GROUP, p % SUBLANES
    return group * GROUP_TOKENS + i * ROW_STRIDE + j


def _ada_kernel(c_ref, w_ref, b_ref, o_ref):
    c = c_ref[...]
    o_ref[...] = jnp.dot(_silu(c), w_ref[...], preferred_element_type=jnp.float32) + b_ref[...]


def _sublane_partner(p, sb, isub):
    k = 1
    while k < sb:
        p = jnp.where((isub & k) != 0, p, _roll_rows(p, -k))
        k *= 2
    if 2 * sb == p.shape[0]:
        return _roll_rows(p, sb)
    return jnp.where((isub & sb) != 0, _roll_rows(p, sb), _roll_rows(p, -sb))


def _last_row(a):
    return jnp.broadcast_to(a[SUBLANES - 1:SUBLANES, :], a.shape)


def _lane_sum(a):
    return jnp.sum(a, axis=-1, keepdims=True)


def _local_outputs(q, qf, pf, kk, v):
    out = []
    for base in range(0, TILES, TILES_PER_GROUP):
        qs, qfs, fs, ks, vs = (a[base:base + TILES_PER_GROUP] for a in (q, qf, pf, kk, v))
        acc = [_lane_sum(qs[j] * ks[j]) * vs[j] for j in range(TILES_PER_GROUP)]
        decayed_k = {}
        for j_t in range(1, TILES_PER_GROUP):
            for j_s in range(j_t - 1, -1, -1):
                decayed_k[j_t, j_s] = (ks[j_s] if j_s == j_t - 1
                                       else decayed_k[j_t - 1, j_s] * fs[j_t - 1])
                acc[j_t] = acc[j_t] + _lane_sum(qfs[j_t] * decayed_k[j_t, j_s]) * vs[j_s]
        out.extend(acc)
    return out


def _scan_operands(q, f, v, isub):
    tiles = range(TILES)
    kk = [1.0 - f[t] for t in tiles]
    pre = [jnp.maximum(f[t], F_FLOOR) for t in tiles]
    qp = [q[t] * pre[t] for t in tiles]
    ks = list(kk)
    local = _local_outputs(q, qp, pre, kk, v)
    zero = jnp.zeros_like(q[0])
    lhs, rhs = [], []
    for bit in range(TIME_BITS):
        if bit in TILE_BITS:
            tb = 1 << TILE_BITS.index(bit)
            if bit >= LOCAL_BITS:
                lhs.append([qp[t] if t & tb else zero for t in tiles])
                rhs.append([zero if t & tb else ks[t] for t in tiles])
        else:
            sb = 1 << SUBLANE_BITS.index(bit)
            upper = (isub & sb) != 0
            lhs.append([jnp.where(upper, qp[t], 0.0) for t in tiles])
            rhs.append([jnp.where(upper, 0.0, ks[t]) for t in tiles])
        new_pre, new_qp, new_ks = list(pre), list(qp), list(ks)
        if bit in TILE_BITS:
            totals = {}
            for t in tiles:
                src = (t ^ tb) | (tb - 1)
                if src not in totals:
                    totals[src] = _last_row(pre[src]) if bit > max(SUBLANE_BITS) else pre[src]
                if t & tb:
                    new_pre[t] = pre[t] * totals[src]
                    new_qp[t] = qp[t] * totals[src]
                else:
                    new_ks[t] = ks[t] * totals[src]
        else:
            for base in range(0, TILES, TILES_PER_GROUP):
                partner = _sublane_partner(pre[base + TILES_PER_GROUP - 1], sb, isub)
                pre_mul = jnp.where(upper, partner, 1.0)
                ks_mul = jnp.where(upper, 1.0, partner)
                for t in range(base, base + TILES_PER_GROUP):
                    new_pre[t] = pre[t] * pre_mul
                    new_qp[t] = qp[t] * pre_mul
                    new_ks[t] = ks[t] * ks_mul
        pre, qp, ks = new_pre, new_qp, new_ks
    lhs.append(qp)
    rhs.append(ks)
    return lhs, rhs, pre[TILES - 1], local


def _layer_kernel(x_ref, ada_ref, gpre_ref, gpost_ref, lbl_ref, pscale_ref, hng_ref,
                  win_ref, poolw_ref, wpo_ref, who_ref, wout_ref, o_ref,
                  q_ref, f_ref, v_ref, g_ref, ob_ref, ext_ref, state_ref,
                  lhs_ref, rhs_ref, vb_ref, tot_ref, loc_ref, *, layer):
    ts = SEQ_TILE
    si = pl.program_id(1)

    @pl.when(si == 0)
    def _():
        state_ref[...] = jnp.zeros_like(state_ref)
        ext_ref[0:HALO, :] = jnp.zeros((HALO, POOL_WIDTH), jnp.float32)

    def tile_pass(part, carry):
        r0 = pl.multiple_of(part * SEQ_TILE, SEQ_TILE)
        x = x_ref[pl.ds(r0, SEQ_TILE), :]
        shift = ada_ref[0:1, :]
        scale = ada_ref[1:2, :]
        gate = ada_ref[2:3, :]
        hb = (_rms(x) * (gpre_ref[...] * (1.0 + scale)) + shift).astype(jnp.bfloat16)

        def proj(section):
            cols = slice(IN_OFFSETS[section], IN_OFFSETS[section] + IN_SIZES[section])
            return jnp.dot(hb, win_ref[:, cols], preferred_element_type=jnp.float32)

        logits = lbl_ref[...]
        e = jnp.exp(logits - jnp.max(logits, axis=0, keepdims=True))
        lb = jnp.zeros((1, HGRN_WIDTH), jnp.float32)
        for j in range(1, layer + 1):
            lb = lb + e[j:j + 1, :]
        lb = jnp.clip(lb / jnp.sum(e, axis=0, keepdims=True), 0.0, 1.0)

        def to_heads(ref, a):
            for hd in range(HEADS):
                ref[hd] = a[:, hd * HEAD_DIM:(hd + 1) * HEAD_DIM]

        to_heads(q_ref, _silu(proj(SEC_HQ)))
        to_heads(f_ref, lb + (1.0 - lb) * _sigmoid(proj(SEC_HF)))
        to_heads(v_ref, proj(SEC_HI))
        to_heads(g_ref, _silu(proj(SEC_HG)))

        u = proj(SEC_PV)
        ext_ref[HALO:HALO + ts, :] = u
        first = si * GRID_TILE + r0 + 1
        pos = (first + lax.broadcasted_iota(jnp.int32, (ts, 1), 0)).astype(jnp.float32)
        mixed = []
        for gi, w in enumerate(POOL_WINDOWS):
            cols = slice(gi * POOL_GROUP_DIM, (gi + 1) * POOL_GROUP_DIM)
            acc = ext_ref[:, cols]
            step = 1
            while step < w:
                acc = acc + _roll_rows(acc, step)
                step *= 2
            inv_count = 1.0 / jnp.minimum(pos, float(w))
            pooled = acc[HALO:, :] * inv_count - u[:, cols]
            mixed.append(_bdot(pooled, poolw_ref[gi]))
        ext_ref[0:HALO, :] = u[ts - HALO:, :]
        pooled = jnp.concatenate(mixed, axis=-1) * pscale_ref[...]
        branch_a = _bdot(pooled * _silu(proj(SEC_PG)), wpo_ref[...])

        isub = lax.broadcasted_iota(jnp.int32, (SUBLANES, HEAD_DIM), 0)
        cat = functools.partial(jnp.concatenate, axis=0)

        def tile_rows(ci):
            return [pl.ds(ci * CHUNK + GROUP_TOKENS * (t // TILES_PER_GROUP) + t % TILES_PER_GROUP, SUBLANES,
                          stride=ROW_STRIDE) for t in range(TILES)]

        def scan_chunk(ci):
            rows_of = tile_rows(ci)
            for hd in range(HEADS):
                q, f, v = ([ref[hd, rows, :] for rows in rows_of] for ref in (q_ref, f_ref, v_ref))
                lhs, rhs, total, local = _scan_operands(q, f, v, isub)
                for i in range(N_OPS):
                    lhs_ref[ci, hd, i] = cat(lhs[i]).astype(jnp.bfloat16)
                    rhs_ref[ci, hd, i] = cat(rhs[i]).astype(jnp.bfloat16)
                vb_ref[ci, hd] = cat(v).astype(jnp.bfloat16)
                tot_ref[ci, hd] = _last_row(total)
                loc_ref[ci, hd] = cat(local)

        r2 = _token_of_row(lax.broadcasted_iota(jnp.int32, (CHUNK, CHUNK), 0))
        c2 = _token_of_row(lax.broadcasted_iota(jnp.int32, (CHUNK, CHUNK), 1))
        blk_eq = [(r2 >> (bit + 1)) == (c2 >> (bit + 1)) for bit in range(TIME_BITS)]
        hng = hng_ref[...]

        def output_chunk(ci):
            rows_of = tile_rows(ci)
            heads = range(HEADS)

            def nt(hd, i):
                return lax.dot_general(lhs_ref[ci, hd, i], rhs_ref[ci, hd, i], NT,
                                       preferred_element_type=jnp.float32)

            scores = [None] * HEADS
            for bit in range(LOCAL_BITS, TIME_BITS):
                for hd in heads:
                    p = nt(hd, bit - LOCAL_BITS)
                    if bit < TIME_BITS - 1:
                        p = jnp.where(blk_eq[bit], p, 0.0)
                    scores[hd] = p if scores[hd] is None else scores[hd] + p
            state_t = [state_ref[hd] for hd in heads]
            o_inter = [lax.dot_general(lhs_ref[ci, hd, N_OPS - 1], state_t[hd].astype(jnp.bfloat16), NT,
                                       preferred_element_type=jnp.float32) for hd in heads]
            for hd in heads:
                state_ref[hd] = (state_t[hd] * tot_ref[ci, hd][0:1, :]
                                 + lax.dot_general(vb_ref[ci, hd], rhs_ref[ci, hd, N_OPS - 1], TN,
                                                   preferred_element_type=jnp.float32))
            for hd in heads:
                o = (o_inter[hd] + loc_ref[ci, hd]
                     + jnp.dot(scores[hd].astype(jnp.bfloat16), vb_ref[ci, hd],
                               preferred_element_type=jnp.float32))
                for t, rows in enumerate(rows_of):
                    o_tile = o[SUBLANES * t:SUBLANES * (t + 1), :]
                    ob_ref[hd, rows, :] = (_rms(o_tile) * hng) * g_ref[hd, rows, :]

        for ci in range(ts // CHUNK):
            scan_chunk(ci)
        for ci in range(ts // CHUNK):
            output_chunk(ci)
        gated = jnp.concatenate([ob_ref[hd] for hd in range(HEADS)], axis=-1)
        branch_b = _bdot(gated, who_ref[...])

        merged = _sigmoid(proj(SEC_MGP)) * branch_a + _sigmoid(proj(SEC_MGH)) * branch_b
        y = _bdot(merged, wout_ref[...])
        o_ref[pl.ds(r0, SEQ_TILE), :] = x + _rms(y) * (gate * gpost_ref[...])
        return carry

    lax.fori_loop(0, GRID_TILE // SEQ_TILE, tile_pass, 0)


def _resident(shape, index_map):
    return pl.BlockSpec(shape, index_map, pipeline_mode=pl.Buffered(1))


def _nbytes(shape, dtype):
    return math.prod(shape) * jnp.dtype(dtype).itemsize


def _layer_call(layer, x, ada, g_pre, g_post, lb_logits, pool_scale, hgrn_norm_g,
                w_in, pool_w, w_pool_o, w_hgrn_o, w_out):
    batch, seq, d = x.shape
    depth = lb_logits.shape[0]
    ts = SEQ_TILE
    assert seq % GRID_TILE == 0 and GRID_TILE % ts == 0 and ts % CHUNK == 0 and d == D_MODEL
    chunks = ts // CHUNK

    def const2(b, s):
        return (0, 0)

    def per_layer3(b, s):
        return (layer, 0, 0)

    in_specs = [
        pl.BlockSpec((None, GRID_TILE, d), lambda b, s: (b, s, 0)),
        pl.BlockSpec((None, 3, d), lambda b, s: (b, 0, 0)),
        _resident((1, d), const2),
        _resident((1, d), const2),
        _resident((depth, HGRN_WIDTH), const2),
        _resident((1, POOL_WIDTH), const2),
        _resident((1, HEAD_DIM), const2),
        _resident((None, d, sum(IN_SIZES)), per_layer3),
        _resident((None, POOL_GROUPS, POOL_GROUP_DIM, POOL_GROUP_DIM), lambda b, s: (layer, 0, 0, 0)),
        _resident((None, POOL_WIDTH, d), per_layer3),
        _resident((None, HGRN_WIDTH, d), per_layer3),
        _resident((None, d, d), per_layer3),
    ]
    scratch = [
        pltpu.VMEM((HEADS, ts, HEAD_DIM), jnp.float32),
        pltpu.VMEM((HEADS, ts, HEAD_DIM), jnp.float32),
        pltpu.VMEM((HEADS, ts, HEAD_DIM), jnp.float32),
        pltpu.VMEM((HEADS, ts, HEAD_DIM), jnp.float32),
        pltpu.VMEM((HEADS, ts, HEAD_DIM), jnp.float32),
        pltpu.VMEM((HALO + ts, POOL_WIDTH), jnp.float32),
        pltpu.VMEM((HEADS, HEAD_DIM, HEAD_DIM), jnp.float32),
        pltpu.VMEM((chunks, HEADS, N_OPS, CHUNK, HEAD_DIM), jnp.bfloat16),
        pltpu.VMEM((chunks, HEADS, N_OPS, CHUNK, HEAD_DIM), jnp.bfloat16),
        pltpu.VMEM((chunks, HEADS, CHUNK, HEAD_DIM), jnp.bfloat16),
        pltpu.VMEM((chunks, HEADS, SUBLANES, HEAD_DIM), jnp.float32),
        pltpu.VMEM((chunks, HEADS, CHUNK, HEAD_DIM), jnp.float32),
    ]
    weights = (w_in, pool_w, w_pool_o, w_hgrn_o, w_out)
    vmem_bytes = (sum(_nbytes(w.shape[1:], w.dtype) for w in weights)
                  + 2 * 2 * _nbytes((GRID_TILE, d), x.dtype)
                  + sum(_nbytes(s.shape, s.dtype) for s in scratch)
                  + COMPILER_TEMP_BYTES)
    assert vmem_bytes <= V7X_VMEM_BYTES, vmem_bytes
    return pl.pallas_call(
        functools.partial(_layer_kernel, layer=layer),
        grid=(batch, seq // GRID_TILE),
        in_specs=in_specs,
        out_specs=pl.BlockSpec((None, GRID_TILE, d), lambda b, s: (b, s, 0)),
        out_shape=jax.ShapeDtypeStruct(x.shape, x.dtype),
        scratch_shapes=scratch,
        compiler_params=pltpu.CompilerParams(
            dimension_semantics=("arbitrary", "arbitrary"),
            vmem_limit_bytes=vmem_bytes),
        name=f"hybrid_layer{layer}",
    )(x, ada, g_pre[layer][None, :], g_post[layer][None, :], lb_logits,
      pool_scale[layer][None, :], hgrn_norm_g[layer][None, :],
      w_in, pool_w, w_pool_o, w_hgrn_o, w_out)


def _ada_call(c, w_ada, b_ada):
    depth, d, width = w_ada.shape
    batch = c.shape[0]
    assert width % ADA_TILE == 0
    return pl.pallas_call(
        _ada_kernel,
        grid=(depth, width // ADA_TILE),
        in_specs=[
            pl.BlockSpec((batch, d), lambda l, n: (0, 0)),
            pl.BlockSpec((None, d, ADA_TILE), lambda l, n: (l, 0, n)),
            pl.BlockSpec((None, 1, ADA_TILE), lambda l, n: (l, 0, n)),
        ],
        out_specs=pl.BlockSpec((None, batch, ADA_TILE), lambda l, n: (l, 0, n)),
        out_shape=jax.ShapeDtypeStruct((depth, batch, width), jnp.float32),
        name="adaln_vectors",
    )(c, w_ada, b_ada[:, None, :])


def kernel(x, c, w_ada, b_ada, g_pre, g_post, w_in, pool_w, pool_scale, lb_logits,
           hgrn_norm_g, w_pool_o, w_hgrn_o, w_out):
    depth = w_in.shape[0]
    batch, _, d = x.shape
    ada = _ada_call(c, w_ada, b_ada).reshape(depth, batch, 3, d)
    bf = jnp.bfloat16
    w_in_b, pool_w_b = w_in.astype(bf), pool_w.astype(bf)
    w_pool_o_b, w_hgrn_o_b, w_out_b = w_pool_o.astype(bf), w_hgrn_o.astype(bf), w_out.astype(bf)
    for layer in range(depth):
        x = _layer_call(layer, x, ada[layer], g_pre, g_post, lb_logits, pool_scale, hgrn_norm_g,
                        w_in_b, pool_w_b, w_pool_o_b, w_hgrn_o_b, w_out_b)
    return x
```

```python
import functools
import math

import jax
import jax.numpy as jnp
from jax import lax
from jax.experimental import pallas as pl
from jax.experimental.pallas import tpu as pltpu

D_MODEL = 1024
POOL_WINDOWS = (2, 4, 8, 16)
POOL_GROUPS = len(POOL_WINDOWS)
POOL_GROUP_DIM = D_MODEL // 8
POOL_WIDTH = POOL_GROUPS * POOL_GROUP_DIM
HEAD_DIM = 128
HEADS = D_MODEL // HEAD_DIM
HGRN_WIDTH = HEADS * HEAD_DIM
NORM_EPS = 1e-6
F_FLOOR = 1e-30
NEG_LOG2_E = -1.4426950408889634
IN_SIZES = (POOL_WIDTH, POOL_WIDTH, HGRN_WIDTH, HGRN_WIDTH, HGRN_WIDTH, HGRN_WIDTH, D_MODEL, D_MODEL)
IN_OFFSETS = tuple(sum(IN_SIZES[:i]) for i in range(len(IN_SIZES)))
(SEC_PV, SEC_PG, SEC_HQ, SEC_HF, SEC_HI, SEC_HG, SEC_MGP, SEC_MGH) = range(len(IN_SIZES))

GRID_TILE = 512
SEQ_TILE = 256
CHUNK = 64
HALO = max(POOL_WINDOWS)
SUBLANES = 8
TILES = CHUNK // SUBLANES
TILE_BITS = (0, 1, 5)
SUBLANE_BITS = (2, 3, 4)
TIME_BITS = 6
TILES_PER_GROUP = 4
ROW_STRIDE = TILES_PER_GROUP
GROUP_TOKENS = TILES_PER_GROUP * SUBLANES
assert CHUNK == 1 << TIME_BITS
LOCAL_BITS = 2
assert TILE_BITS[:LOCAL_BITS] == tuple(range(LOCAL_BITS)) and TILES_PER_GROUP == 1 << LOCAL_BITS
N_OPS = TIME_BITS - LOCAL_BITS + 1
ADA_TILE = 1024
V7X_VMEM_BYTES = 64 * 1024 * 1024
COMPILER_TEMP_BYTES = 3 * SEQ_TILE * D_MODEL * 4 * 4

NT = (((1,), (1,)), ((), ()))
TN = (((0,), (0,)), ((), ()))


def _bdot(a, b):
    return jnp.dot(a.astype(jnp.bfloat16), b.astype(jnp.bfloat16),
                   preferred_element_type=jnp.float32)


def _sigmoid(z):
    return 1.0 / (1.0 + jnp.exp2(z * NEG_LOG2_E))


def _silu(z):
    return z * _sigmoid(z)


def _rms(x):
    return x * lax.rsqrt(jnp.mean(x * x, axis=-1, keepdims=True) + NORM_EPS)


def _roll_rows(a, shift):
    return pltpu.roll(a, shift % a.shape[0], axis=0)


def _token_of_row(p):
    group, j, i = p // GROUP_TOKENS, (p // SUBLANES) % TILES_PER_GROUP, p % SUBLANES
    return group * GROUP_TOKENS + i * ROW_STRIDE + j


def _ada_kernel(c_ref, w_ref, b_ref, o_ref):
    c = c_ref[...]
    o_ref[...] = jnp.dot(_silu(c), w_ref[...], preferred_element_type=jnp.float32) + b_ref[...]


def _sublane_partner(p, sb, isub):
    k = 1
    while k < sb:
        p = jnp.where((isub & k) != 0, p, _roll_rows(p, -k))
        k *= 2
    if 2 * sb == p.shape[0]:
        return _roll_rows(p, sb)
    return jnp.where((isub & sb) != 0, _roll_rows(p, sb), _roll_rows(p, -sb))


def _last_row(a):
    return jnp.broadcast_to(a[SUBLANES - 1:SUBLANES, :], a.shape)


def _lane_sum(a):
    return jnp.sum(a, axis=-1, keepdims=True)


def _local_outputs(q, pf, kk, v):
    out = []
    for base in range(0, TILES, TILES_PER_GROUP):
        qs, fs, ks, vs = (a[base:base + TILES_PER_GROUP] for a in (q, pf, kk, v))
        acc = [_lane_sum(qs[j] * ks[j]) * vs[j] for j in range(TILES_PER_GROUP)]
        decayed_k = {}
        for j_t in range(1, TILES_PER_GROUP):
            qf = qs[j_t] * fs[j_t]
            for j_s in range(j_t - 1, -1, -1):
                decayed_k[j_t, j_s] = (ks[j_s] if j_s == j_t - 1
                                       else decayed_k[j_t - 1, j_s] * fs[j_t - 1])
                acc[j_t] = acc[j_t] + _lane_sum(qf * decayed_k[j_t, j_s]) * vs[j_s]
        out.extend(acc)
    return out


def _scan_operands(q, pf, kk, v, isub):
    tiles = range(TILES)
    pre = list(pf)
    ks = list(kk)
    local = _local_outputs(q, pre, kk, v)
    zero = jnp.zeros_like(q[0])
    lhs, rhs = [], []
    for bit in range(TIME_BITS):
        if bit in TILE_BITS:
            tb = 1 << TILE_BITS.index(bit)
            if bit >= LOCAL_BITS:
                lhs.append([q[t] * pre[t] if t & tb else zero for t in tiles])
                rhs.append([zero if t & tb else ks[t] for t in tiles])
        else:
            sb = 1 << SUBLANE_BITS.index(bit)
            upper = (isub & sb) != 0
            lhs.append([jnp.where(upper, q[t] * pre[t], 0.0) for t in tiles])
            rhs.append([jnp.where(upper, 0.0, ks[t]) for t in tiles])
        new_pre, new_ks = list(pre), list(ks)
        if bit in TILE_BITS:
            totals = {}
            for t in tiles:
                src = (t ^ tb) | (tb - 1)
                if src not in totals:
                    totals[src] = _last_row(pre[src]) if bit > max(SUBLANE_BITS) else pre[src]
                if t & tb:
                    new_pre[t] = pre[t] * totals[src]
                else:
                    new_ks[t] = ks[t] * totals[src]
        else:
            for base in range(0, TILES, TILES_PER_GROUP):
                partner = _sublane_partner(pre[base + TILES_PER_GROUP - 1], sb, isub)
                pre_mul = jnp.where(upper, partner, 1.0)
                ks_mul = jnp.where(upper, 1.0, partner)
                for t in range(base, base + TILES_PER_GROUP):
                    new_pre[t] = pre[t] * pre_mul
                    new_ks[t] = ks[t] * ks_mul
        pre, ks = new_pre, new_ks
    lhs.append([q[t] * pre[t] for t in tiles])
    rhs.append(ks)
    return lhs, rhs, pre[TILES - 1], local


def _layer_kernel(x_ref, ada_ref, gpre_ref, gpost_ref, lbl_ref, pscale_ref, hng_ref,
                  win_ref, poolw_ref, wpo_ref, who_ref, wout_ref, o_ref,
                  q_ref, pf_ref, kk_ref, v_ref, g_ref, ob_ref, ext_ref, state_ref,
                  lhs_ref, rhs_ref, vb_ref, tot_ref, loc_ref, *, layer):
    ts = SEQ_TILE
    si = pl.program_id(1)

    @pl.when(si == 0)
    def _():
        state_ref[...] = jnp.zeros_like(state_ref)
        ext_ref[0:HALO, :] = jnp.zeros((HALO, POOL_WIDTH), jnp.float32)

    def tile_pass(part, carry):
        r0 = pl.multiple_of(part * SEQ_TILE, SEQ_TILE)
        x = x_ref[pl.ds(r0, SEQ_TILE), :]
        shift = ada_ref[0:1, :]
        scale = ada_ref[1:2, :]
        gate = ada_ref[2:3, :]
        hb = (_rms(x) * (gpre_ref[...] * (1.0 + scale)) + shift).astype(jnp.bfloat16)

        def proj(section):
            cols = slice(IN_OFFSETS[section], IN_OFFSETS[section] + IN_SIZES[section])
            return jnp.dot(hb, win_ref[:, cols], preferred_element_type=jnp.float32)

        logits = lbl_ref[...]
        e = jnp.exp(logits - jnp.max(logits, axis=0, keepdims=True))
        lb = jnp.zeros((1, HGRN_WIDTH), jnp.float32)
        for j in range(1, layer + 1):
            lb = lb + e[j:j + 1, :]
        lb = jnp.clip(lb / jnp.sum(e, axis=0, keepdims=True), 0.0, 1.0)

        def to_heads(ref, a):
            for hd in range(HEADS):
                ref[hd] = a[:, hd * HEAD_DIM:(hd + 1) * HEAD_DIM]

        to_heads(q_ref, _silu(proj(SEC_HQ)))
        forget = lb + (1.0 - lb) * _sigmoid(proj(SEC_HF))
        to_heads(pf_ref, jnp.maximum(forget, F_FLOOR))
        to_heads(kk_ref, 1.0 - forget)
        to_heads(v_ref, proj(SEC_HI))
        to_heads(g_ref, _silu(proj(SEC_HG)))

        u = proj(SEC_PV)
        ext_ref[HALO:HALO + ts, :] = u
        first = si * GRID_TILE + r0 + 1
        pos = (first + lax.broadcasted_iota(jnp.int32, (ts, 1), 0)).astype(jnp.float32)
        mixed = []
        for gi, w in enumerate(POOL_WINDOWS):
            cols = slice(gi * POOL_GROUP_DIM, (gi + 1) * POOL_GROUP_DIM)
            acc = ext_ref[:, cols]
            step = 1
            while step < w:
                acc = acc + _roll_rows(acc, step)
                step *= 2
            inv_count = 1.0 / jnp.minimum(pos, float(w))
            pooled = acc[HALO:, :] * inv_count - u[:, cols]
            mixed.append(_bdot(pooled, poolw_ref[gi]))
        ext_ref[0:HALO, :] = u[ts - HALO:, :]
        pooled = jnp.concatenate(mixed, axis=-1) * pscale_ref[...]
        branch_a = _bdot(pooled * _silu(proj(SEC_PG)), wpo_ref[...])

        isub = lax.broadcasted_iota(jnp.int32, (SUBLANES, HEAD_DIM), 0)
        cat = functools.partial(jnp.concatenate, axis=0)

        def tile_rows(ci):
            return [pl.ds(ci * CHUNK + GROUP_TOKENS * (t // TILES_PER_GROUP) + t % TILES_PER_GROUP, SUBLANES,
                          stride=ROW_STRIDE) for t in range(TILES)]

        def scan_chunk(ci):
            rows_of = tile_rows(ci)
            for hd in range(HEADS):
                q, pf, kk, v = ([ref[hd, rows, :] for rows in rows_of]
                                for ref in (q_ref, pf_ref, kk_ref, v_ref))
                lhs, rhs, total, local = _scan_operands(q, pf, kk, v, isub)
                for i in range(N_OPS):
                    lhs_ref[ci, hd, i] = cat(lhs[i]).astype(jnp.bfloat16)
                    rhs_ref[ci, hd, i] = cat(rhs[i]).astype(jnp.bfloat16)
                vb_ref[ci, hd] = cat(v).astype(jnp.bfloat16)
                tot_ref[ci, hd] = _last_row(total)
                loc_ref[ci, hd] = cat(local)

        r2 = _token_of_row(lax.broadcasted_iota(jnp.int32, (CHUNK, CHUNK), 0))
        c2 = _token_of_row(lax.broadcasted_iota(jnp.int32, (CHUNK, CHUNK), 1))
        blk_eq = [(r2 >> (bit + 1)) == (c2 >> (bit + 1)) for bit in range(TIME_BITS)]
        hng = hng_ref[...]

        def output_chunk(ci):
            rows_of = tile_rows(ci)
            heads = range(HEADS)

            def nt(hd, i):
                return lax.dot_general(lhs_ref[ci, hd, i], rhs_ref[ci, hd, i], NT,
                                       preferred_element_type=jnp.float32)

            scores = [None] * HEADS
            for bit in range(LOCAL_BITS, TIME_BITS):
                for hd in heads:
                    p = nt(hd, bit - LOCAL_BITS)
                    if bit < TIME_BITS - 1:
                        p = jnp.where(blk_eq[bit], p, 0.0)
                    scores[hd] = p if scores[hd] is None else scores[hd] + p
            state_t = [state_ref[hd] for hd in heads]
            o_inter = [lax.dot_general(lhs_ref[ci, hd, N_OPS - 1], state_t[hd].astype(jnp.bfloat16), NT,
                                       preferred_element_type=jnp.float32) for hd in heads]
            for hd in heads:
                state_ref[hd] = (state_t[hd] * tot_ref[ci, hd][0:1, :]
                                 + lax.dot_general(vb_ref[ci, hd], rhs_ref[ci, hd, N_OPS - 1], TN,
                                                   preferred_element_type=jnp.float32))
            for hd in heads:
                o = (o_inter[hd] + loc_ref[ci, hd]
                     + jnp.dot(scores[hd].astype(jnp.bfloat16), vb_ref[ci, hd],
                               preferred_element_type=jnp.float32))
                for t, rows in enumerate(rows_of):
                    o_tile = o[SUBLANES * t:SUBLANES * (t + 1), :]
                    ob_ref[hd, rows, :] = (_rms(o_tile) * hng) * g_ref[hd, rows, :]

        for ci in range(ts // CHUNK):
            scan_chunk(ci)
        for ci in range(ts // CHUNK):
            output_chunk(ci)
        gated = jnp.concatenate([ob_ref[hd] for hd in range(HEADS)], axis=-1)
        branch_b = _bdot(gated, who_ref[...])

        merged = _sigmoid(proj(SEC_MGP)) * branch_a + _sigmoid(proj(SEC_MGH)) * branch_b
        y = _bdot(merged, wout_ref[...])
        o_ref[pl.ds(r0, SEQ_TILE), :] = x + _rms(y) * (gate * gpost_ref[...])
        return carry

    lax.fori_loop(0, GRID_TILE // SEQ_TILE, tile_pass, 0)


def _resident(shape, index_map):
    return pl.BlockSpec(shape, index_map, pipeline_mode=pl.Buffered(1))


def _nbytes(shape, dtype):
    return math.prod(shape) * jnp.dtype(dtype).itemsize


def _layer_call(layer, x, ada, g_pre, g_post, lb_logits, pool_scale, hgrn_norm_g,
                w_in, pool_w, w_pool_o, w_hgrn_o, w_out):
    batch, seq, d = x.shape
    depth = lb_logits.shape[0]
    ts = SEQ_TILE
    assert seq % GRID_TILE == 0 and GRID_TILE % ts == 0 and ts % CHUNK == 0 and d == D_MODEL
    chunks = ts // CHUNK

    def const2(b, s):
        return (0, 0)

    def per_layer3(b, s):
        return (layer, 0, 0)

    in_specs = [
        pl.BlockSpec((None, GRID_TILE, d), lambda b, s: (b, s, 0)),
        pl.BlockSpec((None, 3, d), lambda b, s: (b, 0, 0)),
        _resident((1, d), const2),
        _resident((1, d), const2),
        _resident((depth, HGRN_WIDTH), const2),
        _resident((1, POOL_WIDTH), const2),
        _resident((1, HEAD_DIM), const2),
        _resident((None, d, sum(IN_SIZES)), per_layer3),
        _resident((None, POOL_GROUPS, POOL_GROUP_DIM, POOL_GROUP_DIM), lambda b, s: (layer, 0, 0, 0)),
        _resident((None, POOL_WIDTH, d), per_layer3),
        _resident((None, HGRN_WIDTH, d), per_layer3),
        _resident((None, d, d), per_layer3),
    ]
    scratch = [
        pltpu.VMEM((HEADS, ts, HEAD_DIM), jnp.float32),
        pltpu.VMEM((HEADS, ts, HEAD_DIM), jnp.float32),
        pltpu.VMEM((HEADS, ts, HEAD_DIM), jnp.float32),
        pltpu.VMEM((HEADS, ts, HEAD_DIM), jnp.float32),
        pltpu.VMEM((HEADS, ts, HEAD_DIM), jnp.float32),
        pltpu.VMEM((HEADS, ts, HEAD_DIM), jnp.float32),
        pltpu.VMEM((HALO + ts, POOL_WIDTH), jnp.float32),
        pltpu.VMEM((HEADS, HEAD_DIM, HEAD_DIM), jnp.float32),
        pltpu.VMEM((chunks, HEADS, N_OPS, CHUNK, HEAD_DIM), jnp.bfloat16),
        pltpu.VMEM((chunks, HEADS, N_OPS, CHUNK, HEAD_DIM), jnp.bfloat16),
        pltpu.VMEM((chunks, HEADS, CHUNK, HEAD_DIM), jnp.bfloat16),
        pltpu.VMEM((chunks, HEADS, SUBLANES, HEAD_DIM), jnp.float32),
        pltpu.VMEM((chunks, HEADS, CHUNK, HEAD_DIM), jnp.float32),
    ]
    weights = (w_in, pool_w, w_pool_o, w_hgrn_o, w_out)
    vmem_bytes = (sum(_nbytes(w.shape[1:], w.dtype) for w in weights)
                  + 2 * 2 * _nbytes((GRID_TILE, d), x.dtype)
                  + sum(_nbytes(s.shape, s.dtype) for s in scratch)
                  + COMPILER_TEMP_BYTES)
    assert vmem_bytes <= V7X_VMEM_BYTES, vmem_bytes
    return pl.pallas_call(
        functools.partial(_layer_kernel, layer=layer),
        grid=(batch, seq // GRID_TILE),
        in_specs=in_specs,
        out_specs=pl.BlockSpec((None, GRID_TILE, d), lambda b, s: (b, s, 0)),
        out_shape=jax.ShapeDtypeStruct(x.shape, x.dtype),
        scratch_shapes=scratch,
        compiler_params=pltpu.CompilerParams(
            dimension_semantics=("arbitrary", "arbitrary"),
            vmem_limit_bytes=vmem_bytes),
        name=f"hybrid_layer{layer}",
    )(x, ada, g_pre[layer][None, :], g_post[layer][None, :], lb_logits,
      pool_scale[layer][None, :], hgrn_norm_g[layer][None, :],
      w_in, pool_w, w_pool_o, w_hgrn_o, w_out)


def _ada_call(c, w_ada, b_ada):
    depth, d, width = w_ada.shape
    batch = c.shape[0]
    assert width % ADA_TILE == 0
    return pl.pallas_call(
        _ada_kernel,
        grid=(depth, width // ADA_TILE),
        in_specs=[
            pl.BlockSpec((batch, d), lambda l, n: (0, 0)),
            pl.BlockSpec((None, d, ADA_TILE), lambda l, n: (l, 0, n)),
            pl.BlockSpec((None, 1, ADA_TILE), lambda l, n: (l, 0, n)),
        ],
        out_specs=pl.BlockSpec((None, batch, ADA_TILE), lambda l, n: (l, 0, n)),
        out_shape=jax.ShapeDtypeStruct((depth, batch, width), jnp.float32),
        name="adaln_vectors",
    )(c, w_ada, b_ada[:, None, :])


def kernel(x, c, w_ada, b_ada, g_pre, g_post, w_in, pool_w, pool_scale, lb_logits,
           hgrn_norm_g, w_pool_o, w_hgrn_o, w_out):
    depth = w_in.shape[0]
    batch, _, d = x.shape
    ada = _ada_call(c, w_ada, b_ada).reshape(depth, batch, 3, d)
    bf = jnp.bfloat16
    w_in_b, pool_w_b = w_in.astype(bf), pool_w.astype(bf)
    w_pool_o_b, w_hgrn_o_b, w_out_b = w_pool_o.astype(bf), w_hgrn_o.astype(bf), w_out.astype(bf)
    for layer in range(depth):
        x = _layer_call(layer, x, ada[layer], g_pre, g_post, lb_logits, pool_scale, hgrn_norm_g,
                        w_in_b, pool_w_b, w_pool_o_b, w_hgrn_o_b, w_out_b)
    return x
```

```python
import functools
import math

import jax
import jax.numpy as jnp
from jax import lax
from jax.experimental import pallas as pl
from jax.experimental.pallas import tpu as pltpu

D_MODEL = 1024
POOL_WINDOWS = (2, 4, 8, 16)
POOL_GROUPS = len(POOL_WINDOWS)
POOL_GROUP_DIM = D_MODEL // 8
POOL_WIDTH = POOL_GROUPS * POOL_GROUP_DIM
HEAD_DIM = 128
HEADS = D_MODEL // HEAD_DIM
HGRN_WIDTH = HEADS * HEAD_DIM
NORM_EPS = 1e-6
F_FLOOR = 1e-30
NEG_LOG2_E = -1.4426950408889634
IN_SIZES = (POOL_WIDTH, POOL_WIDTH, HGRN_WIDTH, HGRN_WIDTH, HGRN_WIDTH, HGRN_WIDTH, D_MODEL, D_MODEL)
IN_OFFSETS = tuple(sum(IN_SIZES[:i]) for i in range(len(IN_SIZES)))
(SEC_PV, SEC_PG, SEC_HQ, SEC_HF, SEC_HI, SEC_HG, SEC_MGP, SEC_MGH) = range(len(IN_SIZES))

GRID_TILE = 512
SEQ_TILE = 256
CHUNK = 64
HALO = max(POOL_WINDOWS)
SUBLANES = 8
TILES = CHUNK // SUBLANES
TILE_BITS = (0, 1, 5)
SUBLANE_BITS = (2, 3, 4)
TIME_BITS = 6
TILES_PER_GROUP = 4
ROW_STRIDE = TILES_PER_GROUP
GROUP_TOKENS = TILES_PER_GROUP * SUBLANES
assert CHUNK == 1 << TIME_BITS
LOCAL_BITS = 2
assert TILE_BITS[:LOCAL_BITS] == tuple(range(LOCAL_BITS)) and TILES_PER_GROUP == 1 << LOCAL_BITS
N_OPS = TIME_BITS - LOCAL_BITS + 1
ADA_TILE = 1024
V7X_VMEM_BYTES = 64 * 1024 * 1024
COMPILER_TEMP_BYTES = 3 * SEQ_TILE * D_MODEL * 4 * 4

NT = (((1,), (1,)), ((), ()))
TN = (((0,), (0,)), ((), ()))


def _bdot(a, b):
    return jnp.dot(a.astype(jnp.bfloat16), b.astype(jnp.bfloat16),
                   preferred_element_type=jnp.float32)


def _sigmoid(z):
    return 1.0 / (1.0 + jnp.exp2(z * NEG_LOG2_E))


def _silu(z):
    return z * _sigmoid(z)


def _rms(x):
    return x * lax.rsqrt(jnp.mean(x * x, axis=-1, keepdims=True) + NORM_EPS)


def _roll_rows(a, shift):
    return pltpu.roll(a, shift % a.shape[0], axis=0)


def _token_of_row(p):
    group, j, i = p // GROUP_TOKENS, (p // SUBLANES) % TILES_PER_GROUP, p % SUBLANES
    return group * GROUP_TOKENS + i * ROW_STRIDE + j


def _ada_kernel(c_ref, w_ref, b_ref, o_ref):
    c = c_ref[...]
    o_ref[...] = jnp.dot(_silu(c), w_ref[...], preferred_element_type=jnp.float32) + b_ref[...]


def _sublane_partner(p, sb, isub):
    k = 1
    while k < sb:
        p = jnp.where((isub & k) != 0, p, _roll_rows(p, -k))
        k *= 2
    if 2 * sb == p.shape[0]:
        return _roll_rows(p, sb)
    return jnp.where((isub & sb) != 0, _roll_rows(p, sb), _roll_rows(p, -sb))


def _last_row(a):
    return jnp.broadcast_to(a[SUBLANES - 1:SUBLANES, :], a.shape)


def _lane_sum(a):
    return jnp.sum(a, axis=-1, keepdims=True)


def _local_outputs(q, pf, kk, v):
    out = []
    for base in range(0, TILES, TILES_PER_GROUP):
        qs, fs, ks, vs = (a[base:base + TILES_PER_GROUP] for a in (q, pf, kk, v))
        acc = [_lane_sum(qs[j] * ks[j]) * vs[j] for j in range(TILES_PER_GROUP)]
        decayed_k = {}
        for j_t in range(1, TILES_PER_GROUP):
            qf = qs[j_t] * fs[j_t]
            for j_s in range(j_t - 1, -1, -1):
                decayed_k[j_t, j_s] = (ks[j_s] if j_s == j_t - 1
                                       else decayed_k[j_t - 1, j_s] * fs[j_t - 1])
                acc[j_t] = acc[j_t] + _lane_sum(qf * decayed_k[j_t, j_s]) * vs[j_s]
        out.extend(acc)
    return out


def _scan_operands(q, f, v, isub):
    tiles = range(TILES)
    kk = [1.0 - f[t] for t in tiles]
    pre = [jnp.maximum(f[t], F_FLOOR) for t in tiles]
    ks = list(kk)
    local = _local_outputs(q, pre, kk, v)
    zero = jnp.zeros_like(q[0])
    lhs, rhs = [], []
    for bit in range(TIME_BITS):
        if bit in TILE_BITS:
            tb = 1 << TILE_BITS.index(bit)
            if bit >= LOCAL_BITS:
                lhs.append([q[t] * pre[t] if t & tb else zero for t in tiles])
                rhs.append([zero if t & tb else ks[t] for t in tiles])
        else:
            sb = 1 << SUBLANE_BITS.index(bit)
            upper = (isub & sb) != 0
            lhs.append([jnp.where(upper, q[t] * pre[t], 0.0) for t in tiles])
            rhs.append([jnp.where(upper, 0.0, ks[t]) for t in tiles])
        new_pre, new_ks = list(pre), list(ks)
        if bit in TILE_BITS:
            totals = {}
            for t in tiles:
                src = (t ^ tb) | (tb - 1)
                if src not in totals:
                    totals[src] = _last_row(pre[src]) if bit > max(SUBLANE_BITS) else pre[src]
                if t & tb:
                    new_pre[t] = pre[t] * totals[src]
                else:
                    new_ks[t] = ks[t] * totals[src]
        else:
            for base in range(0, TILES, TILES_PER_GROUP):
                partner = _sublane_partner(pre[base + TILES_PER_GROUP - 1], sb, isub)
                pre_mul = jnp.where(upper, partner, 1.0)
                ks_mul = jnp.where(upper, 1.0, partner)
                for t in range(base, base + TILES_PER_GROUP):
                    new_pre[t] = pre[t] * pre_mul
                    new_ks[t] = ks[t] * ks_mul
        pre, ks = new_pre, new_ks
    lhs.append([q[t] * pre[t] for t in tiles])
    rhs.append(ks)
    return lhs, rhs, pre[TILES - 1], local


def _layer_kernel(x_ref, ada_ref, gpre_ref, gpost_ref, lbl_ref, pscale_ref, hng_ref,
                  win_ref, poolw_ref, wpo_ref, who_ref, wout_ref, o_ref,
                  q_ref, f_ref, v_ref, g_ref, ob_ref, ext_ref, state_ref,
                  lhs_ref, rhs_ref, vb_ref, tot_ref, loc_ref, *, layer):
    ts = SEQ_TILE
    si = pl.program_id(1)

    @pl.when(si == 0)
    def _():
        state_ref[...] = jnp.zeros_like(state_ref)
        ext_ref[0:HALO, :] = jnp.zeros((HALO, POOL_WIDTH), jnp.float32)

    def tile_pass(part, carry):
        r0 = pl.multiple_of(part * SEQ_TILE, SEQ_TILE)
        x = x_ref[pl.ds(r0, SEQ_TILE), :]
        shift = ada_ref[0:1, :]
        scale = ada_ref[1:2, :]
        gate = ada_ref[2:3, :]
        hb = (_rms(x) * (gpre_ref[...] * (1.0 + scale)) + shift).astype(jnp.bfloat16)

        def proj(section):
            cols = slice(IN_OFFSETS[section], IN_OFFSETS[section] + IN_SIZES[section])
            return jnp.dot(hb, win_ref[:, cols], preferred_element_type=jnp.float32)

        logits = lbl_ref[...]
        e = jnp.exp(logits - jnp.max(logits, axis=0, keepdims=True))
        lb = jnp.zeros((1, HGRN_WIDTH), jnp.float32)
        for j in range(1, layer + 1):
            lb = lb + e[j:j + 1, :]
        lb = jnp.clip(lb / jnp.sum(e, axis=0, keepdims=True), 0.0, 1.0)

        def to_heads(ref, a):
            for hd in range(HEADS):
                ref[hd] = a[:, hd * HEAD_DIM:(hd + 1) * HEAD_DIM]

        to_heads(q_ref, _silu(proj(SEC_HQ)))
        to_heads(f_ref, lb + (1.0 - lb) * _sigmoid(proj(SEC_HF)))
        to_heads(v_ref, proj(SEC_HI))
        to_heads(g_ref, _silu(proj(SEC_HG)))

        u = proj(SEC_PV)
        ext_ref[HALO:HALO + ts, :] = u
        first = si * GRID_TILE + r0 + 1
        pos = (first + lax.broadcasted_iota(jnp.int32, (ts, 1), 0)).astype(jnp.float32)
        mixed = []
        for gi, w in enumerate(POOL_WINDOWS):
            cols = slice(gi * POOL_GROUP_DIM, (gi + 1) * POOL_GROUP_DIM)
            acc = ext_ref[:, cols]
            step = 1
            while step < w:
                acc = acc + _roll_rows(acc, step)
                step *= 2
            inv_count = 1.0 / jnp.minimum(pos, float(w))
            pooled = acc[HALO:, :] * inv_count - u[:, cols]
            mixed.append(_bdot(pooled, poolw_ref[gi]))
        ext_ref[0:HALO, :] = u[ts - HALO:, :]
        pooled = jnp.concatenate(mixed, axis=-1) * pscale_ref[...]
        branch_a = _bdot(pooled * _silu(proj(SEC_PG)), wpo_ref[...])

        isub = lax.broadcasted_iota(jnp.int32, (SUBLANES, HEAD_DIM), 0)
        cat = functools.partial(jnp.concatenate, axis=0)

        def tile_rows(ci):
            return [pl.ds(ci * CHUNK + GROUP_TOKENS * (t // TILES_PER_GROUP) + t % TILES_PER_GROUP, SUBLANES,
                          stride=ROW_STRIDE) for t in range(TILES)]

        def scan_chunk(ci):
            rows_of = tile_rows(ci)
            for hd in range(HEADS):
                q, f, v = ([ref[hd, rows, :] for rows in rows_of] for ref in (q_ref, f_ref, v_ref))
                lhs, rhs, total, local = _scan_operands(q, f, v, isub)
                for i in range(N_OPS):
                    lhs_ref[ci, hd, i] = cat(lhs[i]).astype(jnp.bfloat16)
                    rhs_ref[ci, hd, i] = cat(rhs[i]).astype(jnp.bfloat16)
                vb_ref[ci, hd] = cat(v).astype(jnp.bfloat16)
                tot_ref[ci, hd] = _last_row(total)
                loc_ref[ci, hd] = cat(local)

        r2 = _token_of_row(lax.broadcasted_iota(jnp.int32, (CHUNK, CHUNK), 0))
        c2 = _token_of_row(lax.broadcasted_iota(jnp.int32, (CHUNK, CHUNK), 1))
        blk_eq = [(r2 >> (bit + 1)) == (c2 >> (bit + 1)) for bit in range(TIME_BITS)]
        hng = hng_ref[...]

        def output_chunk(ci):
            rows_of = tile_rows(ci)
            heads = range(HEADS)

            def nt(hd, i):
                return lax.dot_general(lhs_ref[ci, hd, i], rhs_ref[ci, hd, i], NT,
                                       preferred_element_type=jnp.float32)

            scores = [None] * HEADS
            for bit in range(LOCAL_BITS, TIME_BITS):
                for hd in heads:
                    p = nt(hd, bit - LOCAL_BITS)
                    if bit < TIME_BITS - 1:
                        p = jnp.where(blk_eq[bit], p, 0.0)
                    scores[hd] = p if scores[hd] is None else scores[hd] + p
            state_t = [state_ref[hd] for hd in heads]
            o_inter = [lax.dot_general(lhs_ref[ci, hd, N_OPS - 1], state_t[hd].astype(jnp.bfloat16), NT,
                                       preferred_element_type=jnp.float32) for hd in heads]
            for hd in heads:
                state_ref[hd] = (state_t[hd] * tot_ref[ci, hd][0:1, :]
                                 + lax.dot_general(vb_ref[ci, hd], rhs_ref[ci, hd, N_OPS - 1], TN,
                                                   preferred_element_type=jnp.float32))
            for hd in heads:
                o = (o_inter[hd] + loc_ref[ci, hd]
                     + jnp.dot(scores[hd].astype(jnp.bfloat16), vb_ref[ci, hd],
                               preferred_element_type=jnp.float32))
                for t, rows in enumerate(rows_of):
                    o_tile = o[SUBLANES * t:SUBLANES * (t + 1), :]
                    ob_ref[hd, rows, :] = (_rms(o_tile) * hng) * g_ref[hd, rows, :]

        for ci in range(ts // CHUNK):
            scan_chunk(ci)
        for ci in range(ts // CHUNK):
            output_chunk(ci)
        gated = jnp.concatenate([ob_ref[hd] for hd in range(HEADS)], axis=-1)
        branch_b = _bdot(gated, who_ref[...])

        merged = _sigmoid(proj(SEC_MGP)) * branch_a + _sigmoid(proj(SEC_MGH)) * branch_b
        y = _bdot(merged, wout_ref[...])
        o_ref[pl.ds(r0, SEQ_TILE), :] = x + _rms(y) * (gate * gpost_ref[...])
        return carry

    lax.fori_loop(0, GRID_TILE // SEQ_TILE, tile_pass, 0)


def _resident(shape, index_map):
    return pl.BlockSpec(shape, index_map, pipeline_mode=pl.Buffered(1))


def _nbytes(shape, dtype):
    return math.prod(shape) * jnp.dtype(dtype).itemsize


def _layer_call(layer, x, ada, g_pre, g_post, lb_logits, pool_scale, hgrn_norm_g,
                w_in, pool_w, w_pool_o, w_hgrn_o, w_out):
    batch, seq, d = x.shape
    depth = lb_logits.shape[0]
    ts = SEQ_TILE
    assert seq % GRID_TILE == 0 and GRID_TILE % ts == 0 and ts % CHUNK == 0 and d == D_MODEL
    chunks = ts // CHUNK

    def const2(b, s):
        return (0, 0)

    def per_layer_row(a):
        return a.reshape(depth, 1, a.shape[-1])

    def per_layer3(b, s):
        return (layer, 0, 0)

    in_specs = [
        pl.BlockSpec((None, GRID_TILE, d), lambda b, s: (b, s, 0)),
        pl.BlockSpec((None, None, 3, d), lambda b, s: (layer, b, 0, 0)),
        _resident((None, 1, d), per_layer3),
        _resident((None, 1, d), per_layer3),
        _resident((depth, HGRN_WIDTH), const2),
        _resident((None, 1, POOL_WIDTH), per_layer3),
        _resident((None, 1, HEAD_DIM), per_layer3),
        _resident((None, d, sum(IN_SIZES)), per_layer3),
        _resident((None, POOL_GROUPS, POOL_GROUP_DIM, POOL_GROUP_DIM), lambda b, s: (layer, 0, 0, 0)),
        _resident((None, POOL_WIDTH, d), per_layer3),
        _resident((None, HGRN_WIDTH, d), per_layer3),
        _resident((None, d, d), per_layer3),
    ]
    scratch = [
        pltpu.VMEM((HEADS, ts, HEAD_DIM), jnp.float32),
        pltpu.VMEM((HEADS, ts, HEAD_DIM), jnp.float32),
        pltpu.VMEM((HEADS, ts, HEAD_DIM), jnp.float32),
        pltpu.VMEM((HEADS, ts, HEAD_DIM), jnp.float32),
        pltpu.VMEM((HEADS, ts, HEAD_DIM), jnp.float32),
        pltpu.VMEM((HALO + ts, POOL_WIDTH), jnp.float32),
        pltpu.VMEM((HEADS, HEAD_DIM, HEAD_DIM), jnp.float32),
        pltpu.VMEM((chunks, HEADS, N_OPS, CHUNK, HEAD_DIM), jnp.bfloat16),
        pltpu.VMEM((chunks, HEADS, N_OPS, CHUNK, HEAD_DIM), jnp.bfloat16),
        pltpu.VMEM((chunks, HEADS, CHUNK, HEAD_DIM), jnp.bfloat16),
        pltpu.VMEM((chunks, HEADS, SUBLANES, HEAD_DIM), jnp.float32),
        pltpu.VMEM((chunks, HEADS, CHUNK, HEAD_DIM), jnp.float32),
    ]
    weights = (w_in, pool_w, w_pool_o, w_hgrn_o, w_out)
    vmem_bytes = (sum(_nbytes(w.shape[1:], w.dtype) for w in weights)
                  + 2 * 2 * _nbytes((GRID_TILE, d), x.dtype)
                  + sum(_nbytes(s.shape, s.dtype) for s in scratch)
                  + COMPILER_TEMP_BYTES)
    assert vmem_bytes <= V7X_VMEM_BYTES, vmem_bytes
    return pl.pallas_call(
        functools.partial(_layer_kernel, layer=layer),
        grid=(batch, seq // GRID_TILE),
        in_specs=in_specs,
        out_specs=pl.BlockSpec((None, GRID_TILE, d), lambda b, s: (b, s, 0)),
        out_shape=jax.ShapeDtypeStruct(x.shape, x.dtype),
        scratch_shapes=scratch,
        compiler_params=pltpu.CompilerParams(
            dimension_semantics=("arbitrary", "arbitrary"),
            vmem_limit_bytes=vmem_bytes),
        name=f"hybrid_layer{layer}",
    )(x, ada, per_layer_row(g_pre), per_layer_row(g_post), lb_logits,
      per_layer_row(pool_scale), per_layer_row(hgrn_norm_g),
      w_in, pool_w, w_pool_o, w_hgrn_o, w_out)


def _ada_call(c, w_ada, b_ada):
    depth, d, width = w_ada.shape
    batch = c.shape[0]
    assert width % ADA_TILE == 0
    return pl.pallas_call(
        _ada_kernel,
        grid=(depth, width // ADA_TILE),
        in_specs=[
            pl.BlockSpec((batch, d), lambda l, n: (0, 0)),
            pl.BlockSpec((None, d, ADA_TILE), lambda l, n: (l, 0, n)),
            pl.BlockSpec((None, 1, ADA_TILE), lambda l, n: (l, 0, n)),
        ],
        out_specs=pl.BlockSpec((None, batch, ADA_TILE), lambda l, n: (l, 0, n)),
        out_shape=jax.ShapeDtypeStruct((depth, batch, width), jnp.float32),
        name="adaln_vectors",
    )(c, w_ada, b_ada.reshape(depth, 1, width))


def kernel(x, c, w_ada, b_ada, g_pre, g_post, w_in, pool_w, pool_scale, lb_logits,
           hgrn_norm_g, w_pool_o, w_hgrn_o, w_out):
    depth = w_in.shape[0]
    batch, _, d = x.shape
    ada = _ada_call(c, w_ada, b_ada).reshape(depth, batch, 3, d)
    bf = jnp.bfloat16
    w_in_b, pool_w_b = w_in.astype(bf), pool_w.astype(bf)
    w_pool_o_b, w_hgrn_o_b, w_out_b = w_pool_o.astype(bf), w_hgrn_o.astype(bf), w_out.astype(bf)
    for layer in range(depth):
        x = _layer_call(layer, x, ada, g_pre, g_post, lb_logits, pool_scale, hgrn_norm_g,
                        w_in_b, pool_w_b, w_pool_o_b, w_hgrn_o_b, w_out_b)
    return x
```

```python
import functools
import math

import jax
import jax.numpy as jnp
from jax import lax
from jax.experimental import pallas as pl
from jax.experimental.pallas import tpu as pltpu

D_MODEL = 1024
POOL_WINDOWS = (2, 4, 8, 16)
POOL_GROUPS = len(POOL_WINDOWS)
POOL_GROUP_DIM = D_MODEL // 8
POOL_WIDTH = POOL_GROUPS * POOL_GROUP_DIM
HEAD_DIM = 128
HEADS = D_MODEL // HEAD_DIM
HGRN_WIDTH = HEADS * HEAD_DIM
NORM_EPS = 1e-6
F_FLOOR = 1e-30
NEG_LOG2_E = -1.4426950408889634
IN_SIZES = (POOL_WIDTH, POOL_WIDTH, HGRN_WIDTH, HGRN_WIDTH, HGRN_WIDTH, HGRN_WIDTH, D_MODEL, D_MODEL)
IN_OFFSETS = tuple(sum(IN_SIZES[:i]) for i in range(len(IN_SIZES)))
(SEC_PV, SEC_PG, SEC_HQ, SEC_HF, SEC_HI, SEC_HG, SEC_MGP, SEC_MGH) = range(len(IN_SIZES))

GRID_TILE = 512
SEQ_TILE = 256
CHUNK = 64
HALO = max(POOL_WINDOWS)
SUBLANES = 8
TILES = CHUNK // SUBLANES
TILE_BITS = (0, 1, 5)
SUBLANE_BITS = (2, 3, 4)
TIME_BITS = 6
TILES_PER_GROUP = 4
ROW_STRIDE = TILES_PER_GROUP
GROUP_TOKENS = TILES_PER_GROUP * SUBLANES
assert CHUNK == 1 << TIME_BITS
LOCAL_BITS = 2
assert TILE_BITS[:LOCAL_BITS] == tuple(range(LOCAL_BITS)) and TILES_PER_GROUP == 1 << LOCAL_BITS
N_OPS = TIME_BITS - LOCAL_BITS + 1
ADA_TILE = 1024
V7X_VMEM_BYTES = 64 * 1024 * 1024
COMPILER_TEMP_BYTES = 3 * SEQ_TILE * D_MODEL * 4 * 4

NT = (((1,), (1,)), ((), ()))
TN = (((0,), (0,)), ((), ()))


def _bdot(a, b):
    return jnp.dot(a.astype(jnp.bfloat16), b.astype(jnp.bfloat16),
                   preferred_element_type=jnp.float32)


def _sigmoid(z):
    return 1.0 / (1.0 + jnp.exp2(z * NEG_LOG2_E))


def _silu(z):
    return z * _sigmoid(z)


def _rms(x):
    return x * lax.rsqrt(jnp.mean(x * x, axis=-1, keepdims=True) + NORM_EPS)


def _roll_rows(a, shift):
    return pltpu.roll(a, shift % a.shape[0], axis=0)


def _token_of_row(p):
    group, j, i = p // GROUP_TOKENS, (p // SUBLANES) % TILES_PER_GROUP, p % SUBLANES
    return group * GROUP_TOKENS + i * ROW_STRIDE + j


def _ada_kernel(c_ref, w_ref, b_ref, o_ref):
    c = c_ref[...]
    o_ref[...] = jnp.dot(_silu(c), w_ref[...], preferred_element_type=jnp.float32) + b_ref[...]


def _sublane_partner(p, sb, isub):
    k = 1
    while k < sb:
        p = jnp.where((isub & k) != 0, p, _roll_rows(p, -k))
        k *= 2
    if 2 * sb == p.shape[0]:
        return _roll_rows(p, sb)
    return jnp.where((isub & sb) != 0, _roll_rows(p, sb), _roll_rows(p, -sb))


def _last_row(a):
    return jnp.broadcast_to(a[SUBLANES - 1:SUBLANES, :], a.shape)


def _lane_sum(a):
    return jnp.sum(a, axis=-1, keepdims=True)


def _local_outputs(q, pf, kk, v):
    out = []
    for base in range(0, TILES, TILES_PER_GROUP):
        qs, fs, ks, vs = (a[base:base + TILES_PER_GROUP] for a in (q, pf, kk, v))
        acc = [_lane_sum(qs[j] * ks[j]) * vs[j] for j in range(TILES_PER_GROUP)]
        decayed_k = {}
        for j_t in range(1, TILES_PER_GROUP):
            qf = qs[j_t] * fs[j_t]
            for j_s in range(j_t - 1, -1, -1):
                decayed_k[j_t, j_s] = (ks[j_s] if j_s == j_t - 1
                                       else decayed_k[j_t - 1, j_s] * fs[j_t - 1])
                acc[j_t] = acc[j_t] + _lane_sum(qf * decayed_k[j_t, j_s]) * vs[j_s]
        out.extend(acc)
    return out


def _scan_operands(q, f, v, isub):
    tiles = range(TILES)
    kk = [1.0 - f[t] for t in tiles]
    pre = [jnp.maximum(f[t], F_FLOOR) for t in tiles]
    ks = list(kk)
    local = _local_outputs(q, pre, kk, v)
    lhs, rhs = [], []
    for bit in range(TIME_BITS):
        if bit in TILE_BITS:
            tb = 1 << TILE_BITS.index(bit)
            if bit >= LOCAL_BITS:
                lhs.append([q[t] * pre[t] for t in tiles if t & tb])
                rhs.append([ks[t] for t in tiles if not t & tb])
        else:
            sb = 1 << SUBLANE_BITS.index(bit)
            upper = (isub & sb) != 0
            lhs.append([jnp.where(upper, q[t] * pre[t], 0.0) for t in tiles])
            rhs.append([jnp.where(upper, 0.0, ks[t]) for t in tiles])
        new_pre, new_ks = list(pre), list(ks)
        if bit in TILE_BITS:
            totals = {}
            for t in tiles:
                src = (t ^ tb) | (tb - 1)
                if src not in totals:
                    totals[src] = _last_row(pre[src]) if bit > max(SUBLANE_BITS) else pre[src]
                if t & tb:
                    new_pre[t] = pre[t] * totals[src]
                else:
                    new_ks[t] = ks[t] * totals[src]
        else:
            for base in range(0, TILES, TILES_PER_GROUP):
                partner = _sublane_partner(pre[base + TILES_PER_GROUP - 1], sb, isub)
                pre_mul = jnp.where(upper, partner, 1.0)
                ks_mul = jnp.where(upper, 1.0, partner)
                for t in range(base, base + TILES_PER_GROUP):
                    new_pre[t] = pre[t] * pre_mul
                    new_ks[t] = ks[t] * ks_mul
        pre, ks = new_pre, new_ks
    lhs.append([q[t] * pre[t] for t in tiles])
    rhs.append(ks)
    return lhs, rhs, pre[TILES - 1], local


def _layer_kernel(x_ref, ada_ref, gpre_ref, gpost_ref, lbl_ref, pscale_ref, hng_ref,
                  win_ref, poolw_ref, wpo_ref, who_ref, wout_ref, o_ref,
                  q_ref, f_ref, v_ref, g_ref, ob_ref, ext_ref, state_ref,
                  lhs_ref, rhs_ref, vb_ref, tot_ref, loc_ref, *, layer):
    ts = SEQ_TILE
    si = pl.program_id(1)

    @pl.when(si == 0)
    def _():
        state_ref[...] = jnp.zeros_like(state_ref)
        ext_ref[0:HALO, :] = jnp.zeros((HALO, POOL_WIDTH), jnp.float32)

    def tile_pass(part, carry):
        r0 = pl.multiple_of(part * SEQ_TILE, SEQ_TILE)
        x = x_ref[pl.ds(r0, SEQ_TILE), :]
        shift = ada_ref[0:1, :]
        scale = ada_ref[1:2, :]
        gate = ada_ref[2:3, :]
        hb = (_rms(x) * (gpre_ref[...] * (1.0 + scale)) + shift).astype(jnp.bfloat16)

        def proj(section):
            cols = slice(IN_OFFSETS[section], IN_OFFSETS[section] + IN_SIZES[section])
            return jnp.dot(hb, win_ref[:, cols], preferred_element_type=jnp.float32)

        logits = lbl_ref[...]
        e = jnp.exp(logits - jnp.max(logits, axis=0, keepdims=True))
        lb = jnp.zeros((1, HGRN_WIDTH), jnp.float32)
        for j in range(1, layer + 1):
            lb = lb + e[j:j + 1, :]
        lb = jnp.clip(lb / jnp.sum(e, axis=0, keepdims=True), 0.0, 1.0)

        def to_heads(ref, a):
            for hd in range(HEADS):
                ref[hd] = a[:, hd * HEAD_DIM:(hd + 1) * HEAD_DIM]

        to_heads(q_ref, _silu(proj(SEC_HQ)))
        to_heads(f_ref, lb + (1.0 - lb) * _sigmoid(proj(SEC_HF)))
        to_heads(v_ref, proj(SEC_HI))
        to_heads(g_ref, _silu(proj(SEC_HG)))

        u = proj(SEC_PV)
        ext_ref[HALO:HALO + ts, :] = u
        first = si * GRID_TILE + r0 + 1
        pos = (first + lax.broadcasted_iota(jnp.int32, (ts, 1), 0)).astype(jnp.float32)
        mixed = []
        for gi, w in enumerate(POOL_WINDOWS):
            cols = slice(gi * POOL_GROUP_DIM, (gi + 1) * POOL_GROUP_DIM)
            acc = ext_ref[:, cols]
            step = 1
            while step < w:
                acc = acc + _roll_rows(acc, step)
                step *= 2
            inv_count = 1.0 / jnp.minimum(pos, float(w))
            pooled = acc[HALO:, :] * inv_count - u[:, cols]
            mixed.append(_bdot(pooled, poolw_ref[gi]))
        ext_ref[0:HALO, :] = u[ts - HALO:, :]
        pooled = jnp.concatenate(mixed, axis=-1) * pscale_ref[...]
        branch_a = _bdot(pooled * _silu(proj(SEC_PG)), wpo_ref[...])

        isub = lax.broadcasted_iota(jnp.int32, (SUBLANES, HEAD_DIM), 0)
        cat = functools.partial(jnp.concatenate, axis=0)

        def tile_rows(ci):
            return [pl.ds(ci * CHUNK + GROUP_TOKENS * (t // TILES_PER_GROUP) + t % TILES_PER_GROUP, SUBLANES,
                          stride=ROW_STRIDE) for t in range(TILES)]

        def scan_chunk(ci):
            rows_of = tile_rows(ci)
            for hd in range(HEADS):
                q, f, v = ([ref[hd, rows, :] for rows in rows_of] for ref in (q_ref, f_ref, v_ref))
                lhs, rhs, total, local = _scan_operands(q, f, v, isub)
                for i in range(N_OPS):
                    used = slice(0, SUBLANES * len(lhs[i]))
                    lhs_ref[ci, hd, i, used] = cat(lhs[i]).astype(jnp.bfloat16)
                    rhs_ref[ci, hd, i, used] = cat(rhs[i]).astype(jnp.bfloat16)
                vb_ref[ci, hd] = cat(v).astype(jnp.bfloat16)
                tot_ref[ci, hd] = _last_row(total)
                loc_ref[ci, hd] = cat(local)

        r2 = _token_of_row(lax.broadcasted_iota(jnp.int32, (CHUNK, CHUNK), 0))
        c2 = _token_of_row(lax.broadcasted_iota(jnp.int32, (CHUNK, CHUNK), 1))
        blk_eq = [(r2 >> (bit + 1)) == (c2 >> (bit + 1)) for bit in range(TIME_BITS)]
        hng = hng_ref[...]

        def output_chunk(ci):
            rows_of = tile_rows(ci)
            heads = range(HEADS)

            def nt(hd, i):
                return lax.dot_general(lhs_ref[ci, hd, i], rhs_ref[ci, hd, i], NT,
                                       preferred_element_type=jnp.float32)

            scores = [None] * HEADS
            for bit in range(LOCAL_BITS, TIME_BITS - 1):
                for hd in heads:
                    p = jnp.where(blk_eq[bit], nt(hd, bit - LOCAL_BITS), 0.0)
                    scores[hd] = p if scores[hd] is None else scores[hd] + p
            half = CHUNK // 2
            top = [lax.dot_general(lhs_ref[ci, hd, N_OPS - 2, 0:half], rhs_ref[ci, hd, N_OPS - 2, 0:half], NT,
                                   preferred_element_type=jnp.float32) for hd in heads]
            state_t = [state_ref[hd] for hd in heads]
            o_inter = [lax.dot_general(lhs_ref[ci, hd, N_OPS - 1], state_t[hd].astype(jnp.bfloat16), NT,
                                       preferred_element_type=jnp.float32) for hd in heads]
            for hd in heads:
                state_ref[hd] = (state_t[hd] * tot_ref[ci, hd][0:1, :]
                                 + lax.dot_general(vb_ref[ci, hd], rhs_ref[ci, hd, N_OPS - 1], TN,
                                                   preferred_element_type=jnp.float32))
            for hd in heads:
                o = (o_inter[hd] + loc_ref[ci, hd]
                     + jnp.dot(scores[hd].astype(jnp.bfloat16), vb_ref[ci, hd],
                               preferred_element_type=jnp.float32))
                o_top = jnp.dot(top[hd].astype(jnp.bfloat16), vb_ref[ci, hd, 0:half],
                                preferred_element_type=jnp.float32)
                o = jnp.concatenate([o[0:half], o[half:] + o_top], axis=0)
                for t, rows in enumerate(rows_of):
                    o_tile = o[SUBLANES * t:SUBLANES * (t + 1), :]
                    ob_ref[hd, rows, :] = (_rms(o_tile) * hng) * g_ref[hd, rows, :]

        for ci in range(ts // CHUNK):
            scan_chunk(ci)
        for ci in range(ts // CHUNK):
            output_chunk(ci)
        gated = jnp.concatenate([ob_ref[hd] for hd in range(HEADS)], axis=-1)
        branch_b = _bdot(gated, who_ref[...])

        merged = _sigmoid(proj(SEC_MGP)) * branch_a + _sigmoid(proj(SEC_MGH)) * branch_b
        y = _bdot(merged, wout_ref[...])
        o_ref[pl.ds(r0, SEQ_TILE), :] = x + _rms(y) * (gate * gpost_ref[...])
        return carry

    lax.fori_loop(0, GRID_TILE // SEQ_TILE, tile_pass, 0)


def _resident(shape, index_map):
    return pl.BlockSpec(shape, index_map, pipeline_mode=pl.Buffered(1))


def _nbytes(shape, dtype):
    return math.prod(shape) * jnp.dtype(dtype).itemsize


def _layer_call(layer, x, ada, g_pre, g_post, lb_logits, pool_scale, hgrn_norm_g,
                w_in, pool_w, w_pool_o, w_hgrn_o, w_out):
    batch, seq, d = x.shape
    depth = lb_logits.shape[0]
    ts = SEQ_TILE
    assert seq % GRID_TILE == 0 and GRID_TILE % ts == 0 and ts % CHUNK == 0 and d == D_MODEL
    chunks = ts // CHUNK

    def const2(b, s):
        return (0, 0)

    def per_layer_row(a):
        return a.reshape(depth, 1, a.shape[-1])

    def per_layer3(b, s):
        return (layer, 0, 0)

    in_specs = [
        pl.BlockSpec((None, GRID_TILE, d), lambda b, s: (b, s, 0)),
        pl.BlockSpec((None, None, 3, d), lambda b, s: (layer, b, 0, 0)),
        _resident((None, 1, d), per_layer3),
        _resident((None, 1, d), per_layer3),
        _resident((depth, HGRN_WIDTH), const2),
        _resident((None, 1, POOL_WIDTH), per_layer3),
        _resident((None, 1, HEAD_DIM), per_layer3),
        _resident((None, d, sum(IN_SIZES)), per_layer3),
        _resident((None, POOL_GROUPS, POOL_GROUP_DIM, POOL_GROUP_DIM), lambda b, s: (layer, 0, 0, 0)),
        _resident((None, POOL_WIDTH, d), per_layer3),
        _resident((None, HGRN_WIDTH, d), per_layer3),
        _resident((None, d, d), per_layer3),
    ]
    scratch = [
        pltpu.VMEM((HEADS, ts, HEAD_DIM), jnp.float32),
        pltpu.VMEM((HEADS, ts, HEAD_DIM), jnp.float32),
        pltpu.VMEM((HEADS, ts, HEAD_DIM), jnp.float32),
        pltpu.VMEM((HEADS, ts, HEAD_DIM), jnp.float32),
        pltpu.VMEM((HEADS, ts, HEAD_DIM), jnp.float32),
        pltpu.VMEM((HALO + ts, POOL_WIDTH), jnp.float32),
        pltpu.VMEM((HEADS, HEAD_DIM, HEAD_DIM), jnp.float32),
        pltpu.VMEM((chunks, HEADS, N_OPS, CHUNK, HEAD_DIM), jnp.bfloat16),
        pltpu.VMEM((chunks, HEADS, N_OPS, CHUNK, HEAD_DIM), jnp.bfloat16),
        pltpu.VMEM((chunks, HEADS, CHUNK, HEAD_DIM), jnp.bfloat16),
        pltpu.VMEM((chunks, HEADS, SUBLANES, HEAD_DIM), jnp.float32),
        pltpu.VMEM((chunks, HEADS, CHUNK, HEAD_DIM), jnp.float32),
    ]
    weights = (w_in, pool_w, w_pool_o, w_hgrn_o, w_out)
    vmem_bytes = (sum(_nbytes(w.shape[1:], w.dtype) for w in weights)
                  + 2 * 2 * _nbytes((GRID_TILE, d), x.dtype)
                  + sum(_nbytes(s.shape, s.dtype) for s in scratch)
                  + COMPILER_TEMP_BYTES)
    assert vmem_bytes <= V7X_VMEM_BYTES, vmem_bytes
    return pl.pallas_call(
        functools.partial(_layer_kernel, layer=layer),
        grid=(batch, seq // GRID_TILE),
        in_specs=in_specs,
        out_specs=pl.BlockSpec((None, GRID_TILE, d), lambda b, s: (b, s, 0)),
        out_shape=jax.ShapeDtypeStruct(x.shape, x.dtype),
        scratch_shapes=scratch,
        compiler_params=pltpu.CompilerParams(
            dimension_semantics=("arbitrary", "arbitrary"),
            vmem_limit_bytes=vmem_bytes),
        name=f"hybrid_layer{layer}",
    )(x, ada, per_layer_row(g_pre), per_layer_row(g_post), lb_logits,
      per_layer_row(pool_scale), per_layer_row(hgrn_norm_g),
      w_in, pool_w, w_pool_o, w_hgrn_o, w_out)


def _ada_call(c, w_ada, b_ada):
    depth, d, width = w_ada.shape
    batch = c.shape[0]
    assert width % ADA_TILE == 0
    return pl.pallas_call(
        _ada_kernel,
        grid=(depth, width // ADA_TILE),
        in_specs=[
            pl.BlockSpec((batch, d), lambda l, n: (0, 0)),
            pl.BlockSpec((None, d, ADA_TILE), lambda l, n: (l, 0, n)),
            pl.BlockSpec((None, 1, ADA_TILE), lambda l, n: (l, 0, n)),
        ],
        out_specs=pl.BlockSpec((None, batch, ADA_TILE), lambda l, n: (l, 0, n)),
        out_shape=jax.ShapeDtypeStruct((depth, batch, width), jnp.float32),
        name="adaln_vectors",
    )(c, w_ada, b_ada.reshape(depth, 1, width))


def kernel(x, c, w_ada, b_ada, g_pre, g_post, w_in, pool_w, pool_scale, lb_logits,
           hgrn_norm_g, w_pool_o, w_hgrn_o, w_out):
    depth = w_in.shape[0]
    batch, _, d = x.shape
    ada = _ada_call(c, w_ada, b_ada).reshape(depth, batch, 3, d)
    bf = jnp.bfloat16
    w_in_b, pool_w_b = w_in.astype(bf), pool_w.astype(bf)
    w_pool_o_b, w_hgrn_o_b, w_out_b = w_pool_o.astype(bf), w_hgrn_o.astype(bf), w_out.astype(bf)
    for layer in range(depth):
        x = _layer_call(layer, x, ada, g_pre, g_post, lb_logits, pool_scale, hgrn_norm_g,
                        w_in_b, pool_w_b, w_pool_o_b, w_hgrn_o_b, w_out_b)
    return x
```

```python
import functools
import math

import jax
import jax.numpy as jnp
from jax import lax
from jax.experimental import pallas as pl
from jax.experimental.pallas import tpu as pltpu

D_MODEL = 1024
POOL_WINDOWS = (2, 4, 8, 16)
POOL_GROUPS = len(POOL_WINDOWS)
POOL_GROUP_DIM = D_MODEL // 8
POOL_WIDTH = POOL_GROUPS * POOL_GROUP_DIM
HEAD_DIM = 128
HEADS = D_MODEL // HEAD_DIM
HGRN_WIDTH = HEADS * HEAD_DIM
NORM_EPS = 1e-6
F_FLOOR = 1e-30
NEG_LOG2_E = -1.4426950408889634
IN_SIZES = (POOL_WIDTH, POOL_WIDTH, HGRN_WIDTH, HGRN_WIDTH, HGRN_WIDTH, HGRN_WIDTH, D_MODEL, D_MODEL)
IN_OFFSETS = tuple(sum(IN_SIZES[:i]) for i in range(len(IN_SIZES)))
(SEC_PV, SEC_PG, SEC_HQ, SEC_HF, SEC_HI, SEC_HG, SEC_MGP, SEC_MGH) = range(len(IN_SIZES))

GRID_TILE = 512
SEQ_TILE = 256
CHUNK = 64
HALO = max(POOL_WINDOWS)
SUBLANES = 8
TILES = CHUNK // SUBLANES
TILE_BITS = (0, 1, 5)
SUBLANE_BITS = (2, 3, 4)
TIME_BITS = 6
TILES_PER_GROUP = 4
ROW_STRIDE = TILES_PER_GROUP
GROUP_TOKENS = TILES_PER_GROUP * SUBLANES
assert CHUNK == 1 << TIME_BITS
LOCAL_BITS = 2
assert TILE_BITS[:LOCAL_BITS] == tuple(range(LOCAL_BITS)) and TILES_PER_GROUP == 1 << LOCAL_BITS
N_OPS = TIME_BITS - LOCAL_BITS + 1
ADA_TILE = 1024
V7X_VMEM_BYTES = 64 * 1024 * 1024
COMPILER_TEMP_BYTES = 3 * SEQ_TILE * D_MODEL * 4 * 4

NT = (((1,), (1,)), ((), ()))
TN = (((0,), (0,)), ((), ()))


def _bdot(a, b):
    return jnp.dot(a.astype(jnp.bfloat16), b.astype(jnp.bfloat16),
                   preferred_element_type=jnp.float32)


def _sigmoid(z):
    return 1.0 / (1.0 + jnp.exp2(z * NEG_LOG2_E))


def _silu(z):
    return z * _sigmoid(z)


def _rms(x):
    return x * lax.rsqrt(jnp.mean(x * x, axis=-1, keepdims=True) + NORM_EPS)


def _roll_rows(a, shift):
    return pltpu.roll(a, shift % a.shape[0], axis=0)


def _token_of_row(p):
    group, j, i = p // GROUP_TOKENS, (p // SUBLANES) % TILES_PER_GROUP, p % SUBLANES
    return group * GROUP_TOKENS + i * ROW_STRIDE + j


def _ada_kernel(c_ref, w_ref, b_ref, o_ref):
    c = c_ref[...]
    o_ref[...] = jnp.dot(_silu(c), w_ref[...], preferred_element_type=jnp.float32) + b_ref[...]


def _sublane_partner(p, sb, isub):
    k = 1
    while k < sb:
        p = jnp.where((isub & k) != 0, p, _roll_rows(p, -k))
        k *= 2
    if 2 * sb == p.shape[0]:
        return _roll_rows(p, sb)
    return jnp.where((isub & sb) != 0, _roll_rows(p, sb), _roll_rows(p, -sb))


def _last_row(a):
    return jnp.broadcast_to(a[SUBLANES - 1:SUBLANES, :], a.shape)


def _lane_sum(a):
    return jnp.sum(a, axis=-1, keepdims=True)


def _local_outputs(q, pf, kk, v):
    out = []
    for base in range(0, TILES, TILES_PER_GROUP):
        qs, fs, ks, vs = (a[base:base + TILES_PER_GROUP] for a in (q, pf, kk, v))
        acc = [_lane_sum(qs[j] * ks[j]) * vs[j] for j in range(TILES_PER_GROUP)]
        decayed_k = {}
        for j_t in range(1, TILES_PER_GROUP):
            qf = qs[j_t] * fs[j_t]
            for j_s in range(j_t - 1, -1, -1):
                decayed_k[j_t, j_s] = (ks[j_s] if j_s == j_t - 1
                                       else decayed_k[j_t - 1, j_s] * fs[j_t - 1])
                acc[j_t] = acc[j_t] + _lane_sum(qf * decayed_k[j_t, j_s]) * vs[j_s]
        out.extend(acc)
    return out


def _scan_operands(q, f, v, isub):
    tiles = range(TILES)
    kk = [1.0 - f[t] for t in tiles]
    pre = [jnp.maximum(f[t], F_FLOOR) for t in tiles]
    ks = list(kk)
    local = _local_outputs(q, pre, kk, v)
    lhs, rhs = [], []
    for bit in range(TIME_BITS):
        if bit in TILE_BITS:
            tb = 1 << TILE_BITS.index(bit)
            if bit >= LOCAL_BITS:
                lhs.append([q[t] * pre[t] for t in tiles if t & tb])
                rhs.append([ks[t] for t in tiles if not t & tb])
        else:
            sb = 1 << SUBLANE_BITS.index(bit)
            upper = (isub & sb) != 0
            lhs.append([jnp.where(upper, q[t] * pre[t], 0.0) for t in tiles])
            rhs.append([jnp.where(upper, 0.0, ks[t]) for t in tiles])
        new_pre, new_ks = list(pre), list(ks)
        if bit in TILE_BITS:
            totals = {}
            for t in tiles:
                src = (t ^ tb) | (tb - 1)
                if src not in totals:
                    totals[src] = _last_row(pre[src]) if bit > max(SUBLANE_BITS) else pre[src]
                if t & tb:
                    new_pre[t] = pre[t] * totals[src]
                else:
                    new_ks[t] = ks[t] * totals[src]
        else:
            for base in range(0, TILES, TILES_PER_GROUP):
                partner = _sublane_partner(pre[base + TILES_PER_GROUP - 1], sb, isub)
                pre_mul = jnp.where(upper, partner, 1.0)
                ks_mul = jnp.where(upper, 1.0, partner)
                for t in range(base, base + TILES_PER_GROUP):
                    new_pre[t] = pre[t] * pre_mul
                    new_ks[t] = ks[t] * ks_mul
        pre, ks = new_pre, new_ks
    lhs.append([q[t] * pre[t] for t in tiles])
    rhs.append(ks)
    return lhs, rhs, pre[TILES - 1], local


def _layer_kernel(x_ref, ada_ref, gpre_ref, gpost_ref, lbl_ref, pscale_ref, hng_ref,
                  win_ref, poolw_ref, wpo_ref, who_ref, wout_ref, o_ref,
                  q_ref, f_ref, v_ref, g_ref, ob_ref, ext_ref, state_ref,
                  lhs_ref, rhs_ref, vb_ref, tot_ref, loc_ref, *, layer):
    ts = SEQ_TILE
    si = pl.program_id(1)

    @pl.when(si == 0)
    def _():
        state_ref[...] = jnp.zeros_like(state_ref)
        ext_ref[0:HALO, :] = jnp.zeros((HALO, POOL_WIDTH), jnp.float32)

    def tile_pass(part, carry):
        r0 = pl.multiple_of(part * SEQ_TILE, SEQ_TILE)
        x = x_ref[pl.ds(r0, SEQ_TILE), :]
        shift = ada_ref[0:1, :]
        scale = ada_ref[1:2, :]
        gate = ada_ref[2:3, :]
        hb = (_rms(x) * (gpre_ref[...] * (1.0 + scale)) + shift).astype(jnp.bfloat16)

        def proj(section):
            cols = slice(IN_OFFSETS[section], IN_OFFSETS[section] + IN_SIZES[section])
            return jnp.dot(hb, win_ref[:, cols], preferred_element_type=jnp.float32)

        logits = lbl_ref[...]
        e = jnp.exp(logits - jnp.max(logits, axis=0, keepdims=True))
        lb = jnp.zeros((1, HGRN_WIDTH), jnp.float32)
        for j in range(1, layer + 1):
            lb = lb + e[j:j + 1, :]
        lb = jnp.clip(lb / jnp.sum(e, axis=0, keepdims=True), 0.0, 1.0)

        def to_heads(ref, a):
            for hd in range(HEADS):
                ref[hd] = a[:, hd * HEAD_DIM:(hd + 1) * HEAD_DIM]

        to_heads(q_ref, _silu(proj(SEC_HQ)))
        to_heads(f_ref, lb + (1.0 - lb) * _sigmoid(proj(SEC_HF)))
        to_heads(v_ref, proj(SEC_HI))
        to_heads(g_ref, _silu(proj(SEC_HG)))

        u = proj(SEC_PV)
        ext_ref[HALO:HALO + ts, :] = u
        first = si * GRID_TILE + r0 + 1
        pos = (first + lax.broadcasted_iota(jnp.int32, (ts, 1), 0)).astype(jnp.float32)
        mixed = []
        for gi, w in enumerate(POOL_WINDOWS):
            cols = slice(gi * POOL_GROUP_DIM, (gi + 1) * POOL_GROUP_DIM)
            acc = ext_ref[:, cols]
            step = 1
            while step < w:
                acc = acc + _roll_rows(acc, step)
                step *= 2
            inv_count = 1.0 / jnp.minimum(pos, float(w))
            pooled = acc[HALO:, :] * inv_count - u[:, cols]
            mixed.append(_bdot(pooled, poolw_ref[gi]))
        ext_ref[0:HALO, :] = u[ts - HALO:, :]
        pooled = jnp.concatenate(mixed, axis=-1) * pscale_ref[...]
        branch_a = _bdot(pooled * _silu(proj(SEC_PG)), wpo_ref[...])

        isub = lax.broadcasted_iota(jnp.int32, (SUBLANES, HEAD_DIM), 0)
        cat = functools.partial(jnp.concatenate, axis=0)

        def tile_rows(ci):
            return [pl.ds(ci * CHUNK + GROUP_TOKENS * (t // TILES_PER_GROUP) + t % TILES_PER_GROUP, SUBLANES,
                          stride=ROW_STRIDE) for t in range(TILES)]

        def scan_chunk(ci):
            rows_of = tile_rows(ci)
            for hd in range(HEADS):
                q, f, v = ([ref[hd, rows, :] for rows in rows_of] for ref in (q_ref, f_ref, v_ref))
                lhs, rhs, total, local = _scan_operands(q, f, v, isub)
                for i in range(N_OPS):
                    used = slice(0, SUBLANES * len(lhs[i]))
                    lhs_ref[ci, hd, i, used] = cat(lhs[i]).astype(jnp.bfloat16)
                    rhs_ref[ci, hd, i, used] = cat(rhs[i]).astype(jnp.bfloat16)
                vb_ref[ci, hd] = cat(v).astype(jnp.bfloat16)
                decay = _last_row(total)
                hi = decay.astype(jnp.bfloat16).astype(jnp.float32)
                mid = (decay - hi).astype(jnp.bfloat16).astype(jnp.float32)
                lo = decay - hi - mid
                pieces = jnp.where(isub == 0, hi, jnp.where(isub == 1, mid, jnp.where(isub == 2, lo, 0.0)))
                tot_ref[ci, hd] = jnp.concatenate([pieces, jnp.zeros_like(pieces)], axis=0).astype(jnp.bfloat16)
                loc_ref[ci, hd] = cat(local)

        r2 = _token_of_row(lax.broadcasted_iota(jnp.int32, (CHUNK, CHUNK), 0))
        c2 = _token_of_row(lax.broadcasted_iota(jnp.int32, (CHUNK, CHUNK), 1))
        blk_eq = [(r2 >> (bit + 1)) == (c2 >> (bit + 1)) for bit in range(TIME_BITS)]
        hng = hng_ref[...]
        ones_rows = jnp.ones((2 * SUBLANES, HEAD_DIM), jnp.bfloat16)

        def output_chunk(ci):
            rows_of = tile_rows(ci)
            heads = range(HEADS)

            def nt(hd, i):
                return lax.dot_general(lhs_ref[ci, hd, i], rhs_ref[ci, hd, i], NT,
                                       preferred_element_type=jnp.float32)

            scores = [None] * HEADS
            for bit in range(LOCAL_BITS, TIME_BITS - 1):
                for hd in heads:
                    p = jnp.where(blk_eq[bit], nt(hd, bit - LOCAL_BITS), 0.0)
                    scores[hd] = p if scores[hd] is None else scores[hd] + p
            half = CHUNK // 2
            top = [lax.dot_general(lhs_ref[ci, hd, N_OPS - 2, 0:half], rhs_ref[ci, hd, N_OPS - 2, 0:half], NT,
                                   preferred_element_type=jnp.float32) for hd in heads]
            state = [state_ref[hd] for hd in heads]
            o_inter = [jnp.dot(lhs_ref[ci, hd, N_OPS - 1], state[hd].astype(jnp.bfloat16),
                               preferred_element_type=jnp.float32) for hd in heads]
            for hd in heads:
                decay_kv = lax.dot_general(tot_ref[ci, hd], ones_rows, TN, preferred_element_type=jnp.float32)
                state_ref[hd] = (state[hd] * decay_kv
                                 + lax.dot_general(rhs_ref[ci, hd, N_OPS - 1], vb_ref[ci, hd], TN,
                                                   preferred_element_type=jnp.float32))
            for hd in heads:
                o = (o_inter[hd] + loc_ref[ci, hd]
                     + jnp.dot(scores[hd].astype(jnp.bfloat16), vb_ref[ci, hd],
                               preferred_element_type=jnp.float32))
                o_top = jnp.dot(top[hd].astype(jnp.bfloat16), vb_ref[ci, hd, 0:half],
                                preferred_element_type=jnp.float32)
                o = jnp.concatenate([o[0:half], o[half:] + o_top], axis=0)
                for t, rows in enumerate(rows_of):
                    o_tile = o[SUBLANES * t:SUBLANES * (t + 1), :]
                    ob_ref[hd, rows, :] = (_rms(o_tile) * hng) * g_ref[hd, rows, :]

        for ci in range(ts // CHUNK):
            scan_chunk(ci)
        for ci in range(ts // CHUNK):
            output_chunk(ci)
        gated = jnp.concatenate([ob_ref[hd] for hd in range(HEADS)], axis=-1)
        branch_b = _bdot(gated, who_ref[...])

        merged = _sigmoid(proj(SEC_MGP)) * branch_a + _sigmoid(proj(SEC_MGH)) * branch_b
        y = _bdot(merged, wout_ref[...])
        o_ref[pl.ds(r0, SEQ_TILE), :] = x + _rms(y) * (gate * gpost_ref[...])
        return carry

    lax.fori_loop(0, GRID_TILE // SEQ_TILE, tile_pass, 0)


def _resident(shape, index_map):
    return pl.BlockSpec(shape, index_map, pipeline_mode=pl.Buffered(1))


def _nbytes(shape, dtype):
    return math.prod(shape) * jnp.dtype(dtype).itemsize


def _layer_call(layer, x, ada, g_pre, g_post, lb_logits, pool_scale, hgrn_norm_g,
                w_in, pool_w, w_pool_o, w_hgrn_o, w_out):
    batch, seq, d = x.shape
    depth = lb_logits.shape[0]
    ts = SEQ_TILE
    assert seq % GRID_TILE == 0 and GRID_TILE % ts == 0 and ts % CHUNK == 0 and d == D_MODEL
    chunks = ts // CHUNK

    def const2(b, s):
        return (0, 0)

    def per_layer_row(a):
        return a.reshape(depth, 1, a.shape[-1])

    def per_layer3(b, s):
        return (layer, 0, 0)

    in_specs = [
        pl.BlockSpec((None, GRID_TILE, d), lambda b, s: (b, s, 0)),
        pl.BlockSpec((None, None, 3, d), lambda b, s: (layer, b, 0, 0)),
        _resident((None, 1, d), per_layer3),
        _resident((None, 1, d), per_layer3),
        _resident((depth, HGRN_WIDTH), const2),
        _resident((None, 1, POOL_WIDTH), per_layer3),
        _resident((None, 1, HEAD_DIM), per_layer3),
        _resident((None, d, sum(IN_SIZES)), per_layer3),
        _resident((None, POOL_GROUPS, POOL_GROUP_DIM, POOL_GROUP_DIM), lambda b, s: (layer, 0, 0, 0)),
        _resident((None, POOL_WIDTH, d), per_layer3),
        _resident((None, HGRN_WIDTH, d), per_layer3),
        _resident((None, d, d), per_layer3),
    ]
    scratch = [
        pltpu.VMEM((HEADS, ts, HEAD_DIM), jnp.float32),
        pltpu.VMEM((HEADS, ts, HEAD_DIM), jnp.float32),
        pltpu.VMEM((HEADS, ts, HEAD_DIM), jnp.float32),
        pltpu.VMEM((HEADS, ts, HEAD_DIM), jnp.float32),
        pltpu.VMEM((HEADS, ts, HEAD_DIM), jnp.float32),
        pltpu.VMEM((HALO + ts, POOL_WIDTH), jnp.float32),
        pltpu.VMEM((HEADS, HEAD_DIM, HEAD_DIM), jnp.float32),
        pltpu.VMEM((chunks, HEADS, N_OPS, CHUNK, HEAD_DIM), jnp.bfloat16),
        pltpu.VMEM((chunks, HEADS, N_OPS, CHUNK, HEAD_DIM), jnp.bfloat16),
        pltpu.VMEM((chunks, HEADS, CHUNK, HEAD_DIM), jnp.bfloat16),
        pltpu.VMEM((chunks, HEADS, 2 * SUBLANES, HEAD_DIM), jnp.bfloat16),
        pltpu.VMEM((chunks, HEADS, CHUNK, HEAD_DIM), jnp.float32),
    ]
    weights = (w_in, pool_w, w_pool_o, w_hgrn_o, w_out)
    vmem_bytes = (sum(_nbytes(w.shape[1:], w.dtype) for w in weights)
                  + 2 * 2 * _nbytes((GRID_TILE, d), x.dtype)
                  + sum(_nbytes(s.shape, s.dtype) for s in scratch)
                  + COMPILER_TEMP_BYTES)
    assert vmem_bytes <= V7X_VMEM_BYTES, vmem_bytes
    return pl.pallas_call(
        functools.partial(_layer_kernel, layer=layer),
        grid=(batch, seq // GRID_TILE),
        in_specs=in_specs,
        out_specs=pl.BlockSpec((None, GRID_TILE, d), lambda b, s: (b, s, 0)),
        out_shape=jax.ShapeDtypeStruct(x.shape, x.dtype),
        scratch_shapes=scratch,
        compiler_params=pltpu.CompilerParams(
            dimension_semantics=("arbitrary", "arbitrary"),
            vmem_limit_bytes=vmem_bytes),
        name=f"hybrid_layer{layer}",
    )(x, ada, per_layer_row(g_pre), per_layer_row(g_post), lb_logits,
      per_layer_row(pool_scale), per_layer_row(hgrn_norm_g),
      w_in, pool_w, w_pool_o, w_hgrn_o, w_out)


def _ada_call(c, w_ada, b_ada):
    depth, d, width = w_ada.shape
    batch = c.shape[0]
    assert width % ADA_TILE == 0
    return pl.pallas_call(
        _ada_kernel,
        grid=(depth, width // ADA_TILE),
        in_specs=[
            pl.BlockSpec((batch, d), lambda l, n: (0, 0)),
            pl.BlockSpec((None, d, ADA_TILE), lambda l, n: (l, 0, n)),
            pl.BlockSpec((None, 1, ADA_TILE), lambda l, n: (l, 0, n)),
        ],
        out_specs=pl.BlockSpec((None, batch, ADA_TILE), lambda l, n: (l, 0, n)),
        out_shape=jax.ShapeDtypeStruct((depth, batch, width), jnp.float32),
        name="adaln_vectors",
    )(c, w_ada, b_ada.reshape(depth, 1, width))


def kernel(x, c, w_ada, b_ada, g_pre, g_post, w_in, pool_w, pool_scale, lb_logits,
           hgrn_norm_g, w_pool_o, w_hgrn_o, w_out):
    depth = w_in.shape[0]
    batch, _, d = x.shape
    ada = _ada_call(c, w_ada, b_ada).reshape(depth, batch, 3, d)
    bf = jnp.bfloat16
    w_in_b, pool_w_b = w_in.astype(bf), pool_w.astype(bf)
    w_pool_o_b, w_hgrn_o_b, w_out_b = w_pool_o.astype(bf), w_hgrn_o.astype(bf), w_out.astype(bf)
    for layer in range(depth):
        x = _layer_call(layer, x, ada, g_pre, g_post, lb_logits, pool_scale, hgrn_norm_g,
                        w_in_b, pool_w_b, w_pool_o_b, w_hgrn_o_b, w_out_b)
    return x
```

```python
import functools
import math

import jax
import jax.numpy as jnp
from jax import lax
from jax.experimental import pallas as pl
from jax.experimental.pallas import tpu as pltpu

D_MODEL = 1024
POOL_WINDOWS = (2, 4, 8, 16)
POOL_GROUPS = len(POOL_WINDOWS)
POOL_GROUP_DIM = D_MODEL // 8
POOL_WIDTH = POOL_GROUPS * POOL_GROUP_DIM
HEAD_DIM = 128
HEADS = D_MODEL // HEAD_DIM
HGRN_WIDTH = HEADS * HEAD_DIM
NORM_EPS = 1e-6
F_FLOOR = 1e-30
NEG_LOG2_E = -1.4426950408889634
IN_SIZES = (POOL_WIDTH, POOL_WIDTH, HGRN_WIDTH, HGRN_WIDTH, HGRN_WIDTH, HGRN_WIDTH, D_MODEL, D_MODEL)
IN_OFFSETS = tuple(sum(IN_SIZES[:i]) for i in range(len(IN_SIZES)))
(SEC_PV, SEC_PG, SEC_HQ, SEC_HF, SEC_HI, SEC_HG, SEC_MGP, SEC_MGH) = range(len(IN_SIZES))

GRID_TILE = 512
SEQ_TILE = 256
CHUNK = 64
HALO = max(POOL_WINDOWS)
SUBLANES = 8
TILES = CHUNK // SUBLANES
TILE_BITS = (0, 1, 5)
SUBLANE_BITS = (2, 3, 4)
TIME_BITS = 6
TILES_PER_GROUP = 4
ROW_STRIDE = TILES_PER_GROUP
GROUP_TOKENS = TILES_PER_GROUP * SUBLANES
assert CHUNK == 1 << TIME_BITS
LOCAL_BITS = 2
assert TILE_BITS[:LOCAL_BITS] == tuple(range(LOCAL_BITS)) and TILES_PER_GROUP == 1 << LOCAL_BITS
N_OPS = TIME_BITS - LOCAL_BITS + 1
ADA_TILE = 1024
V7X_VMEM_BYTES = 64 * 1024 * 1024
COMPILER_TEMP_BYTES = 3 * SEQ_TILE * D_MODEL * 4 * 4

NT = (((1,), (1,)), ((), ()))
TN = (((0,), (0,)), ((), ()))


def _bdot(a, b):
    return jnp.dot(a.astype(jnp.bfloat16), b.astype(jnp.bfloat16),
                   preferred_element_type=jnp.float32)


def _sigmoid(z):
    return 1.0 / (1.0 + jnp.exp2(z * NEG_LOG2_E))


def _silu(z):
    return z * _sigmoid(z)


def _rms(x):
    return x * lax.rsqrt(jnp.mean(x * x, axis=-1, keepdims=True) + NORM_EPS)


def _roll_rows(a, shift):
    return pltpu.roll(a, shift % a.shape[0], axis=0)


def _token_of_row(p):
    group, j, i = p // GROUP_TOKENS, (p // SUBLANES) % TILES_PER_GROUP, p % SUBLANES
    return group * GROUP_TOKENS + i * ROW_STRIDE + j


def _ada_kernel(c_ref, w_ref, b_ref, o_ref):
    c = c_ref[...]
    o_ref[...] = jnp.dot(_silu(c), w_ref[...], preferred_element_type=jnp.float32) + b_ref[...]


def _sublane_partner(p, sb, isub):
    k = 1
    while k < sb:
        p = jnp.where((isub & k) != 0, p, _roll_rows(p, -k))
        k *= 2
    if 2 * sb == p.shape[0]:
        return _roll_rows(p, sb)
    return jnp.where((isub & sb) != 0, _roll_rows(p, sb), _roll_rows(p, -sb))


def _last_row(a):
    return jnp.broadcast_to(a[SUBLANES - 1:SUBLANES, :], a.shape)


def _lane_sum(a):
    return jnp.sum(a, axis=-1, keepdims=True)


def _local_outputs(q, pf, kk, v):
    out = []
    for base in range(0, TILES, TILES_PER_GROUP):
        qs, fs, ks, vs = (a[base:base + TILES_PER_GROUP] for a in (q, pf, kk, v))
        acc = [_lane_sum(qs[j] * ks[j]) * vs[j] for j in range(TILES_PER_GROUP)]
        decayed_k = {}
        for j_t in range(1, TILES_PER_GROUP):
            qf = qs[j_t] * fs[j_t]
            for j_s in range(j_t - 1, -1, -1):
                decayed_k[j_t, j_s] = (ks[j_s] if j_s == j_t - 1
                                       else decayed_k[j_t - 1, j_s] * fs[j_t - 1])
                acc[j_t] = acc[j_t] + _lane_sum(qf * decayed_k[j_t, j_s]) * vs[j_s]
        out.extend(acc)
    return out


def _scan_operands(q, f, v, isub):
    tiles = range(TILES)
    kk = [1.0 - f[t] for t in tiles]
    pre = [jnp.maximum(f[t], F_FLOOR) for t in tiles]
    ks = list(kk)
    local = _local_outputs(q, pre, kk, v)
    lhs, rhs = [], []
    for bit in range(TIME_BITS):
        if bit in TILE_BITS:
            tb = 1 << TILE_BITS.index(bit)
            if bit >= LOCAL_BITS:
                lhs.append([q[t] * pre[t] for t in tiles if t & tb])
                rhs.append([ks[t] for t in tiles if not t & tb])
        else:
            sb = 1 << SUBLANE_BITS.index(bit)
            upper = (isub & sb) != 0
            lhs.append([jnp.where(upper, q[t] * pre[t], ks[t]) for t in tiles])
            rhs.append(None)
        new_pre, new_ks = list(pre), list(ks)
        if bit in TILE_BITS:
            totals = {}
            for t in tiles:
                src = (t ^ tb) | (tb - 1)
                if src not in totals:
                    totals[src] = _last_row(pre[src]) if bit > max(SUBLANE_BITS) else pre[src]
                if t & tb:
                    new_pre[t] = pre[t] * totals[src]
                else:
                    new_ks[t] = ks[t] * totals[src]
        else:
            for base in range(0, TILES, TILES_PER_GROUP):
                partner = _sublane_partner(pre[base + TILES_PER_GROUP - 1], sb, isub)
                pre_mul = jnp.where(upper, partner, 1.0)
                ks_mul = jnp.where(upper, 1.0, partner)
                for t in range(base, base + TILES_PER_GROUP):
                    new_pre[t] = pre[t] * pre_mul
                    new_ks[t] = ks[t] * ks_mul
        pre, ks = new_pre, new_ks
    lhs.append([q[t] * pre[t] for t in tiles])
    rhs.append(ks)
    return lhs, rhs, pre[TILES - 1], local


def _layer_kernel(x_ref, ada_ref, gpre_ref, gpost_ref, lbl_ref, pscale_ref, hng_ref,
                  win_ref, poolw_ref, wpo_ref, who_ref, wout_ref, o_ref,
                  q_ref, f_ref, v_ref, g_ref, ob_ref, ext_ref, state_ref,
                  lhs_ref, rhs_ref, vb_ref, tot_ref, loc_ref, *, layer):
    ts = SEQ_TILE
    si = pl.program_id(1)

    @pl.when(si == 0)
    def _():
        state_ref[...] = jnp.zeros_like(state_ref)
        ext_ref[0:HALO, :] = jnp.zeros((HALO, POOL_WIDTH), jnp.float32)

    def tile_pass(part, carry):
        r0 = pl.multiple_of(part * SEQ_TILE, SEQ_TILE)
        x = x_ref[pl.ds(r0, SEQ_TILE), :]
        shift = ada_ref[0:1, :]
        scale = ada_ref[1:2, :]
        gate = ada_ref[2:3, :]
        hb = (_rms(x) * (gpre_ref[...] * (1.0 + scale)) + shift).astype(jnp.bfloat16)

        def proj(section):
            cols = slice(IN_OFFSETS[section], IN_OFFSETS[section] + IN_SIZES[section])
            return jnp.dot(hb, win_ref[:, cols], preferred_element_type=jnp.float32)

        logits = lbl_ref[...]
        e = jnp.exp(logits - jnp.max(logits, axis=0, keepdims=True))
        lb = jnp.zeros((1, HGRN_WIDTH), jnp.float32)
        for j in range(1, layer + 1):
            lb = lb + e[j:j + 1, :]
        lb = jnp.clip(lb / jnp.sum(e, axis=0, keepdims=True), 0.0, 1.0)

        def to_heads(ref, a):
            for hd in range(HEADS):
                ref[hd] = a[:, hd * HEAD_DIM:(hd + 1) * HEAD_DIM]

        to_heads(q_ref, _silu(proj(SEC_HQ)))
        to_heads(f_ref, lb + (1.0 - lb) * _sigmoid(proj(SEC_HF)))
        to_heads(v_ref, proj(SEC_HI))
        to_heads(g_ref, _silu(proj(SEC_HG)))

        u = proj(SEC_PV)
        ext_ref[HALO:HALO + ts, :] = u
        first = si * GRID_TILE + r0 + 1
        pos = (first + lax.broadcasted_iota(jnp.int32, (ts, 1), 0)).astype(jnp.float32)
        mixed = []
        for gi, w in enumerate(POOL_WINDOWS):
            cols = slice(gi * POOL_GROUP_DIM, (gi + 1) * POOL_GROUP_DIM)
            acc = ext_ref[:, cols]
            step = 1
            while step < w:
                acc = acc + _roll_rows(acc, step)
                step *= 2
            inv_count = 1.0 / jnp.minimum(pos, float(w))
            pooled = acc[HALO:, :] * inv_count - u[:, cols]
            mixed.append(_bdot(pooled, poolw_ref[gi]))
        ext_ref[0:HALO, :] = u[ts - HALO:, :]
        pooled = jnp.concatenate(mixed, axis=-1) * pscale_ref[...]
        branch_a = _bdot(pooled * _silu(proj(SEC_PG)), wpo_ref[...])

        isub = lax.broadcasted_iota(jnp.int32, (SUBLANES, HEAD_DIM), 0)
        cat = functools.partial(jnp.concatenate, axis=0)

        def tile_rows(ci):
            return [pl.ds(ci * CHUNK + GROUP_TOKENS * (t // TILES_PER_GROUP) + t % TILES_PER_GROUP, SUBLANES,
                          stride=ROW_STRIDE) for t in range(TILES)]

        def scan_chunk(ci):
            rows_of = tile_rows(ci)
            for hd in range(HEADS):
                q, f, v = ([ref[hd, rows, :] for rows in rows_of] for ref in (q_ref, f_ref, v_ref))
                lhs, rhs, total, local = _scan_operands(q, f, v, isub)
                for i in range(N_OPS):
                    used = slice(0, SUBLANES * len(lhs[i]))
                    lhs_ref[ci, hd, i, used] = cat(lhs[i]).astype(jnp.bfloat16)
                    if rhs[i] is not None:
                        rhs_ref[ci, hd, i, used] = cat(rhs[i]).astype(jnp.bfloat16)
                vb_ref[ci, hd] = cat(v).astype(jnp.bfloat16)
                tot_ref[ci, hd] = _last_row(total)
                loc_ref[ci, hd] = cat(local)

        r2 = _token_of_row(lax.broadcasted_iota(jnp.int32, (CHUNK, CHUNK), 0))
        c2 = _token_of_row(lax.broadcasted_iota(jnp.int32, (CHUNK, CHUNK), 1))
        pair_of = [((r2 >> (bit + 1)) == (c2 >> (bit + 1))) & (((r2 >> bit) & 1) == 1) & (((c2 >> bit) & 1) == 0)
                   for bit in range(TIME_BITS)]
        hng = hng_ref[...]

        def output_chunk(ci):
            rows_of = tile_rows(ci)
            heads = range(HEADS)

            def nt(hd, i):
                both = lhs_ref[ci, hd, i]
                return lax.dot_general(both, both, NT, preferred_element_type=jnp.float32)

            scores = [None] * HEADS
            for bit in range(LOCAL_BITS, TIME_BITS - 1):
                for hd in heads:
                    p = jnp.where(pair_of[bit], nt(hd, bit - LOCAL_BITS), 0.0)
                    scores[hd] = p if scores[hd] is None else scores[hd] + p
            half = CHUNK // 2
            top = [lax.dot_general(lhs_ref[ci, hd, N_OPS - 2, 0:half], rhs_ref[ci, hd, N_OPS - 2, 0:half], NT,
                                   preferred_element_type=jnp.float32) for hd in heads]
            state_t = [state_ref[hd] for hd in heads]
            o_inter = [lax.dot_general(lhs_ref[ci, hd, N_OPS - 1], state_t[hd].astype(jnp.bfloat16), NT,
                                       preferred_element_type=jnp.float32) for hd in heads]
            for hd in heads:
                state_ref[hd] = (state_t[hd] * tot_ref[ci, hd][0:1, :]
                                 + lax.dot_general(vb_ref[ci, hd], rhs_ref[ci, hd, N_OPS - 1], TN,
                                                   preferred_element_type=jnp.float32))
            for hd in heads:
                o = (o_inter[hd] + loc_ref[ci, hd]
                     + jnp.dot(scores[hd].astype(jnp.bfloat16), vb_ref[ci, hd],
                               preferred_element_type=jnp.float32))
                o_top = jnp.dot(top[hd].astype(jnp.bfloat16), vb_ref[ci, hd, 0:half],
                                preferred_element_type=jnp.float32)
                o = jnp.concatenate([o[0:half], o[half:] + o_top], axis=0)
                for t, rows in enumerate(rows_of):
                    o_tile = o[SUBLANES * t:SUBLANES * (t + 1), :]
                    ob_ref[hd, rows, :] = (_rms(o_tile) * hng) * g_ref[hd, rows, :]

        for ci in range(ts // CHUNK):
            scan_chunk(ci)
        for ci in range(ts // CHUNK):
            output_chunk(ci)
        gated = jnp.concatenate([ob_ref[hd] for hd in range(HEADS)], axis=-1)
        branch_b = _bdot(gated, who_ref[...])

        merged = _sigmoid(proj(SEC_MGP)) * branch_a + _sigmoid(proj(SEC_MGH)) * branch_b
        y = _bdot(merged, wout_ref[...])
        o_ref[pl.ds(r0, SEQ_TILE), :] = x + _rms(y) * (gate * gpost_ref[...])
        return carry

    lax.fori_loop(0, GRID_TILE // SEQ_TILE, tile_pass, 0)


def _resident(shape, index_map):
    return pl.BlockSpec(shape, index_map, pipeline_mode=pl.Buffered(1))


def _nbytes(shape, dtype):
    return math.prod(shape) * jnp.dtype(dtype).itemsize


def _layer_call(layer, x, ada, g_pre, g_post, lb_logits, pool_scale, hgrn_norm_g,
                w_in, pool_w, w_pool_o, w_hgrn_o, w_out):
    batch, seq, d = x.shape
    depth = lb_logits.shape[0]
    ts = SEQ_TILE
    assert seq % GRID_TILE == 0 and GRID_TILE % ts == 0 and ts % CHUNK == 0 and d == D_MODEL
    chunks = ts // CHUNK

    def const2(b, s):
        return (0, 0)

    def per_layer_row(a):
        return a.reshape(depth, 1, a.shape[-1])

    def per_layer3(b, s):
        return (layer, 0, 0)

    in_specs = [
        pl.BlockSpec((None, GRID_TILE, d), lambda b, s: (b, s, 0)),
        pl.BlockSpec((None, None, 3, d), lambda b, s: (layer, b, 0, 0)),
        _resident((None, 1, d), per_layer3),
        _resident((None, 1, d), per_layer3),
        _resident((depth, HGRN_WIDTH), const2),
        _resident((None, 1, POOL_WIDTH), per_layer3),
        _resident((None, 1, HEAD_DIM), per_layer3),
        _resident((None, d, sum(IN_SIZES)), per_layer3),
        _resident((None, POOL_GROUPS, POOL_GROUP_DIM, POOL_GROUP_DIM), lambda b, s: (layer, 0, 0, 0)),
        _resident((None, POOL_WIDTH, d), per_layer3),
        _resident((None, HGRN_WIDTH, d), per_layer3),
        _resident((None, d, d), per_layer3),
    ]
    scratch = [
        pltpu.VMEM((HEADS, ts, HEAD_DIM), jnp.float32),
        pltpu.VMEM((HEADS, ts, HEAD_DIM), jnp.float32),
        pltpu.VMEM((HEADS, ts, HEAD_DIM), jnp.float32),
        pltpu.VMEM((HEADS, ts, HEAD_DIM), jnp.float32),
        pltpu.VMEM((HEADS, ts, HEAD_DIM), jnp.float32),
        pltpu.VMEM((HALO + ts, POOL_WIDTH), jnp.float32),
        pltpu.VMEM((HEADS, HEAD_DIM, HEAD_DIM), jnp.float32),
        pltpu.VMEM((chunks, HEADS, N_OPS, CHUNK, HEAD_DIM), jnp.bfloat16),
        pltpu.VMEM((chunks, HEADS, N_OPS, CHUNK, HEAD_DIM), jnp.bfloat16),
        pltpu.VMEM((chunks, HEADS, CHUNK, HEAD_DIM), jnp.bfloat16),
        pltpu.VMEM((chunks, HEADS, SUBLANES, HEAD_DIM), jnp.float32),
        pltpu.VMEM((chunks, HEADS, CHUNK, HEAD_DIM), jnp.float32),
    ]
    weights = (w_in, pool_w, w_pool_o, w_hgrn_o, w_out)
    vmem_bytes = (sum(_nbytes(w.shape[1:], w.dtype) for w in weights)
                  + 2 * 2 * _nbytes((GRID_TILE, d), x.dtype)
                  + sum(_nbytes(s.shape, s.dtype) for s in scratch)
                  + COMPILER_TEMP_BYTES)
    assert vmem_bytes <= V7X_VMEM_BYTES, vmem_bytes
    return pl.pallas_call(
        functools.partial(_layer_kernel, layer=layer),
        grid=(batch, seq // GRID_TILE),
        in_specs=in_specs,
        out_specs=pl.BlockSpec((None, GRID_TILE, d), lambda b, s: (b, s, 0)),
        out_shape=jax.ShapeDtypeStruct(x.shape, x.dtype),
        scratch_shapes=scratch,
        compiler_params=pltpu.CompilerParams(
            dimension_semantics=("arbitrary", "arbitrary"),
            vmem_limit_bytes=vmem_bytes),
        name=f"hybrid_layer{layer}",
    )(x, ada, per_layer_row(g_pre), per_layer_row(g_post), lb_logits,
      per_layer_row(pool_scale), per_layer_row(hgrn_norm_g),
      w_in, pool_w, w_pool_o, w_hgrn_o, w_out)


def _ada_call(c, w_ada, b_ada):
    depth, d, width = w_ada.shape
    batch = c.shape[0]
    assert width % ADA_TILE == 0
    return pl.pallas_call(
        _ada_kernel,
        grid=(depth, width // ADA_TILE),
        in_specs=[
            pl.BlockSpec((batch, d), lambda l, n: (0, 0)),
            pl.BlockSpec((None, d, ADA_TILE), lambda l, n: (l, 0, n)),
            pl.BlockSpec((None, 1, ADA_TILE), lambda l, n: (l, 0, n)),
        ],
        out_specs=pl.BlockSpec((None, batch, ADA_TILE), lambda l, n: (l, 0, n)),
        out_shape=jax.ShapeDtypeStruct((depth, batch, width), jnp.float32),
        name="adaln_vectors",
    )(c, w_ada, b_ada.reshape(depth, 1, width))


def kernel(x, c, w_ada, b_ada, g_pre, g_post, w_in, pool_w, pool_scale, lb_logits,
           hgrn_norm_g, w_pool_o, w_hgrn_o, w_out):
    depth = w_in.shape[0]
    batch, _, d = x.shape
    ada = _ada_call(c, w_ada, b_ada).reshape(depth, batch, 3, d)
    bf = jnp.bfloat16
    w_in_b, pool_w_b = w_in.astype(bf), pool_w.astype(bf)
    w_pool_o_b, w_hgrn_o_b, w_out_b = w_pool_o.astype(bf), w_hgrn_o.astype(bf), w_out.astype(bf)
    for layer in range(depth):
        x = _layer_call(layer, x, ada, g_pre, g_post, lb_logits, pool_scale, hgrn_norm_g,
                        w_in_b, pool_w_b, w_pool_o_b, w_hgrn_o_b, w_out_b)
    return x
```

```python
import functools
import math

import jax
import jax.numpy as jnp
from jax import lax
from jax.experimental import pallas as pl
from jax.experimental.pallas import tpu as pltpu

D_MODEL = 1024
POOL_WINDOWS = (2, 4, 8, 16)
POOL_GROUPS = len(POOL_WINDOWS)
POOL_GROUP_DIM = D_MODEL // 8
POOL_WIDTH = POOL_GROUPS * POOL_GROUP_DIM
HEAD_DIM = 128
HEADS = D_MODEL // HEAD_DIM
HGRN_WIDTH = HEADS * HEAD_DIM
NORM_EPS = 1e-6
F_FLOOR = 1e-30
NEG_LOG2_E = -1.4426950408889634
IN_SIZES = (POOL_WIDTH, POOL_WIDTH, HGRN_WIDTH, HGRN_WIDTH, HGRN_WIDTH, HGRN_WIDTH, D_MODEL, D_MODEL)
IN_OFFSETS = tuple(sum(IN_SIZES[:i]) for i in range(len(IN_SIZES)))
(SEC_PV, SEC_PG, SEC_HQ, SEC_HF, SEC_HI, SEC_HG, SEC_MGP, SEC_MGH) = range(len(IN_SIZES))

GRID_TILE = 512
SEQ_TILE = 256
CHUNK = 64
HALO = max(POOL_WINDOWS)
SUBLANES = 8
TILES = CHUNK // SUBLANES
TILE_BITS = (0, 1, 5)
SUBLANE_BITS = (2, 3, 4)
TIME_BITS = 6
TILES_PER_GROUP = 4
ROW_STRIDE = TILES_PER_GROUP
GROUP_TOKENS = TILES_PER_GROUP * SUBLANES
assert CHUNK == 1 << TIME_BITS
LOCAL_BITS = 2
assert TILE_BITS[:LOCAL_BITS] == tuple(range(LOCAL_BITS)) and TILES_PER_GROUP == 1 << LOCAL_BITS
N_OPS = TIME_BITS - LOCAL_BITS + 1
ADA_TILE = 1024
V7X_VMEM_BYTES = 64 * 1024 * 1024
COMPILER_TEMP_BYTES = 3 * SEQ_TILE * D_MODEL * 4 * 4

NT = (((1,), (1,)), ((), ()))
TN = (((0,), (0,)), ((), ()))


def _bdot(a, b):
    return jnp.dot(a.astype(jnp.bfloat16), b.astype(jnp.bfloat16),
                   preferred_element_type=jnp.float32)


def _sigmoid(z):
    return 1.0 / (1.0 + jnp.exp2(z * NEG_LOG2_E))


def _silu(z):
    return z * _sigmoid(z)


def _rms(x):
    return x * lax.rsqrt(jnp.mean(x * x, axis=-1, keepdims=True) + NORM_EPS)


def _roll_rows(a, shift):
    return pltpu.roll(a, shift % a.shape[0], axis=0)


def _token_of_row(p):
    group, j, i = p // GROUP_TOKENS, (p // SUBLANES) % TILES_PER_GROUP, p % SUBLANES
    return group * GROUP_TOKENS + i * ROW_STRIDE + j


def _ada_kernel(c_ref, w_ref, b_ref, o_ref):
    c = c_ref[...]
    o_ref[...] = jnp.dot(_silu(c), w_ref[...], preferred_element_type=jnp.float32) + b_ref[...]


def _sublane_partner(p, sb, isub):
    k = 1
    while k < sb:
        p = jnp.where((isub & k) != 0, p, _roll_rows(p, -k))
        k *= 2
    if 2 * sb == p.shape[0]:
        return _roll_rows(p, sb)
    return jnp.where((isub & sb) != 0, _roll_rows(p, sb), _roll_rows(p, -sb))


def _last_row(a):
    return jnp.broadcast_to(a[SUBLANES - 1:SUBLANES, :], a.shape)


def _lane_sum(a):
    return jnp.sum(a, axis=-1, keepdims=True)


def _local_outputs(q, pf, kk, v):
    out = []
    for base in range(0, TILES, TILES_PER_GROUP):
        qs, fs, ks, vs = (a[base:base + TILES_PER_GROUP] for a in (q, pf, kk, v))
        acc = [_lane_sum(qs[j] * ks[j]) * vs[j] for j in range(TILES_PER_GROUP)]
        decayed_k = {}
        for j_t in range(1, TILES_PER_GROUP):
            qf = qs[j_t] * fs[j_t]
            for j_s in range(j_t - 1, -1, -1):
                decayed_k[j_t, j_s] = (ks[j_s] if j_s == j_t - 1
                                       else decayed_k[j_t - 1, j_s] * fs[j_t - 1])
                acc[j_t] = acc[j_t] + _lane_sum(qf * decayed_k[j_t, j_s]) * vs[j_s]
        out.extend(acc)
    return out


def _scan_operands(q, f, v, isub):
    tiles = range(TILES)
    kk = [1.0 - f[t] for t in tiles]
    pre = [jnp.maximum(f[t], F_FLOOR) for t in tiles]
    ks = list(kk)
    local = _local_outputs(q, pre, kk, v)
    lhs, rhs = [], []
    for bit in range(TIME_BITS):
        if bit in TILE_BITS:
            tb = 1 << TILE_BITS.index(bit)
            if bit >= LOCAL_BITS:
                lhs.append([q[t] * pre[t] for t in tiles if t & tb])
                rhs.append([ks[t] for t in tiles if not t & tb])
        else:
            sb = 1 << SUBLANE_BITS.index(bit)
            upper = (isub & sb) != 0
            lhs.append([jnp.where(upper, q[t] * pre[t], ks[t]) for t in tiles])
            rhs.append(None)
        new_pre, new_ks = list(pre), list(ks)
        if bit in TILE_BITS:
            totals = {}
            for t in tiles:
                src = (t ^ tb) | (tb - 1)
                if src not in totals:
                    totals[src] = _last_row(pre[src]) if bit > max(SUBLANE_BITS) else pre[src]
                if t & tb:
                    new_pre[t] = pre[t] * totals[src]
                else:
                    new_ks[t] = ks[t] * totals[src]
        else:
            for base in range(0, TILES, TILES_PER_GROUP):
                partner = _sublane_partner(pre[base + TILES_PER_GROUP - 1], sb, isub)
                pre_mul = jnp.where(upper, partner, 1.0)
                ks_mul = jnp.where(upper, 1.0, partner)
                for t in range(base, base + TILES_PER_GROUP):
                    new_pre[t] = pre[t] * pre_mul
                    new_ks[t] = ks[t] * ks_mul
        pre, ks = new_pre, new_ks
    lhs.append([q[t] * pre[t] for t in tiles])
    rhs.append(ks)
    return lhs, rhs, pre[TILES - 1], local


def _layer_kernel(x_ref, ada_ref, gpre_ref, gpost_ref, lbl_ref, pscale_ref, hng_ref,
                  win_ref, poolw_ref, wpo_ref, who_ref, wout_ref, o_ref,
                  q_ref, f_ref, v_ref, g_ref, ob_ref, ext_ref, state_ref,
                  lhs_ref, rhs_ref, vb_ref, tot_ref, loc_ref, *, layer):
    ts = SEQ_TILE
    si = pl.program_id(1)

    @pl.when(si == 0)
    def _():
        state_ref[...] = jnp.zeros_like(state_ref)
        ext_ref[0:HALO, :] = jnp.zeros((HALO, POOL_WIDTH), jnp.float32)

    def tile_pass(part, carry):
        r0 = pl.multiple_of(part * SEQ_TILE, SEQ_TILE)
        x = x_ref[pl.ds(r0, SEQ_TILE), :]
        shift = ada_ref[0:1, :]
        scale = ada_ref[1:2, :]
        gate = ada_ref[2:3, :]
        hb = (_rms(x) * (gpre_ref[...] * (1.0 + scale)) + shift).astype(jnp.bfloat16)

        def proj(section):
            cols = slice(IN_OFFSETS[section], IN_OFFSETS[section] + IN_SIZES[section])
            return jnp.dot(hb, win_ref[:, cols], preferred_element_type=jnp.float32)

        logits = lbl_ref[...]
        e = jnp.exp(logits - jnp.max(logits, axis=0, keepdims=True))
        lb = jnp.zeros((1, HGRN_WIDTH), jnp.float32)
        for j in range(1, layer + 1):
            lb = lb + e[j:j + 1, :]
        lb = jnp.clip(lb / jnp.sum(e, axis=0, keepdims=True), 0.0, 1.0)

        def to_heads(ref, a):
            for hd in range(HEADS):
                ref[hd] = a[:, hd * HEAD_DIM:(hd + 1) * HEAD_DIM]

        to_heads(q_ref, _silu(proj(SEC_HQ)))
        to_heads(f_ref, lb + (1.0 - lb) * _sigmoid(proj(SEC_HF)))
        to_heads(v_ref, proj(SEC_HI))
        to_heads(g_ref, _silu(proj(SEC_HG)))

        u = proj(SEC_PV)
        ext_ref[HALO:HALO + ts, :] = u
        first = si * GRID_TILE + r0 + 1
        pos = (first + lax.broadcasted_iota(jnp.int32, (ts, 1), 0)).astype(jnp.float32)
        mixed = []
        for gi, w in enumerate(POOL_WINDOWS):
            cols = slice(gi * POOL_GROUP_DIM, (gi + 1) * POOL_GROUP_DIM)
            acc = ext_ref[:, cols]
            step = 1
            while step < w:
                acc = acc + _roll_rows(acc, step)
                step *= 2
            inv_count = 1.0 / jnp.minimum(pos, float(w))
            pooled = acc[HALO:, :] * inv_count - u[:, cols]
            mixed.append(_bdot(pooled, poolw_ref[gi]))
        ext_ref[0:HALO, :] = u[ts - HALO:, :]
        pooled = jnp.concatenate(mixed, axis=-1) * pscale_ref[...]
        branch_a = _bdot(pooled * _silu(proj(SEC_PG)), wpo_ref[...])

        isub = lax.broadcasted_iota(jnp.int32, (SUBLANES, HEAD_DIM), 0)
        cat = functools.partial(jnp.concatenate, axis=0)

        def tile_rows(ci):
            return [pl.ds(ci * CHUNK + GROUP_TOKENS * (t // TILES_PER_GROUP) + t % TILES_PER_GROUP, SUBLANES,
                          stride=ROW_STRIDE) for t in range(TILES)]

        def scan_chunk(ci):
            rows_of = tile_rows(ci)
            for hd in range(HEADS):
                q, f, v = ([ref[hd, rows, :] for rows in rows_of] for ref in (q_ref, f_ref, v_ref))
                lhs, rhs, total, local = _scan_operands(q, f, v, isub)
                for i in range(N_OPS):
                    used = slice(0, SUBLANES * len(lhs[i]))
                    lhs_ref[ci, hd, i, used] = cat(lhs[i]).astype(jnp.bfloat16)
                    if rhs[i] is not None:
                        rhs_ref[ci, hd, i, used] = cat(rhs[i]).astype(jnp.bfloat16)
                vb_ref[ci, hd] = cat(v).astype(jnp.bfloat16)
                tot_ref[ci, hd] = _last_row(total)
                loc_ref[ci, hd] = cat(local)

        r2 = _token_of_row(lax.broadcasted_iota(jnp.int32, (CHUNK, CHUNK), 0))
        c2 = _token_of_row(lax.broadcasted_iota(jnp.int32, (CHUNK, CHUNK), 1))
        pair_of = [((r2 >> (bit + 1)) == (c2 >> (bit + 1))) & (((r2 >> bit) & 1) == 1) & (((c2 >> bit) & 1) == 0)
                   for bit in range(TIME_BITS)]
        hng = hng_ref[...]

        def output_chunk(ci):
            rows_of = tile_rows(ci)
            heads = range(HEADS)

            def nt(hd, i):
                both = lhs_ref[ci, hd, i]
                return lax.dot_general(both, both, NT, preferred_element_type=jnp.float32)

            scores = [None] * HEADS
            for bit in range(LOCAL_BITS, TIME_BITS - 1):
                for hd in heads:
                    p = jnp.where(pair_of[bit], nt(hd, bit - LOCAL_BITS), 0.0)
                    scores[hd] = p if scores[hd] is None else scores[hd] + p
            half = CHUNK // 2
            top = [lax.dot_general(lhs_ref[ci, hd, N_OPS - 2, 0:half], rhs_ref[ci, hd, N_OPS - 2, 0:half], NT,
                                   preferred_element_type=jnp.float32) for hd in heads]
            state_t = [state_ref[hd] for hd in heads]
            o_inter = [lax.dot_general(lhs_ref[ci, hd, N_OPS - 1], state_t[hd].astype(jnp.bfloat16), NT,
                                       preferred_element_type=jnp.float32) for hd in heads]
            for hd in heads:
                o = (o_inter[hd] + loc_ref[ci, hd]
                     + jnp.dot(scores[hd].astype(jnp.bfloat16), vb_ref[ci, hd],
                               preferred_element_type=jnp.float32))
                o_top = jnp.dot(top[hd].astype(jnp.bfloat16), vb_ref[ci, hd, 0:half],
                                preferred_element_type=jnp.float32)
                o = jnp.concatenate([o[0:half], o[half:] + o_top], axis=0)
                for t, rows in enumerate(rows_of):
                    o_tile = o[SUBLANES * t:SUBLANES * (t + 1), :]
                    ob_ref[hd, rows, :] = (_rms(o_tile) * hng) * g_ref[hd, rows, :]
            for hd in heads:
                state_ref[hd] = (state_t[hd] * tot_ref[ci, hd][0:1, :]
                                 + lax.dot_general(vb_ref[ci, hd], rhs_ref[ci, hd, N_OPS - 1], TN,
                                                   preferred_element_type=jnp.float32))

        for ci in range(ts // CHUNK):
            scan_chunk(ci)
        for ci in range(ts // CHUNK):
            output_chunk(ci)
        gated = jnp.concatenate([ob_ref[hd] for hd in range(HEADS)], axis=-1)
        branch_b = _bdot(gated, who_ref[...])

        merged = _sigmoid(proj(SEC_MGP)) * branch_a + _sigmoid(proj(SEC_MGH)) * branch_b
        y = _bdot(merged, wout_ref[...])
        o_ref[pl.ds(r0, SEQ_TILE), :] = x + _rms(y) * (gate * gpost_ref[...])
        return carry

    lax.fori_loop(0, GRID_TILE // SEQ_TILE, tile_pass, 0)


def _resident(shape, index_map):
    return pl.BlockSpec(shape, index_map, pipeline_mode=pl.Buffered(1))


def _nbytes(shape, dtype):
    return math.prod(shape) * jnp.dtype(dtype).itemsize


def _layer_call(layer, x, ada, g_pre, g_post, lb_logits, pool_scale, hgrn_norm_g,
                w_in, pool_w, w_pool_o, w_hgrn_o, w_out):
    batch, seq, d = x.shape
    depth = lb_logits.shape[0]
    ts = SEQ_TILE
    assert seq % GRID_TILE == 0 and GRID_TILE % ts == 0 and ts % CHUNK == 0 and d == D_MODEL
    chunks = ts // CHUNK

    def const2(b, s):
        return (0, 0)

    def per_layer_row(a):
        return a.reshape(depth, 1, a.shape[-1])

    def per_layer3(b, s):
        return (layer, 0, 0)

    in_specs = [
        pl.BlockSpec((None, GRID_TILE, d), lambda b, s: (b, s, 0)),
        pl.BlockSpec((None, None, 3, d), lambda b, s: (layer, b, 0, 0)),
        _resident((None, 1, d), per_layer3),
        _resident((None, 1, d), per_layer3),
        _resident((depth, HGRN_WIDTH), const2),
        _resident((None, 1, POOL_WIDTH), per_layer3),
        _resident((None, 1, HEAD_DIM), per_layer3),
        _resident((None, d, sum(IN_SIZES)), per_layer3),
        _resident((None, POOL_GROUPS, POOL_GROUP_DIM, POOL_GROUP_DIM), lambda b, s: (layer, 0, 0, 0)),
        _resident((None, POOL_WIDTH, d), per_layer3),
        _resident((None, HGRN_WIDTH, d), per_layer3),
        _resident((None, d, d), per_layer3),
    ]
    scratch = [
        pltpu.VMEM((HEADS, ts, HEAD_DIM), jnp.float32),
        pltpu.VMEM((HEADS, ts, HEAD_DIM), jnp.float32),
        pltpu.VMEM((HEADS, ts, HEAD_DIM), jnp.float32),
        pltpu.VMEM((HEADS, ts, HEAD_DIM), jnp.float32),
        pltpu.VMEM((HEADS, ts, HEAD_DIM), jnp.float32),
        pltpu.VMEM((HALO + ts, POOL_WIDTH), jnp.float32),
        pltpu.VMEM((HEADS, HEAD_DIM, HEAD_DIM), jnp.float32),
        pltpu.VMEM((chunks, HEADS, N_OPS, CHUNK, HEAD_DIM), jnp.bfloat16),
        pltpu.VMEM((chunks, HEADS, N_OPS, CHUNK, HEAD_DIM), jnp.bfloat16),
        pltpu.VMEM((chunks, HEADS, CHUNK, HEAD_DIM), jnp.bfloat16),
        pltpu.VMEM((chunks, HEADS, SUBLANES, HEAD_DIM), jnp.float32),
        pltpu.VMEM((chunks, HEADS, CHUNK, HEAD_DIM), jnp.float32),
    ]
    weights = (w_in, pool_w, w_pool_o, w_hgrn_o, w_out)
    vmem_bytes = (sum(_nbytes(w.shape[1:], w.dtype) for w in weights)
                  + 2 * 2 * _nbytes((GRID_TILE, d), x.dtype)
                  + sum(_nbytes(s.shape, s.dtype) for s in scratch)
                  + COMPILER_TEMP_BYTES)
    assert vmem_bytes <= V7X_VMEM_BYTES, vmem_bytes
    return pl.pallas_call(
        functools.partial(_layer_kernel, layer=layer),
        grid=(batch, seq // GRID_TILE),
        in_specs=in_specs,
        out_specs=pl.BlockSpec((None, GRID_TILE, d), lambda b, s: (b, s, 0)),
        out_shape=jax.ShapeDtypeStruct(x.shape, x.dtype),
        scratch_shapes=scratch,
        compiler_params=pltpu.CompilerParams(
            dimension_semantics=("arbitrary", "arbitrary"),
            vmem_limit_bytes=vmem_bytes),
        name=f"hybrid_layer{layer}",
    )(x, ada, per_layer_row(g_pre), per_layer_row(g_post), lb_logits,
      per_layer_row(pool_scale), per_layer_row(hgrn_norm_g),
      w_in, pool_w, w_pool_o, w_hgrn_o, w_out)


def _ada_call(c, w_ada, b_ada):
    depth, d, width = w_ada.shape
    batch = c.shape[0]
    assert width % ADA_TILE == 0
    return pl.pallas_call(
        _ada_kernel,
        grid=(depth, width // ADA_TILE),
        in_specs=[
            pl.BlockSpec((batch, d), lambda l, n: (0, 0)),
            pl.BlockSpec((None, d, ADA_TILE), lambda l, n: (l, 0, n)),
            pl.BlockSpec((None, 1, ADA_TILE), lambda l, n: (l, 0, n)),
        ],
        out_specs=pl.BlockSpec((None, batch, ADA_TILE), lambda l, n: (l, 0, n)),
        out_shape=jax.ShapeDtypeStruct((depth, batch, width), jnp.float32),
        name="adaln_vectors",
    )(c, w_ada, b_ada.reshape(depth, 1, width))


def kernel(x, c, w_ada, b_ada, g_pre, g_post, w_in, pool_w, pool_scale, lb_logits,
           hgrn_norm_g, w_pool_o, w_hgrn_o, w_out):
    depth = w_in.shape[0]
    batch, _, d = x.shape
    ada = _ada_call(c, w_ada, b_ada).reshape(depth, batch, 3, d)
    bf = jnp.bfloat16
    w_in_b, pool_w_b = w_in.astype(bf), pool_w.astype(bf)
    w_pool_o_b, w_hgrn_o_b, w_out_b = w_pool_o.astype(bf), w_hgrn_o.astype(bf), w_out.astype(bf)
    for layer in range(depth):
        x = _layer_call(layer, x, ada, g_pre, g_post, lb_logits, pool_scale, hgrn_norm_g,
                        w_in_b, pool_w_b, w_pool_o_b, w_hgrn_o_b, w_out_b)
    return x
```

```python
import functools
import math

import jax
import jax.numpy as jnp
from jax import lax
from jax.experimental import pallas as pl
from jax.experimental.pallas import tpu as pltpu

D_MODEL = 1024
POOL_WINDOWS = (2, 4, 8, 16)
POOL_GROUPS = len(POOL_WINDOWS)
POOL_GROUP_DIM = D_MODEL // 8
POOL_WIDTH = POOL_GROUPS * POOL_GROUP_DIM
HEAD_DIM = 128
HEADS = D_MODEL // HEAD_DIM
HGRN_WIDTH = HEADS * HEAD_DIM
NORM_EPS = 1e-6
F_FLOOR = 1e-30
NEG_LOG2_E = -1.4426950408889634
IN_SIZES = (POOL_WIDTH, POOL_WIDTH, HGRN_WIDTH, HGRN_WIDTH, HGRN_WIDTH, HGRN_WIDTH, D_MODEL, D_MODEL)
IN_OFFSETS = tuple(sum(IN_SIZES[:i]) for i in range(len(IN_SIZES)))
(SEC_PV, SEC_PG, SEC_HQ, SEC_HF, SEC_HI, SEC_HG, SEC_MGP, SEC_MGH) = range(len(IN_SIZES))

GRID_TILE = 512
SEQ_TILE = 256
CHUNK = 64
HALO = max(POOL_WINDOWS)
SUBLANES = 8
TILES = CHUNK // SUBLANES
TILE_BITS = (0, 1, 5)
SUBLANE_BITS = (2, 3, 4)
TIME_BITS = 6
TILES_PER_GROUP = 4
ROW_STRIDE = TILES_PER_GROUP
GROUP_TOKENS = TILES_PER_GROUP * SUBLANES
assert CHUNK == 1 << TIME_BITS
LOCAL_BITS = 2
assert TILE_BITS[:LOCAL_BITS] == tuple(range(LOCAL_BITS)) and TILES_PER_GROUP == 1 << LOCAL_BITS
N_OPS = TIME_BITS - LOCAL_BITS + 1
ADA_TILE = 1024
V7X_VMEM_BYTES = 64 * 1024 * 1024
COMPILER_TEMP_BYTES = 3 * SEQ_TILE * D_MODEL * 4 * 4

NT = (((1,), (1,)), ((), ()))
TN = (((0,), (0,)), ((), ()))


def _bdot(a, b):
    return jnp.dot(a.astype(jnp.bfloat16), b.astype(jnp.bfloat16),
                   preferred_element_type=jnp.float32)


def _sigmoid(z):
    return 1.0 / (1.0 + jnp.exp2(z * NEG_LOG2_E))


def _silu(z):
    return z * _sigmoid(z)


def _rms(x):
    return x * lax.rsqrt(jnp.mean(x * x, axis=-1, keepdims=True) + NORM_EPS)


def _roll_rows(a, shift):
    return pltpu.roll(a, shift % a.shape[0], axis=0)


def _token_of_row(p):
    group, j, i = p // GROUP_TOKENS, (p // SUBLANES) % TILES_PER_GROUP, p % SUBLANES
    return group * GROUP_TOKENS + i * ROW_STRIDE + j


def _ada_kernel(c_ref, w_ref, b_ref, o_ref):
    c = c_ref[...]
    o_ref[...] = jnp.dot(_silu(c), w_ref[...], preferred_element_type=jnp.float32) + b_ref[...]


def _sublane_partner(p, sb, isub):
    k = 1
    while k < sb:
        p = jnp.where((isub & k) != 0, p, _roll_rows(p, -k))
        k *= 2
    if 2 * sb == p.shape[0]:
        return _roll_rows(p, sb)
    return jnp.where((isub & sb) != 0, _roll_rows(p, sb), _roll_rows(p, -sb))


def _last_row(a):
    return jnp.broadcast_to(a[SUBLANES - 1:SUBLANES, :], a.shape)


def _lane_sum(a):
    return jnp.sum(a, axis=-1, keepdims=True)


def _local_outputs(q, pf, kk, v):
    out = []
    for base in range(0, TILES, TILES_PER_GROUP):
        qs, fs, ks, vs = (a[base:base + TILES_PER_GROUP] for a in (q, pf, kk, v))
        acc = [_lane_sum(qs[j] * ks[j]) * vs[j] for j in range(TILES_PER_GROUP)]
        decayed_k = {}
        for j_t in range(1, TILES_PER_GROUP):
            qf = qs[j_t] * fs[j_t]
            for j_s in range(j_t - 1, -1, -1):
                decayed_k[j_t, j_s] = (ks[j_s] if j_s == j_t - 1
                                       else decayed_k[j_t - 1, j_s] * fs[j_t - 1])
                acc[j_t] = acc[j_t] + _lane_sum(qf * decayed_k[j_t, j_s]) * vs[j_s]
        out.extend(acc)
    return out


def _scan_operands(q, f, v, isub):
    tiles = range(TILES)
    kk = [1.0 - f[t] for t in tiles]
    pre = [jnp.maximum(f[t], F_FLOOR) for t in tiles]
    ks = list(kk)
    local = _local_outputs(q, pre, kk, v)
    lhs, rhs = [], []
    for bit in range(TIME_BITS):
        if bit in TILE_BITS:
            tb = 1 << TILE_BITS.index(bit)
            if bit >= LOCAL_BITS:
                lhs.append([q[t] * pre[t] for t in tiles if t & tb])
                rhs.append([ks[t] for t in tiles if not t & tb])
        else:
            sb = 1 << SUBLANE_BITS.index(bit)
            upper = (isub & sb) != 0
            lhs.append([jnp.where(upper, q[t] * pre[t], ks[t]) for t in tiles])
            rhs.append(None)
        new_pre, new_ks = list(pre), list(ks)
        if bit in TILE_BITS:
            totals = {}
            for t in tiles:
                src = (t ^ tb) | (tb - 1)
                if src not in totals:
                    totals[src] = _last_row(pre[src]) if bit > max(SUBLANE_BITS) else pre[src]
                if t & tb:
                    new_pre[t] = pre[t] * totals[src]
                else:
                    new_ks[t] = ks[t] * totals[src]
        else:
            for base in range(0, TILES, TILES_PER_GROUP):
                partner = _sublane_partner(pre[base + TILES_PER_GROUP - 1], sb, isub)
                pre_mul = jnp.where(upper, partner, 1.0)
                ks_mul = jnp.where(upper, 1.0, partner)
                for t in range(base, base + TILES_PER_GROUP):
                    new_pre[t] = pre[t] * pre_mul
                    new_ks[t] = ks[t] * ks_mul
        pre, ks = new_pre, new_ks
    lhs.append([q[t] * pre[t] for t in tiles])
    rhs.append(ks)
    return lhs, rhs, pre[TILES - 1], local


def _layer_kernel(x_ref, ada_ref, gpre_ref, gpost_ref, lbl_ref, pscale_ref, hng_ref,
                  win_ref, poolw_ref, wpo_ref, who_ref, wout_ref, o_ref,
                  q_ref, f_ref, v_ref, g_ref, ob_ref, ext_ref, state_ref,
                  lhs_ref, rhs_ref, vb_ref, tot_ref, loc_ref, *, layer):
    ts = SEQ_TILE
    si = pl.program_id(1)

    @pl.when(si == 0)
    def _():
        state_ref[...] = jnp.zeros_like(state_ref)
        ext_ref[0:HALO, :] = jnp.zeros((HALO, POOL_WIDTH), jnp.float32)

    def tile_pass(part, carry):
        r0 = pl.multiple_of(part * SEQ_TILE, SEQ_TILE)
        x = x_ref[pl.ds(r0, SEQ_TILE), :]
        shift = ada_ref[0:1, :]
        scale = ada_ref[1:2, :]
        gate = ada_ref[2:3, :]
        hb = (_rms(x) * (gpre_ref[...] * (1.0 + scale)) + shift).astype(jnp.bfloat16)

        def proj(section):
            cols = slice(IN_OFFSETS[section], IN_OFFSETS[section] + IN_SIZES[section])
            return jnp.dot(hb, win_ref[:, cols], preferred_element_type=jnp.float32)

        logits = lbl_ref[...]
        e = jnp.exp(logits - jnp.max(logits, axis=0, keepdims=True))
        lb = jnp.zeros((1, HGRN_WIDTH), jnp.float32)
        for j in range(1, layer + 1):
            lb = lb + e[j:j + 1, :]
        lb = jnp.clip(lb / jnp.sum(e, axis=0, keepdims=True), 0.0, 1.0)

        def to_heads(ref, a):
            for hd in range(HEADS):
                ref[hd] = a[:, hd * HEAD_DIM:(hd + 1) * HEAD_DIM]

        to_heads(q_ref, _silu(proj(SEC_HQ)))
        to_heads(f_ref, lb + (1.0 - lb) * _sigmoid(proj(SEC_HF)))
        to_heads(v_ref, proj(SEC_HI))
        to_heads(g_ref, _silu(proj(SEC_HG)))

        u = proj(SEC_PV)
        ext_ref[HALO:HALO + ts, :] = u
        first = si * GRID_TILE + r0 + 1
        pos = (first + lax.broadcasted_iota(jnp.int32, (ts, 1), 0)).astype(jnp.float32)
        mixed = []
        for gi, w in enumerate(POOL_WINDOWS):
            cols = slice(gi * POOL_GROUP_DIM, (gi + 1) * POOL_GROUP_DIM)
            acc = ext_ref[:, cols]
            step = 1
            while step < w:
                acc = acc + _roll_rows(acc, step)
                step *= 2
            inv_count = 1.0 / jnp.minimum(pos, float(w))
            pooled = acc[HALO:, :] * inv_count - u[:, cols]
            mixed.append(_bdot(pooled, poolw_ref[gi]))
        ext_ref[0:HALO, :] = u[ts - HALO:, :]
        pooled = jnp.concatenate(mixed, axis=-1) * pscale_ref[...]
        branch_a = _bdot(pooled * _silu(proj(SEC_PG)), wpo_ref[...])

        isub = lax.broadcasted_iota(jnp.int32, (SUBLANES, HEAD_DIM), 0)
        cat = functools.partial(jnp.concatenate, axis=0)

        def tile_rows(ci):
            return [pl.ds(ci * CHUNK + GROUP_TOKENS * (t // TILES_PER_GROUP) + t % TILES_PER_GROUP, SUBLANES,
                          stride=ROW_STRIDE) for t in range(TILES)]

        def scan_chunk(ci):
            rows_of = tile_rows(ci)
            for hd in range(HEADS):
                q, f, v = ([ref[hd, rows, :] for rows in rows_of] for ref in (q_ref, f_ref, v_ref))
                lhs, rhs, total, local = _scan_operands(q, f, v, isub)
                for i in range(N_OPS):
                    used = slice(0, SUBLANES * len(lhs[i]))
                    lhs_ref[ci, hd, i, used] = cat(lhs[i]).astype(jnp.bfloat16)
                    if rhs[i] is not None:
                        rhs_ref[ci, hd, i, used] = cat(rhs[i]).astype(jnp.bfloat16)
                vb_ref[ci, hd] = cat(v).astype(jnp.bfloat16)
                tot_ref[ci, hd] = _last_row(total)
                loc_ref[ci, hd] = cat(local)

        r2 = _token_of_row(lax.broadcasted_iota(jnp.int32, (CHUNK, CHUNK), 0))
        c2 = _token_of_row(lax.broadcasted_iota(jnp.int32, (CHUNK, CHUNK), 1))
        pair_of = [((r2 >> (bit + 1)) == (c2 >> (bit + 1))) & (((r2 >> bit) & 1) == 1) & (((c2 >> bit) & 1) == 0)
                   for bit in range(TIME_BITS)]
        hng = hng_ref[...]

        def output_chunk(ci):
            rows_of = tile_rows(ci)
            heads = range(HEADS)

            def nt(hd, i):
                both = lhs_ref[ci, hd, i]
                return lax.dot_general(both, both, NT, preferred_element_type=jnp.float32)

            scores = [None] * HEADS
            for bit in range(LOCAL_BITS, TIME_BITS - 1):
                for hd in heads:
                    scores[hd] = jnp.where(pair_of[bit], nt(hd, bit - LOCAL_BITS),
                                           0.0 if scores[hd] is None else scores[hd])
            half = CHUNK // 2
            top = [lax.dot_general(lhs_ref[ci, hd, N_OPS - 2, 0:half], rhs_ref[ci, hd, N_OPS - 2, 0:half], NT,
                                   preferred_element_type=jnp.float32) for hd in heads]
            state_t = [state_ref[hd] for hd in heads]
            o_inter = [lax.dot_general(lhs_ref[ci, hd, N_OPS - 1], state_t[hd].astype(jnp.bfloat16), NT,
                                       preferred_element_type=jnp.float32) for hd in heads]
            for hd in heads:
                state_ref[hd] = (state_t[hd] * tot_ref[ci, hd][0:1, :]
                                 + lax.dot_general(vb_ref[ci, hd], rhs_ref[ci, hd, N_OPS - 1], TN,
                                                   preferred_element_type=jnp.float32))
            for hd in heads:
                o = (o_inter[hd] + loc_ref[ci, hd]
                     + jnp.dot(scores[hd].astype(jnp.bfloat16), vb_ref[ci, hd],
                               preferred_element_type=jnp.float32))
                o_top = jnp.dot(top[hd].astype(jnp.bfloat16), vb_ref[ci, hd, 0:half],
                                preferred_element_type=jnp.float32)
                o = jnp.concatenate([o[0:half], o[half:] + o_top], axis=0)
                for t, rows in enumerate(rows_of):
                    o_tile = o[SUBLANES * t:SUBLANES * (t + 1), :]
                    ob_ref[hd, rows, :] = (_rms(o_tile) * hng) * g_ref[hd, rows, :]

        for ci in range(ts // CHUNK):
            scan_chunk(ci)
        for ci in range(ts // CHUNK):
            output_chunk(ci)
        gated = jnp.concatenate([ob_ref[hd] for hd in range(HEADS)], axis=-1)
        branch_b = _bdot(gated, who_ref[...])

        merged = _sigmoid(proj(SEC_MGP)) * branch_a + _sigmoid(proj(SEC_MGH)) * branch_b
        y = _bdot(merged, wout_ref[...])
        o_ref[pl.ds(r0, SEQ_TILE), :] = x + _rms(y) * (gate * gpost_ref[...])
        return carry

    lax.fori_loop(0, GRID_TILE // SEQ_TILE, tile_pass, 0)


def _resident(shape, index_map):
    return pl.BlockSpec(shape, index_map, pipeline_mode=pl.Buffered(1))


def _nbytes(shape, dtype):
    return math.prod(shape) * jnp.dtype(dtype).itemsize


def _layer_call(layer, x, ada, g_pre, g_post, lb_logits, pool_scale, hgrn_norm_g,
                w_in, pool_w, w_pool_o, w_hgrn_o, w_out):
    batch, seq, d = x.shape
    depth = lb_logits.shape[0]
    ts = SEQ_TILE
    assert seq % GRID_TILE == 0 and GRID_TILE % ts == 0 and ts % CHUNK == 0 and d == D_MODEL
    chunks = ts // CHUNK

    def const2(b, s):
        return (0, 0)

    def per_layer_row(a):
        return a.reshape(depth, 1, a.shape[-1])

    def per_layer3(b, s):
        return (layer, 0, 0)

    in_specs = [
        pl.BlockSpec((None, GRID_TILE, d), lambda b, s: (b, s, 0)),
        pl.BlockSpec((None, None, 3, d), lambda b, s: (layer, b, 0, 0)),
        _resident((None, 1, d), per_layer3),
        _resident((None, 1, d), per_layer3),
        _resident((depth, HGRN_WIDTH), const2),
        _resident((None, 1, POOL_WIDTH), per_layer3),
        _resident((None, 1, HEAD_DIM), per_layer3),
        _resident((None, d, sum(IN_SIZES)), per_layer3),
        _resident((None, POOL_GROUPS, POOL_GROUP_DIM, POOL_GROUP_DIM), lambda b, s: (layer, 0, 0, 0)),
        _resident((None, POOL_WIDTH, d), per_layer3),
        _resident((None, HGRN_WIDTH, d), per_layer3),
        _resident((None, d, d), per_layer3),
    ]
    scratch = [
        pltpu.VMEM((HEADS, ts, HEAD_DIM), jnp.float32),
        pltpu.VMEM((HEADS, ts, HEAD_DIM), jnp.float32),
        pltpu.VMEM((HEADS, ts, HEAD_DIM), jnp.float32),
        pltpu.VMEM((HEADS, ts, HEAD_DIM), jnp.float32),
        pltpu.VMEM((HEADS, ts, HEAD_DIM), jnp.float32),
        pltpu.VMEM((HALO + ts, POOL_WIDTH), jnp.float32),
        pltpu.VMEM((HEADS, HEAD_DIM, HEAD_DIM), jnp.float32),
        pltpu.VMEM((chunks, HEADS, N_OPS, CHUNK, HEAD_DIM), jnp.bfloat16),
        pltpu.VMEM((chunks, HEADS, N_OPS, CHUNK, HEAD_DIM), jnp.bfloat16),
        pltpu.VMEM((chunks, HEADS, CHUNK, HEAD_DIM), jnp.bfloat16),
        pltpu.VMEM((chunks, HEADS, SUBLANES, HEAD_DIM), jnp.float32),
        pltpu.VMEM((chunks, HEADS, CHUNK, HEAD_DIM), jnp.float32),
    ]
    weights = (w_in, pool_w, w_pool_o, w_hgrn_o, w_out)
    vmem_bytes = (sum(_nbytes(w.shape[1:], w.dtype) for w in weights)
                  + 2 * 2 * _nbytes((GRID_TILE, d), x.dtype)
                  + sum(_nbytes(s.shape, s.dtype) for s in scratch)
                  + COMPILER_TEMP_BYTES)
    assert vmem_bytes <= V7X_VMEM_BYTES, vmem_bytes
    return pl.pallas_call(
        functools.partial(_layer_kernel, layer=layer),
        grid=(batch, seq // GRID_TILE),
        in_specs=in_specs,
        out_specs=pl.BlockSpec((None, GRID_TILE, d), lambda b, s: (b, s, 0)),
        out_shape=jax.ShapeDtypeStruct(x.shape, x.dtype),
        scratch_shapes=scratch,
        compiler_params=pltpu.CompilerParams(
            dimension_semantics=("arbitrary", "arbitrary"),
            vmem_limit_bytes=vmem_bytes),
        name=f"hybrid_layer{layer}",
    )(x, ada, per_layer_row(g_pre), per_layer_row(g_post), lb_logits,
      per_layer_row(pool_scale), per_layer_row(hgrn_norm_g),
      w_in, pool_w, w_pool_o, w_hgrn_o, w_out)


def _ada_call(c, w_ada, b_ada):
    depth, d, width = w_ada.shape
    batch = c.shape[0]
    assert width % ADA_TILE == 0
    return pl.pallas_call(
        _ada_kernel,
        grid=(depth, width // ADA_TILE),
        in_specs=[
            pl.BlockSpec((batch, d), lambda l, n: (0, 0)),
            pl.BlockSpec((None, d, ADA_TILE), lambda l, n: (l, 0, n)),
            pl.BlockSpec((None, 1, ADA_TILE), lambda l, n: (l, 0, n)),
        ],
        out_specs=pl.BlockSpec((None, batch, ADA_TILE), lambda l, n: (l, 0, n)),
        out_shape=jax.ShapeDtypeStruct((depth, batch, width), jnp.float32),
        name="adaln_vectors",
    )(c, w_ada, b_ada.reshape(depth, 1, width))


def kernel(x, c, w_ada, b_ada, g_pre, g_post, w_in, pool_w, pool_scale, lb_logits,
           hgrn_norm_g, w_pool_o, w_hgrn_o, w_out):
    depth = w_in.shape[0]
    batch, _, d = x.shape
    ada = _ada_call(c, w_ada, b_ada).reshape(depth, batch, 3, d)
    bf = jnp.bfloat16
    w_in_b, pool_w_b = w_in.astype(bf), pool_w.astype(bf)
    w_pool_o_b, w_hgrn_o_b, w_out_b = w_pool_o.astype(bf), w_hgrn_o.astype(bf), w_out.astype(bf)
    for layer in range(depth):
        x = _layer_call(layer, x, ada, g_pre, g_post, lb_logits, pool_scale, hgrn_norm_g,
                        w_in_b, pool_w_b, w_pool_o_b, w_hgrn_o_b, w_out_b)
    return x
```

```python
import functools
import math

import jax
import jax.numpy as jnp
from jax import lax
from jax.experimental import pallas as pl
from jax.experimental.pallas import tpu as pltpu

D_MODEL = 1024
POOL_WINDOWS = (2, 4, 8, 16)
POOL_GROUPS = len(POOL_WINDOWS)
POOL_GROUP_DIM = D_MODEL // 8
POOL_WIDTH = POOL_GROUPS * POOL_GROUP_DIM
HEAD_DIM = 128
HEADS = D_MODEL // HEAD_DIM
HGRN_WIDTH = HEADS * HEAD_DIM
NORM_EPS = 1e-6
F_FLOOR = 1e-30
NEG_LOG2_E = -1.4426950408889634
IN_SIZES = (POOL_WIDTH, POOL_WIDTH, HGRN_WIDTH, HGRN_WIDTH, HGRN_WIDTH, HGRN_WIDTH, D_MODEL, D_MODEL)
IN_OFFSETS = tuple(sum(IN_SIZES[:i]) for i in range(len(IN_SIZES)))
(SEC_PV, SEC_PG, SEC_HQ, SEC_HF, SEC_HI, SEC_HG, SEC_MGP, SEC_MGH) = range(len(IN_SIZES))

GRID_TILE = 512
SEQ_TILE = 256
CHUNK = 64
HALO = max(POOL_WINDOWS)
SUBLANES = 8
TILES = CHUNK // SUBLANES
TILE_BITS = (0, 1, 5)
SUBLANE_BITS = (2, 3, 4)
TIME_BITS = 6
TILES_PER_GROUP = 4
ROW_STRIDE = TILES_PER_GROUP
GROUP_TOKENS = TILES_PER_GROUP * SUBLANES
assert CHUNK == 1 << TIME_BITS
LOCAL_BITS = 2
assert TILE_BITS[:LOCAL_BITS] == tuple(range(LOCAL_BITS)) and TILES_PER_GROUP == 1 << LOCAL_BITS
N_OPS = TIME_BITS - LOCAL_BITS + 1
ADA_TILE = 1024
V7X_VMEM_BYTES = 64 * 1024 * 1024
COMPILER_TEMP_BYTES = 3 * SEQ_TILE * D_MODEL * 4 * 4

NT = (((1,), (1,)), ((), ()))
TN = (((0,), (0,)), ((), ()))


def _bdot(a, b):
    return jnp.dot(a.astype(jnp.bfloat16), b.astype(jnp.bfloat16),
                   preferred_element_type=jnp.float32)


def _sigmoid(z):
    return 1.0 / (1.0 + jnp.exp2(z * NEG_LOG2_E))


def _silu(z):
    return z * _sigmoid(z)


def _rms(x):
    return x * lax.rsqrt(jnp.mean(x * x, axis=-1, keepdims=True) + NORM_EPS)


def _roll_rows(a, shift):
    return pltpu.roll(a, shift % a.shape[0], axis=0)


def _token_of_row(p):
    group, j, i = p // GROUP_TOKENS, (p // SUBLANES) % TILES_PER_GROUP, p % SUBLANES
    return group * GROUP_TOKENS + i * ROW_STRIDE + j


def _ada_kernel(c_ref, w_ref, b_ref, o_ref):
    c = c_ref[...]
    o_ref[...] = jnp.dot(_silu(c), w_ref[...], preferred_element_type=jnp.float32) + b_ref[...]


def _sublane_partner(p, sb, isub):
    k = 1
    while k < sb:
        p = jnp.where((isub & k) != 0, p, _roll_rows(p, -k))
        k *= 2
    if 2 * sb == p.shape[0]:
        return _roll_rows(p, sb)
    return jnp.where((isub & sb) != 0, _roll_rows(p, sb), _roll_rows(p, -sb))


def _last_row(a):
    return jnp.broadcast_to(a[SUBLANES - 1:SUBLANES, :], a.shape)


def _lane_sum(a):
    return jnp.sum(a, axis=-1, keepdims=True)


def _local_outputs(q, pf, kk, v):
    out = []
    for base in range(0, TILES, TILES_PER_GROUP):
        qs, fs, ks, vs = (a[base:base + TILES_PER_GROUP] for a in (q, pf, kk, v))
        acc = [_lane_sum(qs[j] * ks[j]) * vs[j] for j in range(TILES_PER_GROUP)]
        decayed_k = {}
        for j_t in range(1, TILES_PER_GROUP):
            qf = qs[j_t] * fs[j_t]
            for j_s in range(j_t - 1, -1, -1):
                decayed_k[j_t, j_s] = (ks[j_s] if j_s == j_t - 1
                                       else decayed_k[j_t - 1, j_s] * fs[j_t - 1])
                acc[j_t] = acc[j_t] + _lane_sum(qf * decayed_k[j_t, j_s]) * vs[j_s]
        out.extend(acc)
    return out


def _scan_operands(q, f, v, isub):
    tiles = range(TILES)
    kk = [1.0 - f[t] for t in tiles]
    pre = [jnp.maximum(f[t], F_FLOOR) for t in tiles]
    ks = list(kk)
    local = _local_outputs(q, pre, kk, v)
    lhs, rhs = [], []
    for bit in range(TIME_BITS):
        if bit in TILE_BITS:
            tb = 1 << TILE_BITS.index(bit)
            if bit >= LOCAL_BITS:
                lhs.append([q[t] * pre[t] for t in tiles if t & tb])
                rhs.append([ks[t] for t in tiles if not t & tb])
        else:
            sb = 1 << SUBLANE_BITS.index(bit)
            upper = (isub & sb) != 0
            lhs.append([jnp.where(upper, q[t] * pre[t], ks[t]) for t in tiles])
            rhs.append(None)
        new_pre, new_ks = list(pre), list(ks)
        if bit in TILE_BITS:
            totals = {}
            for t in tiles:
                src = (t ^ tb) | (tb - 1)
                if src not in totals:
                    totals[src] = _last_row(pre[src]) if bit > max(SUBLANE_BITS) else pre[src]
                if t & tb:
                    new_pre[t] = pre[t] * totals[src]
                else:
                    new_ks[t] = ks[t] * totals[src]
        else:
            for base in range(0, TILES, TILES_PER_GROUP):
                partner = _sublane_partner(pre[base + TILES_PER_GROUP - 1], sb, isub)
                pre_mul = jnp.where(upper, partner, 1.0)
                ks_mul = jnp.where(upper, 1.0, partner)
                for t in range(base, base + TILES_PER_GROUP):
                    new_pre[t] = pre[t] * pre_mul
                    new_ks[t] = ks[t] * ks_mul
        pre, ks = new_pre, new_ks
    lhs.append([q[t] * pre[t] for t in tiles])
    rhs.append(ks)
    return lhs, rhs, pre[TILES - 1], local


def _layer_kernel(x_ref, ada_ref, gpre_ref, gpost_ref, lbl_ref, pscale_ref, hng_ref,
                  win_ref, poolw_ref, wpo_ref, who_ref, wout_ref, o_ref,
                  q_ref, f_ref, v_ref, g_ref, ob_ref, ext_ref, state_ref,
                  lhs_ref, rhs_ref, vb_ref, tot_ref, loc_ref, *, layer):
    ts = SEQ_TILE
    si = pl.program_id(1)

    @pl.when(si == 0)
    def _():
        state_ref[...] = jnp.zeros_like(state_ref)
        ext_ref[0:HALO, :] = jnp.zeros((HALO, POOL_WIDTH), jnp.float32)

    def tile_pass(part, carry):
        r0 = pl.multiple_of(part * SEQ_TILE, SEQ_TILE)
        x = x_ref[pl.ds(r0, SEQ_TILE), :]
        shift = ada_ref[0:1, :]
        scale = ada_ref[1:2, :]
        gate = ada_ref[2:3, :]
        hb = (_rms(x) * (gpre_ref[...] * (1.0 + scale)) + shift).astype(jnp.bfloat16)

        def proj(section):
            cols = slice(IN_OFFSETS[section], IN_OFFSETS[section] + IN_SIZES[section])
            return jnp.dot(hb, win_ref[:, cols], preferred_element_type=jnp.float32)

        logits = lbl_ref[...]
        e = jnp.exp(logits - jnp.max(logits, axis=0, keepdims=True))
        lb = jnp.zeros((1, HGRN_WIDTH), jnp.float32)
        for j in range(1, layer + 1):
            lb = lb + e[j:j + 1, :]
        lb = jnp.clip(lb / jnp.sum(e, axis=0, keepdims=True), 0.0, 1.0)

        def to_heads(ref, a):
            for hd in range(HEADS):
                ref[hd] = a[:, hd * HEAD_DIM:(hd + 1) * HEAD_DIM]

        to_heads(q_ref, _silu(proj(SEC_HQ)))
        to_heads(f_ref, lb + (1.0 - lb) * _sigmoid(proj(SEC_HF)))
        to_heads(v_ref, proj(SEC_HI))
        to_heads(g_ref, _silu(proj(SEC_HG)))

        u = proj(SEC_PV)
        ext_ref[HALO:HALO + ts, :] = u
        first = si * GRID_TILE + r0 + 1
        pos = (first + lax.broadcasted_iota(jnp.int32, (ts, 1), 0)).astype(jnp.float32)
        mixed = []
        for gi, w in enumerate(POOL_WINDOWS):
            cols = slice(gi * POOL_GROUP_DIM, (gi + 1) * POOL_GROUP_DIM)
            acc = ext_ref[:, cols]
            step = 1
            while step < w:
                acc = acc + _roll_rows(acc, step)
                step *= 2
            inv_count = 1.0 / jnp.minimum(pos, float(w))
            pooled = acc[HALO:, :] * inv_count - u[:, cols]
            mixed.append(_bdot(pooled, poolw_ref[gi]))
        ext_ref[0:HALO, :] = u[ts - HALO:, :]
        pooled = jnp.concatenate(mixed, axis=-1) * pscale_ref[...]
        branch_a = _bdot(pooled * _silu(proj(SEC_PG)), wpo_ref[...])

        isub = lax.broadcasted_iota(jnp.int32, (SUBLANES, HEAD_DIM), 0)
        cat = functools.partial(jnp.concatenate, axis=0)

        def tile_rows(ci):
            return [pl.ds(ci * CHUNK + GROUP_TOKENS * (t // TILES_PER_GROUP) + t % TILES_PER_GROUP, SUBLANES,
                          stride=ROW_STRIDE) for t in range(TILES)]

        def scan_chunk(ci):
            rows_of = tile_rows(ci)
            for hd in range(HEADS):
                q, f, v = ([ref[hd, rows, :] for rows in rows_of] for ref in (q_ref, f_ref, v_ref))
                lhs, rhs, total, local = _scan_operands(q, f, v, isub)
                for i in range(N_OPS):
                    used = slice(0, SUBLANES * len(lhs[i]))
                    lhs_ref[ci, hd, i, used] = cat(lhs[i]).astype(jnp.bfloat16)
                    if rhs[i] is not None:
                        rhs_ref[ci, hd, i, used] = cat(rhs[i]).astype(jnp.bfloat16)
                vb_ref[ci, hd] = cat(v).astype(jnp.bfloat16)
                tot_ref[ci, hd] = _last_row(total)
                loc_ref[ci, hd] = cat(local)

        r2 = _token_of_row(lax.broadcasted_iota(jnp.int32, (CHUNK, CHUNK), 0))
        c2 = _token_of_row(lax.broadcasted_iota(jnp.int32, (CHUNK, CHUNK), 1))
        pair_of = [((r2 >> (bit + 1)) == (c2 >> (bit + 1))) & (((r2 >> bit) & 1) == 1) & (((c2 >> bit) & 1) == 0)
                   for bit in range(TIME_BITS)]
        hng = hng_ref[...]

        def output_chunk(ci):
            rows_of = tile_rows(ci)
            heads = range(HEADS)

            def nt(hd, i):
                both = lhs_ref[ci, hd, i]
                return lax.dot_general(both, both, NT, preferred_element_type=jnp.float32)

            scores = [None] * HEADS
            for bit in range(LOCAL_BITS, TIME_BITS - 1):
                for hd in heads:
                    p = jnp.where(pair_of[bit], nt(hd, bit - LOCAL_BITS), 0.0)
                    scores[hd] = p if scores[hd] is None else scores[hd] + p
            half = CHUNK // 2
            top = [lax.dot_general(lhs_ref[ci, hd, N_OPS - 2, 0:half], rhs_ref[ci, hd, N_OPS - 2, 0:half], NT,
                                   preferred_element_type=jnp.float32) for hd in heads]
            state_t = [state_ref[hd] for hd in heads]
            o_inter = [lax.dot_general(lhs_ref[ci, hd, N_OPS - 1], state_t[hd].astype(jnp.bfloat16), NT,
                                       preferred_element_type=jnp.float32) for hd in heads]
            for hd in heads:
                state_ref[hd] = (state_t[hd] * tot_ref[ci, hd][0:1, :]
                                 + lax.dot_general(vb_ref[ci, hd], rhs_ref[ci, hd, N_OPS - 1], TN,
                                                   preferred_element_type=jnp.float32))
            o_tops = [jnp.dot(top[hd].astype(jnp.bfloat16), vb_ref[ci, hd, 0:half],
                              preferred_element_type=jnp.float32) for hd in heads]
            for hd in heads:
                o = (o_inter[hd] + loc_ref[ci, hd]
                     + jnp.dot(scores[hd].astype(jnp.bfloat16), vb_ref[ci, hd],
                               preferred_element_type=jnp.float32))
                o = jnp.concatenate([o[0:half], o[half:] + o_tops[hd]], axis=0)
                for t, rows in enumerate(rows_of):
                    o_tile = o[SUBLANES * t:SUBLANES * (t + 1), :]
                    ob_ref[hd, rows, :] = (_rms(o_tile) * hng) * g_ref[hd, rows, :]

        for ci in range(ts // CHUNK):
            scan_chunk(ci)
        for ci in range(ts // CHUNK):
            output_chunk(ci)
        gated = jnp.concatenate([ob_ref[hd] for hd in range(HEADS)], axis=-1)
        branch_b = _bdot(gated, who_ref[...])

        merged = _sigmoid(proj(SEC_MGP)) * branch_a + _sigmoid(proj(SEC_MGH)) * branch_b
        y = _bdot(merged, wout_ref[...])
        o_ref[pl.ds(r0, SEQ_TILE), :] = x + _rms(y) * (gate * gpost_ref[...])
        return carry

    lax.fori_loop(0, GRID_TILE // SEQ_TILE, tile_pass, 0)


def _resident(shape, index_map):
    return pl.BlockSpec(shape, index_map, pipeline_mode=pl.Buffered(1))


def _nbytes(shape, dtype):
    return math.prod(shape) * jnp.dtype(dtype).itemsize


def _layer_call(layer, x, ada, g_pre, g_post, lb_logits, pool_scale, hgrn_norm_g,
                w_in, pool_w, w_pool_o, w_hgrn_o, w_out):
    batch, seq, d = x.shape
    depth = lb_logits.shape[0]
    ts = SEQ_TILE
    assert seq % GRID_TILE == 0 and GRID_TILE % ts == 0 and ts % CHUNK == 0 and d == D_MODEL
    chunks = ts // CHUNK

    def const2(b, s):
        return (0, 0)

    def per_layer_row(a):
        return a.reshape(depth, 1, a.shape[-1])

    def per_layer3(b, s):
        return (layer, 0, 0)

    in_specs = [
        pl.BlockSpec((None, GRID_TILE, d), lambda b, s: (b, s, 0)),
        pl.BlockSpec((None, None, 3, d), lambda b, s: (layer, b, 0, 0)),
        _resident((None, 1, d), per_layer3),
        _resident((None, 1, d), per_layer3),
        _resident((depth, HGRN_WIDTH), const2),
        _resident((None, 1, POOL_WIDTH), per_layer3),
        _resident((None, 1, HEAD_DIM), per_layer3),
        _resident((None, d, sum(IN_SIZES)), per_layer3),
        _resident((None, POOL_GROUPS, POOL_GROUP_DIM, POOL_GROUP_DIM), lambda b, s: (layer, 0, 0, 0)),
        _resident((None, POOL_WIDTH, d), per_layer3),
        _resident((None, HGRN_WIDTH, d), per_layer3),
        _resident((None, d, d), per_layer3),
    ]
    scratch = [
        pltpu.VMEM((HEADS, ts, HEAD_DIM), jnp.float32),
        pltpu.VMEM((HEADS, ts, HEAD_DIM), jnp.float32),
        pltpu.VMEM((HEADS, ts, HEAD_DIM), jnp.float32),
        pltpu.VMEM((HEADS, ts, HEAD_DIM), jnp.float32),
        pltpu.VMEM((HEADS, ts, HEAD_DIM), jnp.float32),
        pltpu.VMEM((HALO + ts, POOL_WIDTH), jnp.float32),
        pltpu.VMEM((HEADS, HEAD_DIM, HEAD_DIM), jnp.float32),
        pltpu.VMEM((chunks, HEADS, N_OPS, CHUNK, HEAD_DIM), jnp.bfloat16),
        pltpu.VMEM((chunks, HEADS, N_OPS, CHUNK, HEAD_DIM), jnp.bfloat16),
        pltpu.VMEM((chunks, HEADS, CHUNK, HEAD_DIM), jnp.bfloat16),
        pltpu.VMEM((chunks, HEADS, SUBLANES, HEAD_DIM), jnp.float32),
        pltpu.VMEM((chunks, HEADS, CHUNK, HEAD_DIM), jnp.float32),
    ]
    weights = (w_in, pool_w, w_pool_o, w_hgrn_o, w_out)
    vmem_bytes = (sum(_nbytes(w.shape[1:], w.dtype) for w in weights)
                  + 2 * 2 * _nbytes((GRID_TILE, d), x.dtype)
                  + sum(_nbytes(s.shape, s.dtype) for s in scratch)
                  + COMPILER_TEMP_BYTES)
    assert vmem_bytes <= V7X_VMEM_BYTES, vmem_bytes
    return pl.pallas_call(
        functools.partial(_layer_kernel, layer=layer),
        grid=(batch, seq // GRID_TILE),
        in_specs=in_specs,
        out_specs=pl.BlockSpec((None, GRID_TILE, d), lambda b, s: (b, s, 0)),
        out_shape=jax.ShapeDtypeStruct(x.shape, x.dtype),
        scratch_shapes=scratch,
        compiler_params=pltpu.CompilerParams(
            dimension_semantics=("arbitrary", "arbitrary"),
            vmem_limit_bytes=vmem_bytes),
        name=f"hybrid_layer{layer}",
    )(x, ada, per_layer_row(g_pre), per_layer_row(g_post), lb_logits,
      per_layer_row(pool_scale), per_layer_row(hgrn_norm_g),
      w_in, pool_w, w_pool_o, w_hgrn_o, w_out)


def _ada_call(c, w_ada, b_ada):
    depth, d, width = w_ada.shape
    batch = c.shape[0]
    assert width % ADA_TILE == 0
    return pl.pallas_call(
        _ada_kernel,
        grid=(depth, width // ADA_TILE),
        in_specs=[
            pl.BlockSpec((batch, d), lambda l, n: (0, 0)),
            pl.BlockSpec((None, d, ADA_TILE), lambda l, n: (l, 0, n)),
            pl.BlockSpec((None, 1, ADA_TILE), lambda l, n: (l, 0, n)),
        ],
        out_specs=pl.BlockSpec((None, batch, ADA_TILE), lambda l, n: (l, 0, n)),
        out_shape=jax.ShapeDtypeStruct((depth, batch, width), jnp.float32),
        name="adaln_vectors",
    )(c, w_ada, b_ada.reshape(depth, 1, width))


def kernel(x, c, w_ada, b_ada, g_pre, g_post, w_in, pool_w, pool_scale, lb_logits,
           hgrn_norm_g, w_pool_o, w_hgrn_o, w_out):
    depth = w_in.shape[0]
    batch, _, d = x.shape
    ada = _ada_call(c, w_ada, b_ada).reshape(depth, batch, 3, d)
    bf = jnp.bfloat16
    w_in_b, pool_w_b = w_in.astype(bf), pool_w.astype(bf)
    w_pool_o_b, w_hgrn_o_b, w_out_b = w_pool_o.astype(bf), w_hgrn_o.astype(bf), w_out.astype(bf)
    for layer in range(depth):
        x = _layer_call(layer, x, ada, g_pre, g_post, lb_logits, pool_scale, hgrn_norm_g,
                        w_in_b, pool_w_b, w_pool_o_b, w_hgrn_o_b, w_out_b)
    return x
```
